```python
import math
import jax
import jax.numpy as jnp
from jax import lax
import numpy as np

D_MODEL = 1024
BATCH = 8
SEQ = 4096
DEPTH = 2

CTX_LEN = 256
GRID_W = 64
EPS = 1e-6
ROPE_THETA = 10000.0
Q_BLOCK = 128
WINDOW = 128
NEG = -1e30

HA = 8
A_NOPE = 64
A_ROPE = 32
A_V = 64
A_Q_RANK = 256
A_KV_RANK = 128
HB = 8
HB_KV = 2
B_HD = 64
HC = 8
HC_KV = 2
C_HD = 64

BRANCH_W = 512
N_BRANCH = 3
D_FF = 2816
CONV_W = 3

IN_SIZES = (A_Q_RANK, A_KV_RANK, A_ROPE, HB * B_HD, HB_KV * B_HD, HB_KV * B_HD, HC * C_HD, HC_KV * C_HD, HC_KV * C_HD, N_BRANCH * D_MODEL)
IN_COLS = sum(IN_SIZES)

kernel_name = 'hybrid_mla_gqa_swa_convffn_trunk'


def rms_norm(x, g):
    xf = x.astype(jnp.float32)
    y = xf * lax.rsqrt(jnp.mean(xf * xf, axis=-1, keepdims=True) + EPS)
    return (y * g.astype(jnp.float32)).astype(x.dtype)


def modulate(x, g, shift, scale):
    return rms_norm(x, g) * (1 + scale) + shift


def axial_rope_table(rows, head_dim):
    n_freq = head_dim // 4
    row = jnp.repeat(jnp.arange(rows, dtype=jnp.float32), GRID_W)
    col = jnp.tile(jnp.arange(GRID_W, dtype=jnp.float32), rows)
    inv = ROPE_THETA ** (-jnp.arange(n_freq, dtype=jnp.float32) / n_freq)
    ang = jnp.stack([row[:, None] * inv, col[:, None] * inv], axis=1)
    return jnp.cos(ang), jnp.sin(ang)


def apply_rope(x, table):
    if table is None:
        return x
    cos, sin = table
    shp = x.shape
    nf = shp[-1] // 4
    xr = x.reshape(shp[:-1] + (2, 2, nf))
    bshape = (1, shp[1]) + (1,) * (x.ndim - 3) + (2, nf)
    c = cos.reshape(bshape).astype(x.dtype)
    s = sin.reshape(bshape).astype(x.dtype)
    x0 = xr[..., 0, :]
    x1 = xr[..., 1, :]
    return jnp.stack([x0 * c - x1 * s, x1 * c + x0 * s], axis=-2).reshape(shp)


def blocked_attention(q, k, v, scale, sink=None):
    b, lq, hkv, g, dk = q.shape
    nblk = lq // Q_BLOCK
    n_keys = k.shape[1]
    qb = q.reshape(b, nblk, Q_BLOCK, hkv, g, dk).transpose(1, 0, 2, 3, 4, 5)

    def one_block(qblk):
        s = jnp.einsum('bqhgd,bkhd->bhgqk', qblk, k).astype(jnp.float32) * scale
        if sink is not None:
            sk = jnp.broadcast_to(sink.astype(jnp.float32).reshape(1, hkv, g, 1, 1), s.shape[:-1] + (1,))
            s = jnp.concatenate([s, sk], axis=-1)
        p = jax.nn.softmax(s, axis=-1)[..., :n_keys].astype(v.dtype)
        return jnp.einsum('bhgqk,bkhd->bqhgd', p, v)

    out = lax.map(one_block, qb)
    return out.transpose(1, 0, 2, 3, 4, 5).reshape(b, lq, hkv * g * v.shape[-1])


def window_attention(q, k, v, k_ctx, v_ctx, sink, scale):
    b, n, hkv, g, d = q.shape
    nblk = n // Q_BLOCK
    n_ctx = k_ctx.shape[1]
    band = 3 * Q_BLOCK
    qb = q.reshape(b, nblk, Q_BLOCK, hkv, g, d).transpose(1, 0, 2, 3, 4, 5)

    def banded(t):
        tp = jnp.pad(t, ((0, 0), (Q_BLOCK, Q_BLOCK), (0, 0), (0, 0)))
        tp = tp.reshape(b, nblk + 2, Q_BLOCK, hkv, t.shape[-1])
        return jnp.concatenate([tp[:, :-2], tp[:, 1:-1], tp[:, 2:]], axis=2).transpose(1, 0, 2, 3, 4)

    kw = banded(k)
    vw = banded(v)
    blk = jnp.arange(nblk)[:, None, None]
    qi = jnp.arange(Q_BLOCK)[None, :, None]
    kj = jnp.arange(band)[None, None, :]
    rel = kj - qi
    j_abs = blk * Q_BLOCK - Q_BLOCK + kj
    valid = (rel >= Q_BLOCK - WINDOW) & (rel <= Q_BLOCK + WINDOW) & (j_abs >= 0) & (j_abs < n)
    sink_f = sink.astype(jnp.float32).reshape(1, hkv, g, 1, 1)

    def one_block(args):
        qblk, kblk, vblk, mask = args
        s_ctx = jnp.einsum('bqhgd,bkhd->bhgqk', qblk, k_ctx).astype(jnp.float32) * scale
        s_win = jnp.einsum('bqhgd,bkhd->bhgqk', qblk, kblk).astype(jnp.float32) * scale
        s_win = jnp.where(mask, s_win, NEG)
        s_sink = jnp.broadcast_to(sink_f, s_win.shape[:-1] + (1,))
        p = jax.nn.softmax(jnp.concatenate([s_ctx, s_win, s_sink], axis=-1), axis=-1).astype(v.dtype)
        return (jnp.einsum('bhgqk,bkhd->bqhgd', p[..., :n_ctx], v_ctx)
                + jnp.einsum('bhgqk,bkhd->bqhgd', p[..., n_ctx:n_ctx + band], vblk))

    out = lax.map(one_block, (qb, kw, vw, valid))
    return out.transpose(1, 0, 2, 3, 4, 5).reshape(b, n, hkv * g * v.shape[-1])


def project_mixers(h, w_in, b_gate, g_q_a, w_q_b, g_kv_a, w_kv_b, g_qn, g_kn, rope_a, rope_h):
    b, n, _ = h.shape
    cuts = []
    acc = 0
    for size in IN_SIZES[:-1]:
        acc += size
        cuts.append(acc)
    aq, akv, akr, bq, bk, bv, cq, ck, cv, gl = jnp.split(h @ w_in, cuts, axis=-1)
    q = (rms_norm(aq, g_q_a) @ w_q_b).reshape(b, n, HA, A_NOPE + A_ROPE)
    kv = (rms_norm(akv, g_kv_a) @ w_kv_b).reshape(b, n, HA, A_NOPE + A_V)
    k_rope = apply_rope(akr, rope_a)
    q_a = jnp.concatenate([q[..., :A_NOPE], apply_rope(q[..., A_NOPE:], rope_a)], axis=-1)[:, :, :, None, :]
    k_a = jnp.concatenate([kv[..., :A_NOPE], jnp.broadcast_to(k_rope[:, :, None, :], (b, n, HA, A_ROPE))], axis=-1)
    v_a = kv[..., A_NOPE:]
    q_b = apply_rope(rms_norm(bq.reshape(b, n, HB_KV, HB // HB_KV, B_HD), g_qn), rope_h)
    k_b = apply_rope(rms_norm(bk.reshape(b, n, HB_KV, B_HD), g_kn), rope_h)
    v_b = bv.reshape(b, n, HB_KV, B_HD)
    q_c = apply_rope(cq.reshape(b, n, HC_KV, HC // HC_KV, C_HD), rope_h)
    k_c = apply_rope(ck.reshape(b, n, HC_KV, C_HD), rope_h)
    v_c = cv.reshape(b, n, HC_KV, C_HD)
    gates = jax.nn.sigmoid(gl + b_gate).reshape(b, n, N_BRANCH, D_MODEL)
    return (q_a, k_a, v_a, q_b, k_b, v_b, q_c, k_c, v_c, gates)


def merge_branches(outs, gates, w_branch, w_out):
    y = sum(gates[:, :, i] * (o @ w_branch[i]) for i, o in enumerate(outs))
    return y @ w_out


def conv_ffn(h, w_up, w_conv, b_conv, w_down):
    n = h.shape[1]
    half = CONV_W // 2
    u = jnp.pad(h @ w_up, ((0, 0), (half, half), (0, 0)))
    u = sum(u[:, j:j + n] * w_conv[j] for j in range(CONV_W)) + b_conv
    a, gv = jnp.split(u, 2, axis=-1)
    return (jax.nn.silu(a) * gv) @ w_down


def setup_inputs(seed: int = 0) -> dict:
    key = jax.random.key(seed)
    ks = jax.random.split(key, 24)

    def nrm(k, shape, scale):
        return jax.random.normal(k, shape, jnp.float32) * scale

    def gain(k, shape):
        return 1.0 + 0.02 * jax.random.normal(k, shape, jnp.float32)

    return {
        'x': nrm(ks[0], (BATCH, SEQ, D_MODEL), 1.0),
        'c': nrm(ks[1], (BATCH, D_MODEL), 1.0),
        'ctx': nrm(ks[2], (BATCH, CTX_LEN, D_MODEL), 1.0),
        'c_ctx': nrm(ks[3], (D_MODEL,), 1.0),
        'w_mod': nrm(ks[4], (DEPTH, D_MODEL, 6 * D_MODEL), 0.5 * D_MODEL ** -0.5),
        'b_mod': nrm(ks[5], (DEPTH, 6 * D_MODEL), 0.01),
        'g_norm1': gain(ks[6], (DEPTH, D_MODEL)),
        'w_in': nrm(ks[7], (DEPTH, D_MODEL, IN_COLS), D_MODEL ** -0.5),
        'b_gate': nrm(ks[8], (DEPTH, N_BRANCH * D_MODEL), 0.02),
        'g_q_a': gain(ks[9], (DEPTH, A_Q_RANK)),
        'w_q_b': nrm(ks[10], (DEPTH, A_Q_RANK, HA * (A_NOPE + A_ROPE)), A_Q_RANK ** -0.5),
        'g_kv_a': gain(ks[11], (DEPTH, A_KV_RANK)),
        'w_kv_b': nrm(ks[12], (DEPTH, A_KV_RANK, HA * (A_NOPE + A_V)), A_KV_RANK ** -0.5),
        'g_qn': gain(ks[13], (DEPTH, B_HD)),
        'g_kn': gain(ks[14], (DEPTH, B_HD)),
        'sink': nrm(ks[15], (DEPTH, HC), 0.5),
        'w_branch': nrm(ks[16], (DEPTH, N_BRANCH, BRANCH_W, D_MODEL), BRANCH_W ** -0.5),
        'w_out': nrm(ks[17], (DEPTH, D_MODEL, D_MODEL), D_MODEL ** -0.5),
        'g_norm2': gain(ks[18], (DEPTH, D_MODEL)),
        'w_up': nrm(ks[19], (DEPTH, D_MODEL, 2 * D_FF), D_MODEL ** -0.5),
        'w_conv': nrm(ks[20], (DEPTH, CONV_W, 2 * D_FF), CONV_W ** -0.5),
        'b_conv': nrm(ks[21], (DEPTH, 2 * D_FF), 0.02),
        'w_down': nrm(ks[22], (DEPTH, D_FF, D_MODEL), D_FF ** -0.5),
        'g_final': gain(ks[23], (D_MODEL,)),
    }


def reference(x, c, ctx, c_ctx, w_mod, b_mod, g_norm1, w_in, b_gate, g_q_a, w_q_b, g_kv_a, w_kv_b, g_qn, g_kn, sink, w_branch, w_out, g_norm2, w_up, w_conv, b_conv, w_down, g_final):
    n = x.shape[1]
    rows = n // GRID_W
    rope_a = axial_rope_table(rows, A_ROPE)
    rope_h = axial_rope_table(rows, B_HD)
    scale_a = 1.0 / math.sqrt(A_NOPE + A_ROPE)
    scale_h = 1.0 / math.sqrt(B_HD)
    x_lat = x
    x_ctx = ctx
    for l in range(DEPTH):
        m_lat = jnp.split((jax.nn.silu(c) @ w_mod[l] + b_mod[l])[:, None, :], 6, axis=-1)
        m_ctx = jnp.split(jax.nn.silu(c_ctx) @ w_mod[l] + b_mod[l], 6, axis=-1)
        lw = (w_in[l], b_gate[l], g_q_a[l], w_q_b[l], g_kv_a[l], w_kv_b[l], g_qn[l], g_kn[l])
        qa, ka, va, qb, kb, vb, qc, kc, vc, gates = project_mixers(
            modulate(x_lat, g_norm1[l], m_lat[0], m_lat[1]), *lw, rope_a, rope_h)
        cqa, cka, cva, cqb, ckb, cvb, cqc, ckc, cvc, cgates = project_mixers(
            modulate(x_ctx, g_norm1[l], m_ctx[0], m_ctx[1]), *lw, None, None)
        o_a = blocked_attention(qa, jnp.concatenate([cka, ka], axis=1), jnp.concatenate([cva, va], axis=1), scale_a)
        o_b = blocked_attention(qb, jnp.concatenate([ckb, kb], axis=1), jnp.concatenate([cvb, vb], axis=1), scale_h)
        o_c = window_attention(qc, kc, vc, ckc, cvc, sink[l], scale_h)
        x_lat = x_lat + m_lat[2] * merge_branches((o_a, o_b, o_c), gates, w_branch[l], w_out[l])
        x_lat = x_lat + m_lat[5] * conv_ffn(modulate(x_lat, g_norm2[l], m_lat[3], m_lat[4]), w_up[l], w_conv[l], b_conv[l], w_down[l])
        if l < DEPTH - 1:
            co_a = blocked_attention(cqa, cka, cva, scale_a)
            co_b = blocked_attention(cqb, ckb, cvb, scale_h)
            co_c = blocked_attention(cqc, ckc, cvc, scale_h, sink[l])
            x_ctx = x_ctx + m_ctx[2] * merge_branches((co_a, co_b, co_c), cgates, w_branch[l], w_out[l])
            x_ctx = x_ctx + m_ctx[5] * conv_ffn(modulate(x_ctx, g_norm2[l], m_ctx[3], m_ctx[4]), w_up[l], w_conv[l], b_conv[l], w_down[l])
    return rms_norm(x_lat, g_final)
```

```python
import functools
import math

import jax
import jax.numpy as jnp
import numpy as np
from jax import lax
from jax.experimental import pallas as pl
from jax.experimental.pallas import tpu as pltpu

F32 = jnp.float32
BF16 = jnp.bfloat16

D_MODEL = 1024
GRID_W = 64
EPS = 1e-6
ROPE_THETA = 10000.0
WINDOW = 128
NEG = -1e30
HA, A_NOPE, A_ROPE, A_V, A_Q_RANK, A_KV_RANK = 8, 64, 32, 64, 256, 128
HB, HB_KV, B_HD = 8, 2, 64
HC, HC_KV, C_HD = 8, 2, 64
BRANCH_W = 512
N_BRANCH = 3
D_FF = 2816
CONV_W = 3
N_MOD = 6
IN_SIZES = (A_Q_RANK, A_KV_RANK, A_ROPE, HB * B_HD, HB_KV * B_HD, HB_KV * B_HD, HC * C_HD, HC_KV * C_HD,
            HC_KV * C_HD, N_BRANCH * D_MODEL)

LANES = 128
SUBLANES = 8
BF16_ROWS = 16
TOKEN_TILE = 256
MOD_ROWS = 16
HEAD_PAD = 128
V_ROWS = A_V + BF16_ROWS
FF_CHUNK = 256
VMEM_LIMIT = 56 * 1024 * 1024
LOG2E = 1.4426950408889634


def _dot(a, b):
    return jnp.dot(a, b, preferred_element_type=F32)


def _dot_nt(a, b):
    return lax.dot_general(a, b, (((1,), (1,)), ((), ())), preferred_element_type=F32)


def _dot_tn(a, b):
    return lax.dot_general(a, b, (((0,), (0,)), ((), ())), preferred_element_type=F32)


def _sigmoid(x):
    return 1.0 / (1.0 + jnp.exp(-x))


def _modulated_norm(x, g, shift, scale):
    ms = jnp.mean(x * x, axis=-1, keepdims=True)
    return (x * lax.rsqrt(ms + EPS) * g) * (1.0 + scale) + shift


def _params(n_grid):
    return pltpu.CompilerParams(dimension_semantics=("arbitrary",) * n_grid, vmem_limit_bytes=VMEM_LIMIT)


def _full(shape):
    nd = len(shape)
    return pl.BlockSpec(shape, lambda *_: (0,) * nd)


def _mod_kernel(c_ref, w_ref, b_ref, o_ref):
    c = c_ref[...]
    a = c * _sigmoid(c)
    o_ref[...] = jnp.dot(a, w_ref[...], precision=lax.Precision.HIGHEST, preferred_element_type=F32) + b_ref[...]


def _modulation(cvec, w_mod, b_mod):
    depth = w_mod.shape[0]
    n_col = N_MOD * D_MODEL
    out = pl.pallas_call(
        _mod_kernel,
        grid=(depth, N_MOD),
        in_specs=[
            pl.BlockSpec((MOD_ROWS, D_MODEL), lambda l, j: (0, 0)),
            pl.BlockSpec((None, D_MODEL, D_MODEL), lambda l, j: (l, 0, j)),
            pl.BlockSpec((None, 1, D_MODEL), lambda l, j: (l, 0, j)),
        ],
        out_specs=pl.BlockSpec((None, MOD_ROWS, D_MODEL), lambda l, j: (l, 0, j)),
        out_shape=jax.ShapeDtypeStruct((depth, MOD_ROWS, n_col), F32),
        compiler_params=_params(2),
        name="modulation",
    )(cvec, w_mod, b_mod.reshape(depth, 1, n_col))
    return out.reshape(depth, MOD_ROWS, N_MOD, D_MODEL)


STD_AKV, STD_BK, STD_BKR, STD_CK, STD_CKR, STD_KR, STD_KRR = (i * LANES for i in range(7))
STD_COLS = 7 * LANES
T_AQ = 0
T_AKV = T_AQ + A_Q_RANK
T_BQ = T_AKV + A_KV_RANK
T_CQ = T_BQ + HB * B_HD
T_BV = T_CQ + HC * C_HD
T_CV = T_BV + HB_KV * B_HD
T_ROWS = T_CV + HC_KV * C_HD


def _swap_halves(x, nf):
    parts = []
    for a in range(2):
        base = a * 2 * nf
        parts += [x[base + nf:base + 2 * nf], x[base:base + nf]]
    return jnp.concatenate(parts, axis=0)


def _ones_rows(t):
    row = lax.broadcasted_iota(jnp.int32, (BF16_ROWS, t), 0)
    return jnp.where(row == 0, 1.0, 0.0).astype(BF16)


def _proj_kernel(x_ref, mod_ref, g1_ref, wstd_ref, wt_ref, wqb_ref, wk_ref, wv_ref,
                 gqa_ref, gkvr_ref, gkvc_ref, gqn_ref, gknr_ref, gknrr_ref,
                 ck_ref, sk_ref, ca_ref, sa_ref, ctq_ref, stq_ref, cta_ref, sta_ref,
                 qa_ref, ka_ref, va_ref, qb_ref, kb_ref, vb_ref, qc_ref, kc_ref, vc_ref,
                 *, scale_a, scale_h):
    t = x_ref.shape[0]
    h = _modulated_norm(x_ref[...], g1_ref[...], mod_ref[0:1, :], mod_ref[1:2, :])
    hb = h.astype(BF16)
    ps = _dot(hb, wstd_ref[...])
    pt = _dot_nt(wt_ref[...], hb)
    ones = _ones_rows(t)

    aq = pt[T_AQ:T_AQ + A_Q_RANK]
    aqn = aq * lax.rsqrt(jnp.mean(aq * aq, axis=0, keepdims=True) + EPS) * gqa_ref[...]
    qt = _dot(wqb_ref[...], aqn.astype(BF16))
    cta, sta = cta_ref[...], sta_ref[...]
    qs = scale_a * LOG2E
    for hh in range(HA):
        base = hh * HEAD_PAD
        qa_ref[hh, 0:A_NOPE, :] = (qt[base:base + A_NOPE] * qs).astype(BF16)
        r = qt[base + A_NOPE:base + A_NOPE + A_ROPE]
        rr = r * cta + _swap_halves(r, A_ROPE // 4) * sta
        qa_ref[hh, A_NOPE:A_NOPE + A_ROPE, :] = (rr * qs).astype(BF16)
        qa_ref[hh, A_NOPE + A_ROPE:HEAD_PAD, :] = jnp.zeros((HEAD_PAD - A_NOPE - A_ROPE, t), BF16)

    akv = ps[:, STD_AKV:STD_AKV + LANES]
    akvn = akv * lax.rsqrt(jnp.mean(akv * akv, axis=-1, keepdims=True) + EPS) * gkvr_ref[...]
    kn = _dot(akvn.astype(BF16), wk_ref[...])
    kr = ps[:, STD_KR:STD_KR + LANES] * ca_ref[...] + ps[:, STD_KRR:STD_KRR + LANES] * sa_ref[...]
    for hh in range(HA):
        ka_ref[hh] = (kn[:, hh * HEAD_PAD:(hh + 1) * HEAD_PAD] + kr).astype(BF16)

    akvt = pt[T_AKV:T_AKV + A_KV_RANK]
    akvtn = akvt * lax.rsqrt(jnp.mean(akvt * akvt, axis=0, keepdims=True) + EPS) * gkvc_ref[...]
    vt = _dot(wv_ref[...], akvtn.astype(BF16))
    for hh in range(HA):
        va_ref[hh, 0:A_V, :] = vt[hh * A_V:(hh + 1) * A_V].astype(BF16)
        va_ref[hh, A_V:V_ROWS, :] = ones

    ctq, stq = ctq_ref[...], stq_ref[...]
    ck, sk = ck_ref[...], sk_ref[...]
    qsh = scale_h * LOG2E
    lane = lax.broadcasted_iota(jnp.int32, (t, LANES), 1)
    first = lane < B_HD

    for hh in range(HB):
        blk = pt[T_BQ + hh * B_HD:T_BQ + (hh + 1) * B_HD]
        y = blk * lax.rsqrt(jnp.mean(blk * blk, axis=0, keepdims=True) + EPS) * gqn_ref[...]
        qb_ref[hh] = ((y * ctq + _swap_halves(y, B_HD // 4) * stq) * qsh).astype(BF16)
        blk = pt[T_CQ + hh * C_HD:T_CQ + (hh + 1) * C_HD]
        qc_ref[hh] = ((blk * ctq + _swap_halves(blk, C_HD // 4) * stq) * qsh).astype(BF16)

    bk = ps[:, STD_BK:STD_BK + LANES]
    sq = bk * bk
    s0 = jnp.sum(jnp.where(first, sq, 0.0), axis=-1, keepdims=True)
    s1 = jnp.sum(jnp.where(first, 0.0, sq), axis=-1, keepdims=True)
    rk = lax.rsqrt(jnp.where(first, s0, s1) * (1.0 / B_HD) + EPS)
    kb = (bk * rk * gknr_ref[...]) * ck + (ps[:, STD_BKR:STD_BKR + LANES] * rk * gknrr_ref[...]) * sk
    kc = ps[:, STD_CK:STD_CK + LANES] * ck + ps[:, STD_CKR:STD_CKR + LANES] * sk
    for g in range(HB_KV):
        kb_ref[g] = kb[:, g * B_HD:(g + 1) * B_HD].astype(BF16)
        kc_ref[g] = kc[:, g * C_HD:(g + 1) * C_HD].astype(BF16)
        vb_ref[g, 0:B_HD, :] = pt[T_BV + g * B_HD:T_BV + (g + 1) * B_HD].astype(BF16)
        vb_ref[g, B_HD:V_ROWS, :] = ones
        vc_ref[g, 0:C_HD, :] = pt[T_CV + g * C_HD:T_CV + (g + 1) * C_HD].astype(BF16)
        vc_ref[g, C_HD:V_ROWS, :] = ones


def _project(xc, mod_l, lw, tabs, scale_a, scale_h):
    b, s, _ = xc.shape
    t = TOKEN_TILE
    n_t = s // t
    tok = lambda bi, ti: (bi, ti, 0)
    tok2 = lambda bi, ti: (ti, 0)
    feat2 = lambda bi, ti: (0, ti)
    in_specs = [
        pl.BlockSpec((None, t, D_MODEL), tok),
        pl.BlockSpec((None, N_MOD, D_MODEL), lambda bi, ti: (jnp.where(ti == 0, MOD_ROWS // 2, bi), 0, 0)),
        _full((1, D_MODEL)),
        _full(lw["w_std"].shape), _full(lw["w_t"].shape), _full(lw["w_qb"].shape), _full(lw["w_k"].shape),
        _full(lw["w_v"].shape),
        _full((A_Q_RANK, 1)), _full((1, LANES)), _full((A_KV_RANK, 1)), _full((B_HD, 1)), _full((1, LANES)),
        _full((1, LANES)),
        pl.BlockSpec((t, LANES), tok2), pl.BlockSpec((t, LANES), tok2),
        pl.BlockSpec((t, LANES), tok2), pl.BlockSpec((t, LANES), tok2),
        pl.BlockSpec((B_HD, t), feat2), pl.BlockSpec((B_HD, t), feat2),
        pl.BlockSpec((A_ROPE, t), feat2), pl.BlockSpec((A_ROPE, t), feat2),
    ]
    qspec = lambda heads, rows: pl.BlockSpec((None, heads, rows, t), lambda bi, ti: (bi, 0, 0, ti))
    kspec = lambda heads, cols: pl.BlockSpec((None, heads, t, cols), lambda bi, ti: (bi, 0, ti, 0))
    out_specs = [
        qspec(HA, HEAD_PAD), kspec(HA, HEAD_PAD), qspec(HA, V_ROWS),
        qspec(HB, B_HD), kspec(HB_KV, B_HD), qspec(HB_KV, V_ROWS),
        qspec(HC, C_HD), kspec(HC_KV, C_HD), qspec(HC_KV, V_ROWS),
    ]
    sd = jax.ShapeDtypeStruct
    out_shape = [
        sd((b, HA, HEAD_PAD, s), BF16), sd((b, HA, s, HEAD_PAD), BF16), sd((b, HA, V_ROWS, s), BF16),
        sd((b, HB, B_HD, s), BF16), sd((b, HB_KV, s, B_HD), BF16), sd((b, HB_KV, V_ROWS, s), BF16),
        sd((b, HC, C_HD, s), BF16), sd((b, HC_KV, s, C_HD), BF16), sd((b, HC_KV, V_ROWS, s), BF16),
    ]
    return pl.pallas_call(
        functools.partial(_proj_kernel, scale_a=scale_a, scale_h=scale_h),
        grid=(b, n_t),
        in_specs=in_specs,
        out_specs=out_specs,
        out_shape=out_shape,
        compiler_params=_params(2),
        name="projection",
    )(xc, mod_l, lw["g1"], lw["w_std"], lw["w_t"], lw["w_qb"], lw["w_k"], lw["w_v"],
      lw["g_qa_col"], lw["g_kv_row"], lw["g_kv_col"], lw["g_qn_col"], lw["g_kn_row"], lw["g_kn_rot_row"],
      tabs["ck"], tabs["sk"], tabs["ca"], tabs["sa"], tabs["ctq"], tabs["stq"], tabs["cta"], tabs["sta"])


def _attn_kernel(*refs, n_heads, group, has_sink):
    if has_sink:
        q_ref, k_ref, v_ref, sink_ref, o_ref = refs
    else:
        q_ref, k_ref, v_ref, o_ref = refs

    def one_head(hh, carry):
        g = hh // group
        s = _dot(k_ref[g], q_ref[hh])
        m = jnp.max(s, axis=0, keepdims=True)
        if has_sink:
            snk = sink_ref[hh]
            m = jnp.maximum(m, snk)
        p = jnp.exp2(s - m).astype(BF16)
        o = _dot(v_ref[g], p)
        denom = o[A_V:A_V + 1]
        if has_sink:
            denom = denom + jnp.exp2(snk - m)
        out = o[0:A_V] * (1.0 / denom)
        o_ref[pl.ds(pl.multiple_of(hh * A_V, A_V), A_V), :] = out.astype(BF16)
        return carry

    lax.fori_loop(0, n_heads, one_head, 0)


def _attention(qt, k, vt, sink2, *, n_keys, tile0, n_tiles, out=None):
    b, n_heads, dk, s = qt.shape
    hk = k.shape[1]
    t = TOKEN_TILE
    in_specs = [
        pl.BlockSpec((None, n_heads, dk, t), lambda bi, ti: (bi, 0, 0, ti + tile0)),
        pl.BlockSpec((None, hk, n_keys, k.shape[3]), lambda bi, ti: (bi, 0, 0, 0)),
        pl.BlockSpec((None, hk, V_ROWS, n_keys), lambda bi, ti: (bi, 0, 0, 0)),
    ]
    args = [qt, k, vt]
    if sink2 is not None:
        in_specs.append(_full(sink2.shape))
        args.append(sink2)
    aliases = {}
    if out is not None:
        in_specs.append(pl.BlockSpec(memory_space=pl.ANY))
        args.append(out)
        aliases = {len(args) - 1: 0}
    kern = functools.partial(_attn_kernel, n_heads=n_heads, group=n_heads // hk, has_sink=sink2 is not None)
    if out is not None:
        kern = _drop_last_input(kern, len(args))
    return pl.pallas_call(
        kern,
        grid=(b, n_tiles),
        in_specs=in_specs,
        out_specs=pl.BlockSpec((None, n_heads * A_V, t), lambda bi, ti: (bi, 0, ti + tile0)),
        out_shape=jax.ShapeDtypeStruct((b, n_heads * A_V, s), BF16),
        input_output_aliases=aliases,
        compiler_params=_params(2),
        name="attention",
    )(*args)


def _drop_last_input(kern, n_in):
    def wrapped(*refs):
        return kern(*refs[:n_in - 1], *refs[n_in:])
    return wrapped


def _window_kernel(q_ref, k_ref, v_ref, sink_ref, o_ref, *, n_ctx, n_tiles):
    tq = q_ref.shape[2]
    ti = pl.program_id(1)
    start = pl.multiple_of(n_ctx + ti * tq, LANES)
    lo = pl.multiple_of(start - WINDOW, LANES)
    nxt = pl.multiple_of(jnp.minimum(start + tq, n_ctx + (n_tiles - 1) * tq + tq - WINDOW), LANES)
    n_band = WINDOW + tq
    n_keys = n_ctx + n_band + WINDOW

    row = lax.broadcasted_iota(jnp.int32, (n_keys, tq), 0)
    col = lax.broadcasted_iota(jnp.int32, (n_keys, tq), 1)
    rel = row - n_ctx - WINDOW - col
    in_band = (rel >= -WINDOW) & (rel <= WINDOW)
    ok_lo = (ti > 0) | (row >= n_ctx + WINDOW)
    ok_hi = (ti < n_tiles - 1) | (row < n_ctx + n_band)
    valid = (row < n_ctx) | (in_band & ok_lo & ok_hi)

    group = HC // HC_KV
    for g in range(HC_KV):
        kcat = jnp.concatenate(
            [k_ref[g, 0:n_ctx, :], k_ref[g, pl.ds(lo, n_band), :], k_ref[g, pl.ds(nxt, WINDOW), :]], axis=0)
        vcat = jnp.concatenate(
            [v_ref[g, :, 0:n_ctx], v_ref[g, :, pl.ds(lo, n_band)], v_ref[g, :, pl.ds(nxt, WINDOW)]], axis=1)
        for j in range(group):
            hh = g * group + j
            s = jnp.where(valid, _dot(kcat, q_ref[hh]), NEG)
            snk = sink_ref[hh]
            m = jnp.maximum(jnp.max(s, axis=0, keepdims=True), snk)
            p = jnp.exp2(s - m).astype(BF16)
            o = _dot(vcat, p)
            denom = o[C_HD:C_HD + 1] + jnp.exp2(snk - m)
            o_ref[hh * C_HD:(hh + 1) * C_HD, :] = (o[0:C_HD] * (1.0 / denom)).astype(BF16)


def _window_attention(qt, k, vt, sink2, *, n_ctx):
    b, n_heads, dk, s = qt.shape
    hk = k.shape[1]
    t = TOKEN_TILE
    tile0 = n_ctx // t
    n_tiles = (s - n_ctx) // t
    return pl.pallas_call(
        functools.partial(_window_kernel, n_ctx=n_ctx, n_tiles=n_tiles),
        grid=(b, n_tiles),
        in_specs=[
            pl.BlockSpec((None, n_heads, dk, t), lambda bi, ti: (bi, 0, 0, ti + tile0)),
            pl.BlockSpec((None, hk, s, dk), lambda bi, ti: (bi, 0, 0, 0)),
            pl.BlockSpec((None, hk, V_ROWS, s), lambda bi, ti: (bi, 0, 0, 0)),
            _full(sink2.shape),
        ],
        out_specs=pl.BlockSpec((None, n_heads * C_HD, t), lambda bi, ti: (bi, 0, ti + tile0)),
        out_shape=jax.ShapeDtypeStruct((b, n_heads * C_HD, s), BF16),
        compiler_params=_params(2),
        name="window_attention",
    )(qt, k, vt, sink2)


def _merge_kernel(x_ref, mod_ref, g1_ref, oa_ref, ob_ref, oc_ref, wg_ref, bg_ref, wbr_ref, wout_ref, xo_ref):
    x = x_ref[...]
    h = _modulated_norm(x, g1_ref[...], mod_ref[0:1, :], mod_ref[1:2, :])
    hb = h.astype(BF16)
    y = None
    for i, o_ref in enumerate((oa_ref, ob_ref, oc_ref)):
        cols = slice(i * D_MODEL, (i + 1) * D_MODEL)
        gate = _sigmoid(_dot(hb, wg_ref[:, cols]) + bg_ref[:, cols])
        term = gate * _dot_tn(o_ref[...], wbr_ref[i])
        y = term if y is None else y + term
    z = _dot(y.astype(BF16), wout_ref[...])
    xo_ref[...] = x + mod_ref[2:3, :] * z


def _merge(xc, mod_l, lw, oa, ob, oc, *, tile0, n_tiles):
    b, s, _ = xc.shape
    t = TOKEN_TILE
    ctx_row = MOD_ROWS // 2
    tok = lambda bi, ti: (bi, ti + tile0, 0)
    feat = lambda bi, ti: (bi, 0, ti + tile0)
    return pl.pallas_call(
        _merge_kernel,
        grid=(b, n_tiles),
        in_specs=[
            pl.BlockSpec((None, t, D_MODEL), tok),
            pl.BlockSpec((None, N_MOD, D_MODEL), lambda bi, ti: (jnp.where(ti + tile0 == 0, ctx_row, bi), 0, 0)),
            _full((1, D_MODEL)),
            pl.BlockSpec((None, BRANCH_W, t), feat),
            pl.BlockSpec((None, BRANCH_W, t), feat),
            pl.BlockSpec((None, BRANCH_W, t), feat),
            _full(lw["w_gate"].shape), _full((1, N_BRANCH * D_MODEL)), _full(lw["w_branch"].shape),
            _full(lw["w_out"].shape),
        ],
        out_specs=pl.BlockSpec((None, t, D_MODEL), tok),
        out_shape=jax.ShapeDtypeStruct(xc.shape, F32),
        input_output_aliases={0: 0},
        compiler_params=_params(2),
        name="merge",
    )(xc, mod_l, lw["g1"], oa, ob, oc, lw["w_gate"], lw["b_gate"], lw["w_branch"], lw["w_out"])


def _ffn_kernel(xp_ref, x_ref, xn_ref, mod_ref, g2_ref, wup_ref, wconv_ref, bconv_ref, wdown_ref, gf_ref,
                o_ref, act_ref, *, tile0, n_seq_tiles, final_norm):
    t = x_ref.shape[0]
    halo = SUBLANES
    ta = pl.program_id(1) + tile0
    x = x_ref[...]
    xa = jnp.concatenate([xp_ref[...], x, xn_ref[...]], axis=0)
    h = _modulated_norm(xa, g2_ref[...], mod_ref[3:4, :], mod_ref[4:5, :])
    row = lax.broadcasted_iota(jnp.int32, (t + 2 * halo, 1), 0)
    keep_prev = (ta > 1).astype(F32)
    keep_next = ((ta > 0) & (ta < n_seq_tiles - 1)).astype(F32)
    keep = jnp.where(row < halo, keep_prev, jnp.where(row >= t + halo, keep_next, 1.0))
    hb = (h * keep).astype(BF16)

    def conv(u, cols):
        return (u[halo - 1:halo - 1 + t] * wconv_ref[0:1, cols] + u[halo:halo + t] * wconv_ref[1:2, cols]
                + u[halo + 1:halo + 1 + t] * wconv_ref[2:3, cols] + bconv_ref[:, cols])

    for c in range(D_FF // FF_CHUNK):
        ca = slice(c * FF_CHUNK, (c + 1) * FF_CHUNK)
        cg = slice(D_FF + c * FF_CHUNK, D_FF + (c + 1) * FF_CHUNK)
        a = conv(_dot(hb, wup_ref[:, ca]), ca)
        gv = conv(_dot(hb, wup_ref[:, cg]), cg)
        act_ref[:, ca] = (a * _sigmoid(a) * gv).astype(BF16)
    y = x + mod_ref[5:6, :] * _dot(act_ref[...], wdown_ref[...])
    if final_norm:
        y = y * lax.rsqrt(jnp.mean(y * y, axis=-1, keepdims=True) + EPS) * gf_ref[...]
    o_ref[...] = y


def _conv_ffn(xc, mod_l, lw, g_final, *, tile0, n_tiles, final_norm):
    b, s, _ = xc.shape
    t = TOKEN_TILE
    n_seq_tiles = s // t
    per = t // SUBLANES
    last_blk = s // SUBLANES - 1
    ctx_row = MOD_ROWS // 2
    out_rows = n_tiles * t if final_norm else s
    out_tile0 = 0 if final_norm else tile0
    return pl.pallas_call(
        functools.partial(_ffn_kernel, tile0=tile0, n_seq_tiles=n_seq_tiles, final_norm=final_norm),
        grid=(b, n_tiles),
        in_specs=[
            pl.BlockSpec((None, SUBLANES, D_MODEL), lambda bi, ti: (bi, jnp.maximum((ti + tile0) * per - 1, 0), 0)),
            pl.BlockSpec((None, t, D_MODEL), lambda bi, ti: (bi, ti + tile0, 0)),
            pl.BlockSpec((None, SUBLANES, D_MODEL),
                         lambda bi, ti: (bi, jnp.minimum((ti + tile0 + 1) * per, last_blk), 0)),
            pl.BlockSpec((None, N_MOD, D_MODEL), lambda bi, ti: (jnp.where(ti + tile0 == 0, ctx_row, bi), 0, 0)),
            _full((1, D_MODEL)),
            _full(lw["w_up"].shape), _full((CONV_W, 2 * D_FF)), _full((1, 2 * D_FF)), _full(lw["w_down"].shape),
            _full((1, D_MODEL)),
        ],
        out_specs=pl.BlockSpec((None, t, D_MODEL), lambda bi, ti: (bi, ti + out_tile0, 0)),
        out_shape=jax.ShapeDtypeStruct((b, out_rows, D_MODEL), F32),
        scratch_shapes=[pltpu.VMEM((t, D_FF), BF16)],
        compiler_params=_params(2),
        name="conv_ffn",
    )(xc, xc, xc, mod_l, lw["g2"], lw["w_up"], lw["w_conv"], lw["b_conv"], lw["w_down"], g_final)


def _partner(head_dim):
    nf = head_dim // 4
    d = np.arange(head_dim)
    a, half, f = d // (2 * nf), (d % (2 * nf)) // nf, d % nf
    return a * 2 * nf + (1 - half) * nf + f, np.where(half == 0, -1.0, 1.0).astype(np.float32)


def _rope_tables(n_ctx, n_lat):
    def full(head_dim):
        nf = head_dim // 4
        _, sign = _partner(head_dim)
        rows = n_lat // GRID_W
        row = jnp.repeat(jnp.arange(rows, dtype=F32), GRID_W)
        col = jnp.tile(jnp.arange(GRID_W, dtype=F32), rows)
        inv = ROPE_THETA ** (-jnp.arange(nf, dtype=F32) / nf)
        ang = jnp.stack([row[:, None] * inv, col[:, None] * inv], axis=1)
        c = jnp.broadcast_to(jnp.cos(ang)[:, :, None, :], (n_lat, 2, 2, nf)).reshape(n_lat, head_dim)
        s = jnp.broadcast_to(jnp.sin(ang)[:, :, None, :], (n_lat, 2, 2, nf)).reshape(n_lat, head_dim) * sign
        c = jnp.concatenate([jnp.ones((n_ctx, head_dim), F32), c], axis=0)
        s = jnp.concatenate([jnp.zeros((n_ctx, head_dim), F32), s], axis=0)
        return c, s

    c64, s64 = full(B_HD)
    c32, s32 = full(A_ROPE)
    pad_a = lambda v: jnp.pad(v, ((0, 0), (A_NOPE, LANES - A_NOPE - A_ROPE)))
    return {
        "ck": jnp.tile(c64, (1, LANES // B_HD)), "sk": jnp.tile(s64, (1, LANES // B_HD)),
        "ca": pad_a(c32), "sa": pad_a(s32),
        "ctq": c64.T, "stq": s64.T, "cta": c32.T, "sta": s32.T,
    }


def _layer_weights(l, w_in, b_gate, g_norm1, g_q_a, w_q_b, g_kv_a, w_kv_b, g_qn, g_kn, w_branch, w_out, g_norm2,
                   w_up, w_conv, b_conv, w_down):
    cuts = np.cumsum((0,) + IN_SIZES)
    w = w_in[l]
    aq, akv, akr, bq, bk, bv, cq, ck, cv, gl = (w[:, cuts[i]:cuts[i + 1]] for i in range(len(IN_SIZES)))
    p64, _ = _partner(B_HD)
    p128 = np.concatenate([p64 + i * B_HD for i in range(LANES // B_HD)])
    p32, _ = _partner(A_ROPE)
    pad_a = lambda v: jnp.pad(v, ((0, 0), (A_NOPE, LANES - A_NOPE - A_ROPE)))
    w_std = jnp.concatenate([akv, bk, bk[:, p128], ck, ck[:, p128], pad_a(akr), pad_a(akr[:, p32])], axis=1)
    w_t = jnp.concatenate([aq, akv, bq, cq, bv, cv], axis=1).T
    wqb = w_q_b[l].reshape(A_Q_RANK, HA, A_NOPE + A_ROPE)
    wqb = jnp.pad(wqb, ((0, 0), (0, 0), (0, HEAD_PAD - A_NOPE - A_ROPE))).reshape(A_Q_RANK, HA * HEAD_PAD).T
    wkv = w_kv_b[l].reshape(A_KV_RANK, HA, A_NOPE + A_V)
    w_k = jnp.pad(wkv[:, :, :A_NOPE], ((0, 0), (0, 0), (0, HEAD_PAD - A_NOPE))).reshape(A_KV_RANK, HA * HEAD_PAD)
    w_v = wkv[:, :, A_NOPE:].reshape(A_KV_RANK, HA * A_V).T
    g_kn2 = jnp.tile(g_kn[l], LANES // B_HD)
    return {
        "g1": g_norm1[l].reshape(1, D_MODEL), "g2": g_norm2[l].reshape(1, D_MODEL),
        "w_std": w_std.astype(BF16), "w_t": w_t.astype(BF16), "w_qb": wqb.astype(BF16),
        "w_k": w_k.astype(BF16), "w_v": w_v.astype(BF16),
        "g_qa_col": g_q_a[l].reshape(A_Q_RANK, 1), "g_kv_row": g_kv_a[l].reshape(1, A_KV_RANK),
        "g_kv_col": g_kv_a[l].reshape(A_KV_RANK, 1), "g_qn_col": g_qn[l].reshape(B_HD, 1),
        "g_kn_row": g_kn2.reshape(1, LANES), "g_kn_rot_row": g_kn2[p128].reshape(1, LANES),
        "w_gate": gl.astype(BF16), "b_gate": b_gate[l].reshape(1, N_BRANCH * D_MODEL),
        "w_branch": w_branch[l].astype(BF16), "w_out": w_out[l].astype(BF16),
        "w_up": w_up[l].astype(BF16), "w_conv": w_conv[l], "b_conv": b_conv[l].reshape(1, 2 * D_FF),
        "w_down": w_down[l].astype(BF16),
    }


def kernel(x, c, ctx, c_ctx, w_mod, b_mod, g_norm1, w_in, b_gate, g_q_a, w_q_b, g_kv_a, w_kv_b, g_qn, g_kn, sink,
           w_branch, w_out, g_norm2, w_up, w_conv, b_conv, w_down, g_final):
    b, n_lat, d = x.shape
    n_ctx = ctx.shape[1]
    depth = w_mod.shape[0]
    t = TOKEN_TILE
    assert d == D_MODEL and n_ctx == t and n_lat % t == 0 and n_lat % GRID_W == 0 and b <= MOD_ROWS // 2
    s = n_ctx + n_lat
    n_t = s // t
    scale_a = 1.0 / math.sqrt(A_NOPE + A_ROPE)
    scale_h = 1.0 / math.sqrt(B_HD)

    cvec = jnp.zeros((MOD_ROWS, D_MODEL), F32).at[:b].set(c).at[MOD_ROWS // 2].set(c_ctx)
    mods = _modulation(cvec, w_mod, b_mod)
    tabs = _rope_tables(n_ctx, n_lat)
    xc = jnp.concatenate([ctx, x], axis=1)
    gf = g_final.reshape(1, D_MODEL)

    for l in range(depth):
        last = l == depth - 1
        lw = _layer_weights(l, w_in, b_gate, g_norm1, g_q_a, w_q_b, g_kv_a, w_kv_b, g_qn, g_kn, w_branch, w_out,
                            g_norm2, w_up, w_conv, b_conv, w_down)
        qa, ka, va, qb, kb, vb, qc, kc, vc = _project(xc, mods[l], lw, tabs, scale_a, scale_h)
        sink2 = jnp.broadcast_to((sink[l] * LOG2E).reshape(HC, 1, 1), (HC, 1, t)).astype(F32)
        lat = dict(n_keys=s, tile0=1, n_tiles=n_t - 1)
        oa = _attention(qa, ka, va, None, **lat)
        ob = _attention(qb, kb, vb, None, **lat)
        oc = _window_attention(qc, kc, vc, sink2, n_ctx=n_ctx)
        if not last:
            cx = dict(n_keys=n_ctx, tile0=0, n_tiles=1)
            oa = _attention(qa, ka, va, None, out=oa, **cx)
            ob = _attention(qb, kb, vb, None, out=ob, **cx)
            oc = _attention(qc, kc, vc, sink2, out=oc, **cx)
        tiles = dict(tile0=1, n_tiles=n_t - 1) if last else dict(tile0=0, n_tiles=n_t)
        xc = _merge(xc, mods[l], lw, oa, ob, oc, **tiles)
        xc = _conv_ffn(xc, mods[l], lw, gf, final_norm=last, **tiles)
    return xc
```

```python
import functools
import math

import jax
import jax.numpy as jnp
import numpy as np
from jax import lax
from jax.experimental import pallas as pl
from jax.experimental.pallas import tpu as pltpu

F32 = jnp.float32
BF16 = jnp.bfloat16

D_MODEL = 1024
GRID_W = 64
EPS = 1e-6
ROPE_THETA = 10000.0
WINDOW = 128
NEG = -1e30
HA, A_NOPE, A_ROPE, A_V, A_Q_RANK, A_KV_RANK = 8, 64, 32, 64, 256, 128
HB, HB_KV, B_HD = 8, 2, 64
HC, HC_KV, C_HD = 8, 2, 64
BRANCH_W = 512
N_BRANCH = 3
D_FF = 2816
CONV_W = 3
N_MOD = 6
IN_SIZES = (A_Q_RANK, A_KV_RANK, A_ROPE, HB * B_HD, HB_KV * B_HD, HB_KV * B_HD, HC * C_HD, HC_KV * C_HD,
            HC_KV * C_HD, N_BRANCH * D_MODEL)

LANES = 128
SUBLANES = 8
BF16_ROWS = 16
TOKEN_TILE = 256
MOD_ROWS = 16
HEAD_PAD = 128
V_ROWS = A_V + BF16_ROWS
FF_CHUNK = 256
KEY_CHUNK = 256
VMEM_LIMIT = 56 * 1024 * 1024
LOG2E = 1.4426950408889634


def _dot(a, b):
    return jnp.dot(a, b, preferred_element_type=F32)


def _dot_nt(a, b):
    return lax.dot_general(a, b, (((1,), (1,)), ((), ())), preferred_element_type=F32)


def _dot_tn(a, b):
    return lax.dot_general(a, b, (((0,), (0,)), ((), ())), preferred_element_type=F32)


def _sigmoid(x):
    return 1.0 / (1.0 + jnp.exp(-x))


def _modulated_norm(x, g, shift, scale):
    ms = jnp.mean(x * x, axis=-1, keepdims=True)
    return (x * lax.rsqrt(ms + EPS) * g) * (1.0 + scale) + shift


def _params(n_grid):
    return pltpu.CompilerParams(dimension_semantics=("arbitrary",) * n_grid, vmem_limit_bytes=VMEM_LIMIT)


def _full(shape):
    nd = len(shape)
    return pl.BlockSpec(shape, lambda *_: (0,) * nd)


def _mod_kernel(c_ref, w_ref, b_ref, o_ref):
    c = c_ref[...]
    a = c * _sigmoid(c)
    o_ref[...] = jnp.dot(a, w_ref[...], precision=lax.Precision.HIGHEST, preferred_element_type=F32) + b_ref[...]


def _modulation(cvec, w_mod, b_mod):
    depth = w_mod.shape[0]
    n_col = N_MOD * D_MODEL
    out = pl.pallas_call(
        _mod_kernel,
        grid=(depth, N_MOD),
        in_specs=[
            pl.BlockSpec((MOD_ROWS, D_MODEL), lambda l, j: (0, 0)),
            pl.BlockSpec((None, D_MODEL, D_MODEL), lambda l, j: (l, 0, j)),
            pl.BlockSpec((None, 1, D_MODEL), lambda l, j: (l, 0, j)),
        ],
        out_specs=pl.BlockSpec((None, MOD_ROWS, D_MODEL), lambda l, j: (l, 0, j)),
        out_shape=jax.ShapeDtypeStruct((depth, MOD_ROWS, n_col), F32),
        compiler_params=_params(2),
        name="modulation",
    )(cvec, w_mod, b_mod.reshape(depth, 1, n_col))
    return out.reshape(depth, MOD_ROWS, N_MOD, D_MODEL)


STD_AKV, STD_BK, STD_BKR, STD_CK, STD_CKR, STD_KR, STD_KRR = (i * LANES for i in range(7))
STD_COLS = 7 * LANES
T_AQ = 0
T_AKV = T_AQ + A_Q_RANK
T_BQ = T_AKV + A_KV_RANK
T_CQ = T_BQ + HB * B_HD
T_BV = T_CQ + HC * C_HD
T_CV = T_BV + HB_KV * B_HD
T_ROWS = T_CV + HC_KV * C_HD


def _swap_halves(x, nf):
    parts = []
    for a in range(2):
        base = a * 2 * nf
        parts += [x[base + nf:base + 2 * nf], x[base:base + nf]]
    return jnp.concatenate(parts, axis=0)


def _ones_rows(t):
    row = lax.broadcasted_iota(jnp.int32, (BF16_ROWS, t), 0)
    return jnp.where(row == 0, 1.0, 0.0).astype(BF16)


def _proj_kernel(x_ref, mod_ref, g1_ref, wstd_ref, wt_ref, wqb_ref, wk_ref, wv_ref,
                 gqa_ref, gkvr_ref, gkvc_ref, gqn_ref, gknr_ref, gknrr_ref,
                 ck_ref, sk_ref, ca_ref, sa_ref, ctq_ref, stq_ref, cta_ref, sta_ref,
                 qa_ref, ka_ref, va_ref, qb_ref, kb_ref, vb_ref, qc_ref, kc_ref, vc_ref,
                 *, scale_a, scale_h):
    t = x_ref.shape[0]
    h = _modulated_norm(x_ref[...], g1_ref[...], mod_ref[0:1, :], mod_ref[1:2, :])
    hb = h.astype(BF16)
    ps = _dot(hb, wstd_ref[...])
    pt = _dot_nt(wt_ref[...], hb)
    ones = _ones_rows(t)

    aq = pt[T_AQ:T_AQ + A_Q_RANK]
    aqn = aq * lax.rsqrt(jnp.mean(aq * aq, axis=0, keepdims=True) + EPS) * gqa_ref[...]
    qt = _dot(wqb_ref[...], aqn.astype(BF16))
    cta, sta = cta_ref[...], sta_ref[...]
    qs = scale_a * LOG2E
    for hh in range(HA):
        base = hh * HEAD_PAD
        qa_ref[hh, 0:A_NOPE, :] = (qt[base:base + A_NOPE] * qs).astype(BF16)
        r = qt[base + A_NOPE:base + A_NOPE + A_ROPE]
        rr = r * cta + _swap_halves(r, A_ROPE // 4) * sta
        qa_ref[hh, A_NOPE:A_NOPE + A_ROPE, :] = (rr * qs).astype(BF16)
        qa_ref[hh, A_NOPE + A_ROPE:HEAD_PAD, :] = jnp.zeros((HEAD_PAD - A_NOPE - A_ROPE, t), BF16)

    akv = ps[:, STD_AKV:STD_AKV + LANES]
    akvn = akv * lax.rsqrt(jnp.mean(akv * akv, axis=-1, keepdims=True) + EPS) * gkvr_ref[...]
    kn = _dot(akvn.astype(BF16), wk_ref[...])
    kr = ps[:, STD_KR:STD_KR + LANES] * ca_ref[...] + ps[:, STD_KRR:STD_KRR + LANES] * sa_ref[...]
    for hh in range(HA):
        ka_ref[hh] = (kn[:, hh * HEAD_PAD:(hh + 1) * HEAD_PAD] + kr).astype(BF16)

    akvt = pt[T_AKV:T_AKV + A_KV_RANK]
    akvtn = akvt * lax.rsqrt(jnp.mean(akvt * akvt, axis=0, keepdims=True) + EPS) * gkvc_ref[...]
    vt = _dot(wv_ref[...], akvtn.astype(BF16))
    for hh in range(HA):
        va_ref[hh, 0:A_V, :] = vt[hh * A_V:(hh + 1) * A_V].astype(BF16)
        va_ref[hh, A_V:V_ROWS, :] = ones

    ctq, stq = ctq_ref[...], stq_ref[...]
    ck, sk = ck_ref[...], sk_ref[...]
    qsh = scale_h * LOG2E
    lane = lax.broadcasted_iota(jnp.int32, (t, LANES), 1)
    first = lane < B_HD

    for hh in range(HB):
        blk = pt[T_BQ + hh * B_HD:T_BQ + (hh + 1) * B_HD]
        y = blk * lax.rsqrt(jnp.mean(blk * blk, axis=0, keepdims=True) + EPS) * gqn_ref[...]
        qb_ref[hh] = ((y * ctq + _swap_halves(y, B_HD // 4) * stq) * qsh).astype(BF16)
        blk = pt[T_CQ + hh * C_HD:T_CQ + (hh + 1) * C_HD]
        qc_ref[hh] = ((blk * ctq + _swap_halves(blk, C_HD // 4) * stq) * qsh).astype(BF16)

    bk = ps[:, STD_BK:STD_BK + LANES]
    sq = bk * bk
    s0 = jnp.sum(jnp.where(first, sq, 0.0), axis=-1, keepdims=True)
    s1 = jnp.sum(jnp.where(first, 0.0, sq), axis=-1, keepdims=True)
    rk = lax.rsqrt(jnp.where(first, s0, s1) * (1.0 / B_HD) + EPS)
    kb = (bk * rk * gknr_ref[...]) * ck + (ps[:, STD_BKR:STD_BKR + LANES] * rk * gknrr_ref[...]) * sk
    kc = ps[:, STD_CK:STD_CK + LANES] * ck + ps[:, STD_CKR:STD_CKR + LANES] * sk
    for g in range(HB_KV):
        kb_ref[g] = kb[:, g * B_HD:(g + 1) * B_HD].astype(BF16)
        kc_ref[g] = kc[:, g * C_HD:(g + 1) * C_HD].astype(BF16)
        vb_ref[g, 0:B_HD, :] = pt[T_BV + g * B_HD:T_BV + (g + 1) * B_HD].astype(BF16)
        vb_ref[g, B_HD:V_ROWS, :] = ones
        vc_ref[g, 0:C_HD, :] = pt[T_CV + g * C_HD:T_CV + (g + 1) * C_HD].astype(BF16)
        vc_ref[g, C_HD:V_ROWS, :] = ones


def _project(xc, mod_l, lw, tabs, scale_a, scale_h):
    b, s, _ = xc.shape
    t = TOKEN_TILE
    n_t = s // t
    tok = lambda bi, ti: (bi, ti, 0)
    tok2 = lambda bi, ti: (ti, 0)
    feat2 = lambda bi, ti: (0, ti)
    in_specs = [
        pl.BlockSpec((None, t, D_MODEL), tok),
        pl.BlockSpec((None, N_MOD, D_MODEL), lambda bi, ti: (jnp.where(ti == 0, MOD_ROWS // 2, bi), 0, 0)),
        _full((1, D_MODEL)),
        _full(lw["w_std"].shape), _full(lw["w_t"].shape), _full(lw["w_qb"].shape), _full(lw["w_k"].shape),
        _full(lw["w_v"].shape),
        _full((A_Q_RANK, 1)), _full((1, LANES)), _full((A_KV_RANK, 1)), _full((B_HD, 1)), _full((1, LANES)),
        _full((1, LANES)),
        pl.BlockSpec((t, LANES), tok2), pl.BlockSpec((t, LANES), tok2),
        pl.BlockSpec((t, LANES), tok2), pl.BlockSpec((t, LANES), tok2),
        pl.BlockSpec((B_HD, t), feat2), pl.BlockSpec((B_HD, t), feat2),
        pl.BlockSpec((A_ROPE, t), feat2), pl.BlockSpec((A_ROPE, t), feat2),
    ]
    qspec = lambda heads, rows: pl.BlockSpec((None, heads, rows, t), lambda bi, ti: (bi, 0, 0, ti))
    kspec = lambda heads, cols: pl.BlockSpec((None, heads, t, cols), lambda bi, ti: (bi, 0, ti, 0))
    out_specs = [
        qspec(HA, HEAD_PAD), kspec(HA, HEAD_PAD), qspec(HA, V_ROWS),
        qspec(HB, B_HD), kspec(HB_KV, B_HD), qspec(HB_KV, V_ROWS),
        qspec(HC, C_HD), kspec(HC_KV, C_HD), qspec(HC_KV, V_ROWS),
    ]
    sd = jax.ShapeDtypeStruct
    out_shape = [
        sd((b, HA, HEAD_PAD, s), BF16), sd((b, HA, s, HEAD_PAD), BF16), sd((b, HA, V_ROWS, s), BF16),
        sd((b, HB, B_HD, s), BF16), sd((b, HB_KV, s, B_HD), BF16), sd((b, HB_KV, V_ROWS, s), BF16),
        sd((b, HC, C_HD, s), BF16), sd((b, HC_KV, s, C_HD), BF16), sd((b, HC_KV, V_ROWS, s), BF16),
    ]
    return pl.pallas_call(
        functools.partial(_proj_kernel, scale_a=scale_a, scale_h=scale_h),
        grid=(b, n_t),
        in_specs=in_specs,
        out_specs=out_specs,
        out_shape=out_shape,
        compiler_params=_params(2),
        name="projection",
    )(xc, mod_l, lw["g1"], lw["w_std"], lw["w_t"], lw["w_qb"], lw["w_k"], lw["w_v"],
      lw["g_qa_col"], lw["g_kv_row"], lw["g_kv_col"], lw["g_qn_col"], lw["g_kn_row"], lw["g_kn_rot_row"],
      tabs["ck"], tabs["sk"], tabs["ca"], tabs["sa"], tabs["ctq"], tabs["stq"], tabs["cta"], tabs["sta"])


def _attn_kernel(*refs, n_heads, group, has_sink):
    if has_sink:
        q_ref, k_ref, v_ref, sink_ref, o_ref, s_scr = refs
    else:
        q_ref, k_ref, v_ref, o_ref, s_scr = refs
    n_keys, tq = s_scr.shape[1], s_scr.shape[2]
    chunk = min(KEY_CHUNK, n_keys)
    n_chunks = n_keys // chunk

    def scores(hh, slot):
        g = hh // group
        q = q_ref[hh]
        mrun = None
        for c in range(n_chunks):
            rows = slice(c * chunk, (c + 1) * chunk)
            s = _dot(k_ref[g, rows, :], q)
            s_scr[slot, rows, :] = s
            part = jnp.max(s.reshape(chunk // SUBLANES, SUBLANES, tq), axis=0)
            mrun = part if mrun is None else jnp.maximum(mrun, part)
        return jnp.max(mrun, axis=0, keepdims=True)

    def output(hh, slot, m):
        g = hh // group
        if has_sink:
            snk = sink_ref[hh]
            m = jnp.maximum(m, snk)
        acc = None
        for c in range(n_chunks):
            rows = slice(c * chunk, (c + 1) * chunk)
            p = jnp.exp2(s_scr[slot, rows, :] - m).astype(BF16)
            part = _dot(v_ref[g, :, rows], p)
            acc = part if acc is None else acc + part
        denom = acc[A_V:A_V + 1]
        if has_sink:
            denom = denom + jnp.exp2(snk - m)
        out = acc[0:A_V] * (1.0 / denom)
        o_ref[pl.ds(pl.multiple_of(hh * A_V, A_V), A_V), :] = out.astype(BF16)

    def head_pair(j, m_even):
        h0 = 2 * j
        m_odd = scores(h0 + 1, 1)
        output(h0, 0, m_even)
        m_next = scores(h0 + 2, 0)
        output(h0 + 1, 1, m_odd)
        return m_next

    m_even = lax.fori_loop(0, n_heads // 2 - 1, head_pair, scores(0, 0))
    m_odd = scores(n_heads - 1, 1)
    output(n_heads - 2, 0, m_even)
    output(n_heads - 1, 1, m_odd)


def _attention(qt, k, vt, sink2, *, n_keys, tile0, n_tiles, out=None):
    b, n_heads, dk, s = qt.shape
    hk = k.shape[1]
    t = TOKEN_TILE
    in_specs = [
        pl.BlockSpec((None, n_heads, dk, t), lambda bi, ti: (bi, 0, 0, ti + tile0)),
        pl.BlockSpec((None, hk, n_keys, k.shape[3]), lambda bi, ti: (bi, 0, 0, 0)),
        pl.BlockSpec((None, hk, V_ROWS, n_keys), lambda bi, ti: (bi, 0, 0, 0)),
    ]
    args = [qt, k, vt]
    if sink2 is not None:
        in_specs.append(_full(sink2.shape))
        args.append(sink2)
    aliases = {}
    if out is not None:
        in_specs.append(pl.BlockSpec(memory_space=pl.ANY))
        args.append(out)
        aliases = {len(args) - 1: 0}
    kern = functools.partial(_attn_kernel, n_heads=n_heads, group=n_heads // hk, has_sink=sink2 is not None)
    if out is not None:
        kern = _drop_last_input(kern, len(args))
    return pl.pallas_call(
        kern,
        grid=(b, n_tiles),
        in_specs=in_specs,
        out_specs=pl.BlockSpec((None, n_heads * A_V, t), lambda bi, ti: (bi, 0, ti + tile0)),
        out_shape=jax.ShapeDtypeStruct((b, n_heads * A_V, s), BF16),
        input_output_aliases=aliases,
        scratch_shapes=[pltpu.VMEM((2, n_keys, t), F32)],
        compiler_params=_params(2),
        name="attention",
    )(*args)


def _drop_last_input(kern, n_in):
    def wrapped(*refs):
        return kern(*refs[:n_in - 1], *refs[n_in:])
    return wrapped


def _window_kernel(q_ref, k_ref, v_ref, sink_ref, o_ref, k_scr, v_scr, cap_scr, s_scr, *, n_ctx, n_tiles):
    tq = q_ref.shape[2]
    ti = pl.program_id(1)
    start = pl.multiple_of(n_ctx + ti * tq, LANES)
    lo = pl.multiple_of(start - WINDOW, LANES)
    nxt = pl.multiple_of(jnp.minimum(start + tq, n_ctx + (n_tiles - 1) * tq + tq - WINDOW), LANES)
    n_band = WINDOW + tq
    n_keys = n_ctx + n_band + WINDOW

    row = lax.broadcasted_iota(jnp.int32, (n_keys, tq), 0)
    col = lax.broadcasted_iota(jnp.int32, (n_keys, tq), 1)
    rel = row - n_ctx - WINDOW - col
    in_band = (rel >= -WINDOW) & (rel <= WINDOW)
    ok_lo = (ti > 0) | (row >= n_ctx + WINDOW)
    ok_hi = (ti < n_tiles - 1) | (row < n_ctx + n_band)
    valid = (row < n_ctx) | (in_band & ok_lo & ok_hi)
    cap_scr[...] = jnp.where(valid, jnp.inf, NEG)

    for g in range(HC_KV):
        k_scr[g, 0:n_ctx, :] = k_ref[g, 0:n_ctx, :]
        k_scr[g, n_ctx:n_ctx + n_band, :] = k_ref[g, pl.ds(lo, n_band), :]
        k_scr[g, n_ctx + n_band:n_keys, :] = k_ref[g, pl.ds(nxt, WINDOW), :]
        v_scr[g, :, 0:n_ctx] = v_ref[g, :, 0:n_ctx]
        v_scr[g, :, n_ctx:n_ctx + n_band] = v_ref[g, :, pl.ds(lo, n_band)]
        v_scr[g, :, n_ctx + n_band:n_keys] = v_ref[g, :, pl.ds(nxt, WINDOW)]

    group = HC // HC_KV
    chunk = KEY_CHUNK
    n_chunks = n_keys // chunk

    def scores(hh, slot):
        q = q_ref[hh]
        mrun = None
        for c in range(n_chunks):
            rows = slice(c * chunk, (c + 1) * chunk)
            s = jnp.minimum(_dot(k_scr[hh // group, rows, :], q), cap_scr[rows, :])
            s_scr[slot, rows, :] = s
            part = jnp.max(s.reshape(chunk // SUBLANES, SUBLANES, tq), axis=0)
            mrun = part if mrun is None else jnp.maximum(mrun, part)
        return jnp.max(mrun, axis=0, keepdims=True)

    def output(hh, slot, m):
        snk = sink_ref[hh]
        m = jnp.maximum(m, snk)
        acc = None
        for c in range(n_chunks):
            rows = slice(c * chunk, (c + 1) * chunk)
            p = jnp.exp2(s_scr[slot, rows, :] - m).astype(BF16)
            part = _dot(v_scr[hh // group, :, rows], p)
            acc = part if acc is None else acc + part
        denom = acc[C_HD:C_HD + 1] + jnp.exp2(snk - m)
        o_ref[hh * C_HD:(hh + 1) * C_HD, :] = (acc[0:C_HD] * (1.0 / denom)).astype(BF16)

    m = scores(0, 0)
    for hh in range(HC):
        m_next = scores(hh + 1, (hh + 1) % 2) if hh + 1 < HC else None
        output(hh, hh % 2, m)
        m = m_next


def _window_attention(qt, k, vt, sink2, *, n_ctx):
    b, n_heads, dk, s = qt.shape
    hk = k.shape[1]
    t = TOKEN_TILE
    tile0 = n_ctx // t
    n_tiles = (s - n_ctx) // t
    n_win_keys = n_ctx + WINDOW + t + WINDOW
    return pl.pallas_call(
        functools.partial(_window_kernel, n_ctx=n_ctx, n_tiles=n_tiles),
        grid=(b, n_tiles),
        in_specs=[
            pl.BlockSpec((None, n_heads, dk, t), lambda bi, ti: (bi, 0, 0, ti + tile0)),
            pl.BlockSpec((None, hk, s, dk), lambda bi, ti: (bi, 0, 0, 0)),
            pl.BlockSpec((None, hk, V_ROWS, s), lambda bi, ti: (bi, 0, 0, 0)),
            _full(sink2.shape),
        ],
        out_specs=pl.BlockSpec((None, n_heads * C_HD, t), lambda bi, ti: (bi, 0, ti + tile0)),
        out_shape=jax.ShapeDtypeStruct((b, n_heads * C_HD, s), BF16),
        scratch_shapes=[
            pltpu.VMEM((hk, n_win_keys, dk), BF16), pltpu.VMEM((hk, V_ROWS, n_win_keys), BF16),
            pltpu.VMEM((n_win_keys, t), F32), pltpu.VMEM((2, n_win_keys, t), F32),
        ],
        compiler_params=_params(2),
        name="window_attention",
    )(qt, k, vt, sink2)


def _merge_kernel(x_ref, mod_ref, g1_ref, oa_ref, ob_ref, oc_ref, wg_ref, bg_ref, wbr_ref, wout_ref, xo_ref):
    x = x_ref[...]
    h = _modulated_norm(x, g1_ref[...], mod_ref[0:1, :], mod_ref[1:2, :])
    hb = h.astype(BF16)
    y = None
    for i, o_ref in enumerate((oa_ref, ob_ref, oc_ref)):
        cols = slice(i * D_MODEL, (i + 1) * D_MODEL)
        gate = _sigmoid(_dot(hb, wg_ref[:, cols]) + bg_ref[:, cols])
        term = gate * _dot_tn(o_ref[...], wbr_ref[i])
        y = term if y is None else y + term
    z = _dot(y.astype(BF16), wout_ref[...])
    xo_ref[...] = x + mod_ref[2:3, :] * z


def _merge(xc, mod_l, lw, oa, ob, oc, *, tile0, n_tiles):
    b, s, _ = xc.shape
    t = TOKEN_TILE
    ctx_row = MOD_ROWS // 2
    tok = lambda bi, ti: (bi, ti + tile0, 0)
    feat = lambda bi, ti: (bi, 0, ti + tile0)
    return pl.pallas_call(
        _merge_kernel,
        grid=(b, n_tiles),
        in_specs=[
            pl.BlockSpec((None, t, D_MODEL), tok),
            pl.BlockSpec((None, N_MOD, D_MODEL), lambda bi, ti: (jnp.where(ti + tile0 == 0, ctx_row, bi), 0, 0)),
            _full((1, D_MODEL)),
            pl.BlockSpec((None, BRANCH_W, t), feat),
            pl.BlockSpec((None, BRANCH_W, t), feat),
            pl.BlockSpec((None, BRANCH_W, t), feat),
            _full(lw["w_gate"].shape), _full((1, N_BRANCH * D_MODEL)), _full(lw["w_branch"].shape),
            _full(lw["w_out"].shape),
        ],
        out_specs=pl.BlockSpec((None, t, D_MODEL), tok),
        out_shape=jax.ShapeDtypeStruct(xc.shape, F32),
        input_output_aliases={0: 0},
        compiler_params=_params(2),
        name="merge",
    )(xc, mod_l, lw["g1"], oa, ob, oc, lw["w_gate"], lw["b_gate"], lw["w_branch"], lw["w_out"])


def _ffn_kernel(xp_ref, x_ref, xn_ref, mod_ref, g2_ref, wup_ref, wconv_ref, bconv_ref, wdown_ref, gf_ref,
                o_ref, act_ref, *, tile0, n_seq_tiles, final_norm):
    t = x_ref.shape[0]
    halo = SUBLANES
    ta = pl.program_id(1) + tile0
    x = x_ref[...]
    xa = jnp.concatenate([xp_ref[...], x, xn_ref[...]], axis=0)
    h = _modulated_norm(xa, g2_ref[...], mod_ref[3:4, :], mod_ref[4:5, :])
    row = lax.broadcasted_iota(jnp.int32, (t + 2 * halo, 1), 0)
    keep_prev = (ta > 1).astype(F32)
    keep_next = ((ta > 0) & (ta < n_seq_tiles - 1)).astype(F32)
    keep = jnp.where(row < halo, keep_prev, jnp.where(row >= t + halo, keep_next, 1.0))
    hb = (h * keep).astype(BF16)

    def conv(u, cols):
        return (u[halo - 1:halo - 1 + t] * wconv_ref[0:1, cols] + u[halo:halo + t] * wconv_ref[1:2, cols]
                + u[halo + 1:halo + 1 + t] * wconv_ref[2:3, cols] + bconv_ref[:, cols])

    for c in range(D_FF // FF_CHUNK):
        ca = slice(c * FF_CHUNK, (c + 1) * FF_CHUNK)
        cg = slice(D_FF + c * FF_CHUNK, D_FF + (c + 1) * FF_CHUNK)
        a = conv(_dot(hb, wup_ref[:, ca]), ca)
        gv = conv(_dot(hb, wup_ref[:, cg]), cg)
        act_ref[:, ca] = (a * _sigmoid(a) * gv).astype(BF16)
    y = x + mod_ref[5:6, :] * _dot(act_ref[...], wdown_ref[...])
    if final_norm:
        y = y * lax.rsqrt(jnp.mean(y * y, axis=-1, keepdims=True) + EPS) * gf_ref[...]
    o_ref[...] = y


def _conv_ffn(xc, mod_l, lw, g_final, *, tile0, n_tiles, final_norm):
    b, s, _ = xc.shape
    t = TOKEN_TILE
    n_seq_tiles = s // t
    per = t // SUBLANES
    last_blk = s // SUBLANES - 1
    ctx_row = MOD_ROWS // 2
    out_rows = n_tiles * t if final_norm else s
    out_tile0 = 0 if final_norm else tile0
    return pl.pallas_call(
        functools.partial(_ffn_kernel, tile0=tile0, n_seq_tiles=n_seq_tiles, final_norm=final_norm),
        grid=(b, n_tiles),
        in_specs=[
            pl.BlockSpec((None, SUBLANES, D_MODEL), lambda bi, ti: (bi, jnp.maximum((ti + tile0) * per - 1, 0), 0)),
            pl.BlockSpec((None, t, D_MODEL), lambda bi, ti: (bi, ti + tile0, 0)),
            pl.BlockSpec((None, SUBLANES, D_MODEL),
                         lambda bi, ti: (bi, jnp.minimum((ti + tile0 + 1) * per, last_blk), 0)),
            pl.BlockSpec((None, N_MOD, D_MODEL), lambda bi, ti: (jnp.where(ti + tile0 == 0, ctx_row, bi), 0, 0)),
            _full((1, D_MODEL)),
            _full(lw["w_up"].shape), _full((CONV_W, 2 * D_FF)), _full((1, 2 * D_FF)), _full(lw["w_down"].shape),
            _full((1, D_MODEL)),
        ],
        out_specs=pl.BlockSpec((None, t, D_MODEL), lambda bi, ti: (bi, ti + out_tile0, 0)),
        out_shape=jax.ShapeDtypeStruct((b, out_rows, D_MODEL), F32),
        scratch_shapes=[pltpu.VMEM((t, D_FF), BF16)],
        compiler_params=_params(2),
        name="conv_ffn",
    )(xc, xc, xc, mod_l, lw["g2"], lw["w_up"], lw["w_conv"], lw["b_conv"], lw["w_down"], g_final)


def _partner(head_dim):
    nf = head_dim // 4
    d = np.arange(head_dim)
    a, half, f = d // (2 * nf), (d % (2 * nf)) // nf, d % nf
    return a * 2 * nf + (1 - half) * nf + f, np.where(half == 0, -1.0, 1.0).astype(np.float32)


def _rope_tables(n_ctx, n_lat):
    def full(head_dim):
        nf = head_dim // 4
        _, sign = _partner(head_dim)
        rows = n_lat // GRID_W
        row = jnp.repeat(jnp.arange(rows, dtype=F32), GRID_W)
        col = jnp.tile(jnp.arange(GRID_W, dtype=F32), rows)
        inv = ROPE_THETA ** (-jnp.arange(nf, dtype=F32) / nf)
        ang = jnp.stack([row[:, None] * inv, col[:, None] * inv], axis=1)
        c = jnp.broadcast_to(jnp.cos(ang)[:, :, None, :], (n_lat, 2, 2, nf)).reshape(n_lat, head_dim)
        s = jnp.broadcast_to(jnp.sin(ang)[:, :, None, :], (n_lat, 2, 2, nf)).reshape(n_lat, head_dim) * sign
        c = jnp.concatenate([jnp.ones((n_ctx, head_dim), F32), c], axis=0)
        s = jnp.concatenate([jnp.zeros((n_ctx, head_dim), F32), s], axis=0)
        return c, s

    c64, s64 = full(B_HD)
    c32, s32 = full(A_ROPE)
    pad_a = lambda v: jnp.pad(v, ((0, 0), (A_NOPE, LANES - A_NOPE - A_ROPE)))
    return {
        "ck": jnp.tile(c64, (1, LANES // B_HD)), "sk": jnp.tile(s64, (1, LANES // B_HD)),
        "ca": pad_a(c32), "sa": pad_a(s32),
        "ctq": c64.T, "stq": s64.T, "cta": c32.T, "sta": s32.T,
    }


def _layer_weights(l, w_in, b_gate, g_norm1, g_q_a, w_q_b, g_kv_a, w_kv_b, g_qn, g_kn, w_branch, w_out, g_norm2,
                   w_up, w_conv, b_conv, w_down):
    cuts = np.cumsum((0,) + IN_SIZES)
    w = w_in[l]
    aq, akv, akr, bq, bk, bv, cq, ck, cv, gl = (w[:, cuts[i]:cuts[i + 1]] for i in range(len(IN_SIZES)))
    p64, _ = _partner(B_HD)
    p128 = np.concatenate([p64 + i * B_HD for i in range(LANES // B_HD)])
    p32, _ = _partner(A_ROPE)
    pad_a = lambda v: jnp.pad(v, ((0, 0), (A_NOPE, LANES - A_NOPE - A_ROPE)))
    w_std = jnp.concatenate([akv, bk, bk[:, p128], ck, ck[:, p128], pad_a(akr), pad_a(akr[:, p32])], axis=1)
    w_t = jnp.concatenate([aq, akv, bq, cq, bv, cv], axis=1).T
    wqb = w_q_b[l].reshape(A_Q_RANK, HA, A_NOPE + A_ROPE)
    wqb = jnp.pad(wqb, ((0, 0), (0, 0), (0, HEAD_PAD - A_NOPE - A_ROPE))).reshape(A_Q_RANK, HA * HEAD_PAD).T
    wkv = w_kv_b[l].reshape(A_KV_RANK, HA, A_NOPE + A_V)
    w_k = jnp.pad(wkv[:, :, :A_NOPE], ((0, 0), (0, 0), (0, HEAD_PAD - A_NOPE))).reshape(A_KV_RANK, HA * HEAD_PAD)
    w_v = wkv[:, :, A_NOPE:].reshape(A_KV_RANK, HA * A_V).T
    g_kn2 = jnp.tile(g_kn[l], LANES // B_HD)
    return {
        "g1": g_norm1[l].reshape(1, D_MODEL), "g2": g_norm2[l].reshape(1, D_MODEL),
        "w_std": w_std.astype(BF16), "w_t": w_t.astype(BF16), "w_qb": wqb.astype(BF16),
        "w_k": w_k.astype(BF16), "w_v": w_v.astype(BF16),
        "g_qa_col": g_q_a[l].reshape(A_Q_RANK, 1), "g_kv_row": g_kv_a[l].reshape(1, A_KV_RANK),
        "g_kv_col": g_kv_a[l].reshape(A_KV_RANK, 1), "g_qn_col": g_qn[l].reshape(B_HD, 1),
        "g_kn_row": g_kn2.reshape(1, LANES), "g_kn_rot_row": g_kn2[p128].reshape(1, LANES),
        "w_gate": gl.astype(BF16), "b_gate": b_gate[l].reshape(1, N_BRANCH * D_MODEL),
        "w_branch": w_branch[l].astype(BF16), "w_out": w_out[l].astype(BF16),
        "w_up": w_up[l].astype(BF16), "w_conv": w_conv[l], "b_conv": b_conv[l].reshape(1, 2 * D_FF),
        "w_down": w_down[l].astype(BF16),
    }


def kernel(x, c, ctx, c_ctx, w_mod, b_mod, g_norm1, w_in, b_gate, g_q_a, w_q_b, g_kv_a, w_kv_b, g_qn, g_kn, sink,
           w_branch, w_out, g_norm2, w_up, w_conv, b_conv, w_down, g_final):
    b, n_lat, d = x.shape
    n_ctx = ctx.shape[1]
    depth = w_mod.shape[0]
    t = TOKEN_TILE
    assert d == D_MODEL and n_ctx == t and n_lat % t == 0 and n_lat % GRID_W == 0 and b <= MOD_ROWS // 2
    s = n_ctx + n_lat
    n_t = s // t
    scale_a = 1.0 / math.sqrt(A_NOPE + A_ROPE)
    scale_h = 1.0 / math.sqrt(B_HD)

    cvec = jnp.zeros((MOD_ROWS, D_MODEL), F32).at[:b].set(c).at[MOD_ROWS // 2].set(c_ctx)
    mods = _modulation(cvec, w_mod, b_mod)
    tabs = _rope_tables(n_ctx, n_lat)
    xc = jnp.concatenate([ctx, x], axis=1)
    gf = g_final.reshape(1, D_MODEL)

    for l in range(depth):
        last = l == depth - 1
        lw = _layer_weights(l, w_in, b_gate, g_norm1, g_q_a, w_q_b, g_kv_a, w_kv_b, g_qn, g_kn, w_branch, w_out,
                            g_norm2, w_up, w_conv, b_conv, w_down)
        qa, ka, va, qb, kb, vb, qc, kc, vc = _project(xc, mods[l], lw, tabs, scale_a, scale_h)
        sink2 = jnp.broadcast_to((sink[l] * LOG2E).reshape(HC, 1, 1), (HC, 1, t)).astype(F32)
        lat = dict(n_keys=s, tile0=1, n_tiles=n_t - 1)
        oa = _attention(qa, ka, va, None, **lat)
        ob = _attention(qb, kb, vb, None, **lat)
        oc = _window_attention(qc, kc, vc, sink2, n_ctx=n_ctx)
        if not last:
            cx = dict(n_keys=n_ctx, tile0=0, n_tiles=1)
            oa = _attention(qa, ka, va, None, out=oa, **cx)
            ob = _attention(qb, kb, vb, None, out=ob, **cx)
            oc = _attention(qc, kc, vc, sink2, out=oc, **cx)
        tiles = dict(tile0=1, n_tiles=n_t - 1) if last else dict(tile0=0, n_tiles=n_t)
        xc = _merge(xc, mods[l], lw, oa, ob, oc, **tiles)
        xc = _conv_ffn(xc, mods[l], lw, gf, final_norm=last, **tiles)
    return xc
```

```python
import functools
import math

import jax
import jax.numpy as jnp
import numpy as np
from jax import lax
from jax.experimental import pallas as pl
from jax.experimental.pallas import tpu as pltpu

F32 = jnp.float32
BF16 = jnp.bfloat16

D_MODEL = 1024
GRID_W = 64
EPS = 1e-6
ROPE_THETA = 10000.0
WINDOW = 128
NEG = -1e30
HA, A_NOPE, A_ROPE, A_V, A_Q_RANK, A_KV_RANK = 8, 64, 32, 64, 256, 128
HB, HB_KV, B_HD = 8, 2, 64
HC, HC_KV, C_HD = 8, 2, 64
BRANCH_W = 512
N_BRANCH = 3
D_FF = 2816
CONV_W = 3
N_MOD = 6
IN_SIZES = (A_Q_RANK, A_KV_RANK, A_ROPE, HB * B_HD, HB_KV * B_HD, HB_KV * B_HD, HC * C_HD, HC_KV * C_HD,
            HC_KV * C_HD, N_BRANCH * D_MODEL)

LANES = 128
SUBLANES = 8
BF16_ROWS = 16
TOKEN_TILE = 256
MOD_ROWS = 16
HEAD_PAD = 128
V_ROWS = A_V + BF16_ROWS
FF_CHUNK = 256
KEY_CHUNK = 256
VMEM_LIMIT = 56 * 1024 * 1024
LOG2E = 1.4426950408889634


def _dot(a, b):
    return jnp.dot(a, b, preferred_element_type=F32)


def _dot_nt(a, b):
    return lax.dot_general(a, b, (((1,), (1,)), ((), ())), preferred_element_type=F32)


def _dot_tn(a, b):
    return lax.dot_general(a, b, (((0,), (0,)), ((), ())), preferred_element_type=F32)


def _sigmoid(x):
    return 1.0 / (1.0 + jnp.exp(-x))


def _modulated_norm(x, g, shift, scale):
    ms = jnp.mean(x * x, axis=-1, keepdims=True)
    return (x * lax.rsqrt(ms + EPS) * g) * (1.0 + scale) + shift


def _params(n_grid):
    return pltpu.CompilerParams(dimension_semantics=("arbitrary",) * n_grid, vmem_limit_bytes=VMEM_LIMIT)


def _full(shape):
    nd = len(shape)
    return pl.BlockSpec(shape, lambda *_: (0,) * nd)


def _mod_kernel(c_ref, w_ref, b_ref, o_ref):
    c = c_ref[...]
    a = c * _sigmoid(c)
    o_ref[...] = jnp.dot(a, w_ref[...], precision=lax.Precision.HIGHEST, preferred_element_type=F32) + b_ref[...]


def _modulation(cvec, w_mod, b_mod):
    depth = w_mod.shape[0]
    n_col = N_MOD * D_MODEL
    out = pl.pallas_call(
        _mod_kernel,
        grid=(depth, N_MOD),
        in_specs=[
            pl.BlockSpec((MOD_ROWS, D_MODEL), lambda l, j: (0, 0)),
            pl.BlockSpec((None, D_MODEL, D_MODEL), lambda l, j: (l, 0, j)),
            pl.BlockSpec((None, 1, D_MODEL), lambda l, j: (l, 0, j)),
        ],
        out_specs=pl.BlockSpec((None, MOD_ROWS, D_MODEL), lambda l, j: (l, 0, j)),
        out_shape=jax.ShapeDtypeStruct((depth, MOD_ROWS, n_col), F32),
        compiler_params=_params(2),
        name="modulation",
    )(cvec, w_mod, b_mod.reshape(depth, 1, n_col))
    return out.reshape(depth, MOD_ROWS, N_MOD, D_MODEL)


STD_AKV, STD_BK, STD_BKR, STD_CK, STD_CKR, STD_KR, STD_KRR = (i * LANES for i in range(7))
STD_COLS = 7 * LANES
T_AQ = 0
T_AKV = T_AQ + A_Q_RANK
T_BQ = T_AKV + A_KV_RANK
T_CQ = T_BQ + HB * B_HD
T_BV = T_CQ + HC * C_HD
T_CV = T_BV + HB_KV * B_HD
T_ROWS = T_CV + HC_KV * C_HD


def _swap_halves(x, nf):
    parts = []
    for a in range(2):
        base = a * 2 * nf
        parts += [x[base + nf:base + 2 * nf], x[base:base + nf]]
    return jnp.concatenate(parts, axis=0)


def _ones_rows(t):
    row = lax.broadcasted_iota(jnp.int32, (BF16_ROWS, t), 0)
    return jnp.where(row == 0, 1.0, 0.0).astype(BF16)


def _proj_kernel(x_ref, mod_ref, g1_ref, wstd_ref, wt_ref, wqb_ref, wk_ref, wv_ref,
                 gqa_ref, gkvr_ref, gkvc_ref, gqn_ref, gknr_ref, gknrr_ref,
                 ck_ref, sk_ref, ca_ref, sa_ref, ctq_ref, stq_ref, cta_ref, sta_ref,
                 qa_ref, ka_ref, va_ref, qb_ref, kb_ref, vb_ref, qc_ref, kc_ref, vc_ref,
                 *, scale_a, scale_h):
    t = x_ref.shape[0]
    h = _modulated_norm(x_ref[...], g1_ref[...], mod_ref[0:1, :], mod_ref[1:2, :])
    hb = h.astype(BF16)
    ps = _dot(hb, wstd_ref[...])
    pt = _dot_nt(wt_ref[...], hb)
    ones = _ones_rows(t)

    aq = pt[T_AQ:T_AQ + A_Q_RANK]
    aqn = aq * lax.rsqrt(jnp.mean(aq * aq, axis=0, keepdims=True) + EPS) * gqa_ref[...]
    qt = _dot(wqb_ref[...], aqn.astype(BF16))
    cta, sta = cta_ref[...], sta_ref[...]
    qs = scale_a * LOG2E
    for hh in range(HA):
        base = hh * HEAD_PAD
        qa_ref[hh, 0:A_NOPE, :] = (qt[base:base + A_NOPE] * qs).astype(BF16)
        r = qt[base + A_NOPE:base + A_NOPE + A_ROPE]
        rr = r * cta + _swap_halves(r, A_ROPE // 4) * sta
        qa_ref[hh, A_NOPE:A_NOPE + A_ROPE, :] = (rr * qs).astype(BF16)
        qa_ref[hh, A_NOPE + A_ROPE:HEAD_PAD, :] = jnp.zeros((HEAD_PAD - A_NOPE - A_ROPE, t), BF16)

    akv = ps[:, STD_AKV:STD_AKV + LANES]
    akvn = akv * lax.rsqrt(jnp.mean(akv * akv, axis=-1, keepdims=True) + EPS) * gkvr_ref[...]
    kn = _dot(akvn.astype(BF16), wk_ref[...])
    kr = ps[:, STD_KR:STD_KR + LANES] * ca_ref[...] + ps[:, STD_KRR:STD_KRR + LANES] * sa_ref[...]
    for hh in range(HA):
        ka_ref[hh] = (kn[:, hh * HEAD_PAD:(hh + 1) * HEAD_PAD] + kr).astype(BF16)

    akvt = pt[T_AKV:T_AKV + A_KV_RANK]
    akvtn = akvt * lax.rsqrt(jnp.mean(akvt * akvt, axis=0, keepdims=True) + EPS) * gkvc_ref[...]
    vt = _dot(wv_ref[...], akvtn.astype(BF16))
    for hh in range(HA):
        va_ref[hh, 0:A_V, :] = vt[hh * A_V:(hh + 1) * A_V].astype(BF16)
        va_ref[hh, A_V:V_ROWS, :] = ones

    ctq, stq = ctq_ref[...], stq_ref[...]
    ck, sk = ck_ref[...], sk_ref[...]
    qsh = scale_h * LOG2E
    lane = lax.broadcasted_iota(jnp.int32, (t, LANES), 1)
    first = lane < B_HD

    for hh in range(HB):
        blk = pt[T_BQ + hh * B_HD:T_BQ + (hh + 1) * B_HD]
        y = blk * lax.rsqrt(jnp.mean(blk * blk, axis=0, keepdims=True) + EPS) * gqn_ref[...]
        qb_ref[hh] = ((y * ctq + _swap_halves(y, B_HD // 4) * stq) * qsh).astype(BF16)
        blk = pt[T_CQ + hh * C_HD:T_CQ + (hh + 1) * C_HD]
        qc_ref[hh] = ((blk * ctq + _swap_halves(blk, C_HD // 4) * stq) * qsh).astype(BF16)

    bk = ps[:, STD_BK:STD_BK + LANES]
    sq = bk * bk
    s0 = jnp.sum(jnp.where(first, sq, 0.0), axis=-1, keepdims=True)
    s1 = jnp.sum(jnp.where(first, 0.0, sq), axis=-1, keepdims=True)
    rk = lax.rsqrt(jnp.where(first, s0, s1) * (1.0 / B_HD) + EPS)
    kb = (bk * rk * gknr_ref[...]) * ck + (ps[:, STD_BKR:STD_BKR + LANES] * rk * gknrr_ref[...]) * sk
    kc = ps[:, STD_CK:STD_CK + LANES] * ck + ps[:, STD_CKR:STD_CKR + LANES] * sk
    for g in range(HB_KV):
        kb_ref[g] = kb[:, g * B_HD:(g + 1) * B_HD].astype(BF16)
        kc_ref[g] = kc[:, g * C_HD:(g + 1) * C_HD].astype(BF16)
        vb_ref[g, 0:B_HD, :] = pt[T_BV + g * B_HD:T_BV + (g + 1) * B_HD].astype(BF16)
        vb_ref[g, B_HD:V_ROWS, :] = ones
        vc_ref[g, 0:C_HD, :] = pt[T_CV + g * C_HD:T_CV + (g + 1) * C_HD].astype(BF16)
        vc_ref[g, C_HD:V_ROWS, :] = ones


def _project(xc, mod_l, lw, tabs, scale_a, scale_h):
    b, s, _ = xc.shape
    t = TOKEN_TILE
    n_t = s // t
    tok = lambda bi, ti: (bi, ti, 0)
    tok2 = lambda bi, ti: (ti, 0)
    feat2 = lambda bi, ti: (0, ti)
    in_specs = [
        pl.BlockSpec((None, t, D_MODEL), tok),
        pl.BlockSpec((None, N_MOD, D_MODEL), lambda bi, ti: (jnp.where(ti == 0, MOD_ROWS // 2, bi), 0, 0)),
        _full((1, D_MODEL)),
        _full(lw["w_std"].shape), _full(lw["w_t"].shape), _full(lw["w_qb"].shape), _full(lw["w_k"].shape),
        _full(lw["w_v"].shape),
        _full((A_Q_RANK, 1)), _full((1, LANES)), _full((A_KV_RANK, 1)), _full((B_HD, 1)), _full((1, LANES)),
        _full((1, LANES)),
        pl.BlockSpec((t, LANES), tok2), pl.BlockSpec((t, LANES), tok2),
        pl.BlockSpec((t, LANES), tok2), pl.BlockSpec((t, LANES), tok2),
        pl.BlockSpec((B_HD, t), feat2), pl.BlockSpec((B_HD, t), feat2),
        pl.BlockSpec((A_ROPE, t), feat2), pl.BlockSpec((A_ROPE, t), feat2),
    ]
    qspec = lambda heads, rows: pl.BlockSpec((None, heads, rows, t), lambda bi, ti: (bi, 0, 0, ti))
    kspec = lambda heads, cols: pl.BlockSpec((None, heads, t, cols), lambda bi, ti: (bi, 0, ti, 0))
    out_specs = [
        qspec(HA, HEAD_PAD), kspec(HA, HEAD_PAD), qspec(HA, V_ROWS),
        qspec(HB, B_HD), kspec(HB_KV, B_HD), qspec(HB_KV, V_ROWS),
        qspec(HC, C_HD), kspec(HC_KV, C_HD), qspec(HC_KV, V_ROWS),
    ]
    sd = jax.ShapeDtypeStruct
    out_shape = [
        sd((b, HA, HEAD_PAD, s), BF16), sd((b, HA, s, HEAD_PAD), BF16), sd((b, HA, V_ROWS, s), BF16),
        sd((b, HB, B_HD, s), BF16), sd((b, HB_KV, s, B_HD), BF16), sd((b, HB_KV, V_ROWS, s), BF16),
        sd((b, HC, C_HD, s), BF16), sd((b, HC_KV, s, C_HD), BF16), sd((b, HC_KV, V_ROWS, s), BF16),
    ]
    return pl.pallas_call(
        functools.partial(_proj_kernel, scale_a=scale_a, scale_h=scale_h),
        grid=(b, n_t),
        in_specs=in_specs,
        out_specs=out_specs,
        out_shape=out_shape,
        compiler_params=_params(2),
        name="projection",
    )(xc, mod_l, lw["g1"], lw["w_std"], lw["w_t"], lw["w_qb"], lw["w_k"], lw["w_v"],
      lw["g_qa_col"], lw["g_kv_row"], lw["g_kv_col"], lw["g_qn_col"], lw["g_kn_row"], lw["g_kn_rot_row"],
      tabs["ck"], tabs["sk"], tabs["ca"], tabs["sa"], tabs["ctq"], tabs["stq"], tabs["cta"], tabs["sta"])


def _attn_kernel(*refs, n_heads, group, has_sink):
    if has_sink:
        q_ref, qn_ref, k_ref, v_ref, sink_ref, o_ref, s_scr, m_scr = refs
    else:
        q_ref, qn_ref, k_ref, v_ref, o_ref, s_scr, m_scr = refs
    n_keys, tq = s_scr.shape[1], s_scr.shape[2]
    chunk = min(KEY_CHUNK, n_keys)
    n_chunks = n_keys // chunk

    def step(h_s, h_o, m_o):
        if h_s is not None:
            head_s, slot_s = h_s
            q = jnp.where(head_s == n_heads, qn_ref[0], q_ref[jnp.minimum(head_s, n_heads - 1)])
            head_s = head_s % n_heads
            mrun = None
        if h_o is not None:
            head_o, slot_o = h_o
            if has_sink:
                snk = sink_ref[head_o]
                m_o = jnp.maximum(m_o, snk)
            acc = None
        for c in range(n_chunks):
            rows = slice(c * chunk, (c + 1) * chunk)
            if h_s is not None:
                s = _dot(k_ref[head_s // group, rows, :], q)
                s_scr[slot_s, rows, :] = s
                part = jnp.max(s.reshape(chunk // SUBLANES, SUBLANES, tq), axis=0)
                mrun = part if mrun is None else jnp.maximum(mrun, part)
            if h_o is not None:
                p = jnp.exp2(s_scr[slot_o, rows, :] - m_o).astype(BF16)
                part = _dot(v_ref[head_o // group, :, rows], p)
                acc = part if acc is None else acc + part
        if h_o is not None:
            denom = acc[A_V:A_V + 1]
            if has_sink:
                denom = denom + jnp.exp2(snk - m_o)
            out = acc[0:A_V] * (1.0 / denom)
            o_ref[pl.ds(pl.multiple_of(head_o * A_V, A_V), A_V), :] = out.astype(BF16)
        if h_s is not None:
            return jnp.max(mrun, axis=0, keepdims=True)

    def head_pair(j, m_even):
        h0 = 2 * j
        m_odd = step((h0 + 1, 1), (h0, 0), m_even)
        return step((h0 + 2, 0), (h0 + 1, 1), m_odd)

    @pl.when(pl.program_id(1) == 0)
    def _():
        m_scr[...] = step((0, 0), None, None)

    m_scr[...] = lax.fori_loop(0, n_heads // 2, head_pair, m_scr[...])


def _attention(qt, k, vt, sink2, *, n_keys, tile0, n_tiles, out=None):
    b, n_heads, dk, s = qt.shape
    hk = k.shape[1]
    t = TOKEN_TILE
    in_specs = [
        pl.BlockSpec((None, n_heads, dk, t), lambda bi, ti: (bi, 0, 0, ti + tile0)),
        pl.BlockSpec((None, 1, dk, t), lambda bi, ti: (bi, 0, 0, jnp.minimum(ti + 1, n_tiles - 1) + tile0)),
        pl.BlockSpec((None, hk, n_keys, k.shape[3]), lambda bi, ti: (bi, 0, 0, 0)),
        pl.BlockSpec((None, hk, V_ROWS, n_keys), lambda bi, ti: (bi, 0, 0, 0)),
    ]
    args = [qt, qt, k, vt]
    if sink2 is not None:
        in_specs.append(_full(sink2.shape))
        args.append(sink2)
    aliases = {}
    if out is not None:
        in_specs.append(pl.BlockSpec(memory_space=pl.ANY))
        args.append(out)
        aliases = {len(args) - 1: 0}
    kern = functools.partial(_attn_kernel, n_heads=n_heads, group=n_heads // hk, has_sink=sink2 is not None)
    if out is not None:
        kern = _drop_last_input(kern, len(args))
    return pl.pallas_call(
        kern,
        grid=(b, n_tiles),
        in_specs=in_specs,
        out_specs=pl.BlockSpec((None, n_heads * A_V, t), lambda bi, ti: (bi, 0, ti + tile0)),
        out_shape=jax.ShapeDtypeStruct((b, n_heads * A_V, s), BF16),
        input_output_aliases=aliases,
        scratch_shapes=[pltpu.VMEM((2, n_keys, t), F32), pltpu.VMEM((1, t), F32)],
        compiler_params=_params(2),
        name="attention",
    )(*args)


def _drop_last_input(kern, n_in):
    def wrapped(*refs):
        return kern(*refs[:n_in - 1], *refs[n_in:])
    return wrapped


def _window_kernel(q_ref, k_ref, v_ref, sink_ref, o_ref, k_scr, v_scr, cap_scr, s_scr, *, n_ctx, n_tiles):
    tq = q_ref.shape[2]
    ti = pl.program_id(1)
    start = pl.multiple_of(n_ctx + ti * tq, LANES)
    lo = pl.multiple_of(start - WINDOW, LANES)
    nxt = pl.multiple_of(jnp.minimum(start + tq, n_ctx + (n_tiles - 1) * tq + tq - WINDOW), LANES)
    n_band = WINDOW + tq
    n_keys = n_ctx + n_band + WINDOW

    row = lax.broadcasted_iota(jnp.int32, (n_keys, tq), 0)
    col = lax.broadcasted_iota(jnp.int32, (n_keys, tq), 1)
    rel = row - n_ctx - WINDOW - col
    in_band = (rel >= -WINDOW) & (rel <= WINDOW)
    ok_lo = (ti > 0) | (row >= n_ctx + WINDOW)
    ok_hi = (ti < n_tiles - 1) | (row < n_ctx + n_band)
    valid = (row < n_ctx) | (in_band & ok_lo & ok_hi)
    cap_scr[...] = jnp.where(valid, jnp.inf, NEG)

    for g in range(HC_KV):
        k_scr[g, 0:n_ctx, :] = k_ref[g, 0:n_ctx, :]
        k_scr[g, n_ctx:n_ctx + n_band, :] = k_ref[g, pl.ds(lo, n_band), :]
        k_scr[g, n_ctx + n_band:n_keys, :] = k_ref[g, pl.ds(nxt, WINDOW), :]
        v_scr[g, :, 0:n_ctx] = v_ref[g, :, 0:n_ctx]
        v_scr[g, :, n_ctx:n_ctx + n_band] = v_ref[g, :, pl.ds(lo, n_band)]
        v_scr[g, :, n_ctx + n_band:n_keys] = v_ref[g, :, pl.ds(nxt, WINDOW)]

    group = HC // HC_KV
    chunk = KEY_CHUNK
    n_chunks = n_keys // chunk

    def scores(hh):
        q = q_ref[hh]
        mrun = None
        for c in range(n_chunks):
            rows = slice(c * chunk, (c + 1) * chunk)
            s = jnp.minimum(_dot(k_scr[hh // group, rows, :], q), cap_scr[rows, :])
            s_scr[hh % 2, rows, :] = s
            part = jnp.max(s.reshape(chunk // SUBLANES, SUBLANES, tq), axis=0)
            mrun = part if mrun is None else jnp.maximum(mrun, part)
        return jnp.max(mrun, axis=0, keepdims=True)

    def output(hh, m):
        snk = sink_ref[hh]
        m = jnp.maximum(m, snk)
        acc = None
        for c in range(n_chunks):
            rows = slice(c * chunk, (c + 1) * chunk)
            p = jnp.exp2(s_scr[hh % 2, rows, :] - m).astype(BF16)
            part = _dot(v_scr[hh // group, :, rows], p)
            acc = part if acc is None else acc + part
        denom = acc[C_HD:C_HD + 1] + jnp.exp2(snk - m)
        o_ref[hh * C_HD:(hh + 1) * C_HD, :] = (acc[0:C_HD] * (1.0 / denom)).astype(BF16)

    m = scores(0)
    for hh in range(HC):
        m_next = scores(hh + 1) if hh + 1 < HC else None
        output(hh, m)
        m = m_next


def _window_attention(qt, k, vt, sink2, *, n_ctx):
    b, n_heads, dk, s = qt.shape
    hk = k.shape[1]
    t = TOKEN_TILE
    tile0 = n_ctx // t
    n_tiles = (s - n_ctx) // t
    n_win_keys = n_ctx + WINDOW + t + WINDOW
    return pl.pallas_call(
        functools.partial(_window_kernel, n_ctx=n_ctx, n_tiles=n_tiles),
        grid=(b, n_tiles),
        in_specs=[
            pl.BlockSpec((None, n_heads, dk, t), lambda bi, ti: (bi, 0, 0, ti + tile0)),
            pl.BlockSpec((None, hk, s, dk), lambda bi, ti: (bi, 0, 0, 0)),
            pl.BlockSpec((None, hk, V_ROWS, s), lambda bi, ti: (bi, 0, 0, 0)),
            _full(sink2.shape),
        ],
        out_specs=pl.BlockSpec((None, n_heads * C_HD, t), lambda bi, ti: (bi, 0, ti + tile0)),
        out_shape=jax.ShapeDtypeStruct((b, n_heads * C_HD, s), BF16),
        scratch_shapes=[
            pltpu.VMEM((hk, n_win_keys, dk), BF16), pltpu.VMEM((hk, V_ROWS, n_win_keys), BF16),
            pltpu.VMEM((n_win_keys, t), F32), pltpu.VMEM((2, n_win_keys, t), F32),
        ],
        compiler_params=_params(2),
        name="window_attention",
    )(qt, k, vt, sink2)


def _merge_kernel(x_ref, mod_ref, g1_ref, oa_ref, ob_ref, oc_ref, wg_ref, bg_ref, wbr_ref, wout_ref, xo_ref):
    x = x_ref[...]
    h = _modulated_norm(x, g1_ref[...], mod_ref[0:1, :], mod_ref[1:2, :])
    hb = h.astype(BF16)
    y = None
    for i, o_ref in enumerate((oa_ref, ob_ref, oc_ref)):
        cols = slice(i * D_MODEL, (i + 1) * D_MODEL)
        gate = _sigmoid(_dot(hb, wg_ref[:, cols]) + bg_ref[:, cols])
        term = gate * _dot_tn(o_ref[...], wbr_ref[i])
        y = term if y is None else y + term
    z = _dot(y.astype(BF16), wout_ref[...])
    xo_ref[...] = x + mod_ref[2:3, :] * z


def _merge(xc, mod_l, lw, oa, ob, oc, *, tile0, n_tiles):
    b, s, _ = xc.shape
    t = TOKEN_TILE
    ctx_row = MOD_ROWS // 2
    tok = lambda bi, ti: (bi, ti + tile0, 0)
    feat = lambda bi, ti: (bi, 0, ti + tile0)
    return pl.pallas_call(
        _merge_kernel,
        grid=(b, n_tiles),
        in_specs=[
            pl.BlockSpec((None, t, D_MODEL), tok),
            pl.BlockSpec((None, N_MOD, D_MODEL), lambda bi, ti: (jnp.where(ti + tile0 == 0, ctx_row, bi), 0, 0)),
            _full((1, D_MODEL)),
            pl.BlockSpec((None, BRANCH_W, t), feat),
            pl.BlockSpec((None, BRANCH_W, t), feat),
            pl.BlockSpec((None, BRANCH_W, t), feat),
            _full(lw["w_gate"].shape), _full((1, N_BRANCH * D_MODEL)), _full(lw["w_branch"].shape),
            _full(lw["w_out"].shape),
        ],
        out_specs=pl.BlockSpec((None, t, D_MODEL), tok),
        out_shape=jax.ShapeDtypeStruct(xc.shape, F32),
        input_output_aliases={0: 0},
        compiler_params=_params(2),
        name="merge",
    )(xc, mod_l, lw["g1"], oa, ob, oc, lw["w_gate"], lw["b_gate"], lw["w_branch"], lw["w_out"])


def _ffn_kernel(xp_ref, x_ref, xn_ref, mod_ref, g2_ref, wup_ref, wconv_ref, bconv_ref, wdown_ref, gf_ref,
                o_ref, act_ref, *, tile0, n_seq_tiles, final_norm):
    t = x_ref.shape[0]
    halo = SUBLANES
    ta = pl.program_id(1) + tile0
    x = x_ref[...]
    xa = jnp.concatenate([xp_ref[...], x, xn_ref[...]], axis=0)
    h = _modulated_norm(xa, g2_ref[...], mod_ref[3:4, :], mod_ref[4:5, :])
    row = lax.broadcasted_iota(jnp.int32, (t + 2 * halo, 1), 0)
    keep_prev = (ta > 1).astype(F32)
    keep_next = ((ta > 0) & (ta < n_seq_tiles - 1)).astype(F32)
    keep = jnp.where(row < halo, keep_prev, jnp.where(row >= t + halo, keep_next, 1.0))
    hb = (h * keep).astype(BF16)

    def conv(u, cols):
        return (u[halo - 1:halo - 1 + t] * wconv_ref[0:1, cols] + u[halo:halo + t] * wconv_ref[1:2, cols]
                + u[halo + 1:halo + 1 + t] * wconv_ref[2:3, cols] + bconv_ref[:, cols])

    for c in range(D_FF // FF_CHUNK):
        ca = slice(c * FF_CHUNK, (c + 1) * FF_CHUNK)
        cg = slice(D_FF + c * FF_CHUNK, D_FF + (c + 1) * FF_CHUNK)
        a = conv(_dot(hb, wup_ref[:, ca]), ca)
        gv = conv(_dot(hb, wup_ref[:, cg]), cg)
        act_ref[:, ca] = (a * _sigmoid(a) * gv).astype(BF16)
    y = x + mod_ref[5:6, :] * _dot(act_ref[...], wdown_ref[...])
    if final_norm:
        y = y * lax.rsqrt(jnp.mean(y * y, axis=-1, keepdims=True) + EPS) * gf_ref[...]
    o_ref[...] = y


def _conv_ffn(xc, mod_l, lw, g_final, *, tile0, n_tiles, final_norm):
    b, s, _ = xc.shape
    t = TOKEN_TILE
    n_seq_tiles = s // t
    per = t // SUBLANES
    last_blk = s // SUBLANES - 1
    ctx_row = MOD_ROWS // 2
    out_rows = n_tiles * t if final_norm else s
    out_tile0 = 0 if final_norm else tile0
    return pl.pallas_call(
        functools.partial(_ffn_kernel, tile0=tile0, n_seq_tiles=n_seq_tiles, final_norm=final_norm),
        grid=(b, n_tiles),
        in_specs=[
            pl.BlockSpec((None, SUBLANES, D_MODEL), lambda bi, ti: (bi, jnp.maximum((ti + tile0) * per - 1, 0), 0)),
            pl.BlockSpec((None, t, D_MODEL), lambda bi, ti: (bi, ti + tile0, 0)),
            pl.BlockSpec((None, SUBLANES, D_MODEL),
                         lambda bi, ti: (bi, jnp.minimum((ti + tile0 + 1) * per, last_blk), 0)),
            pl.BlockSpec((None, N_MOD, D_MODEL), lambda bi, ti: (jnp.where(ti + tile0 == 0, ctx_row, bi), 0, 0)),
            _full((1, D_MODEL)),
            _full(lw["w_up"].shape), _full((CONV_W, 2 * D_FF)), _full((1, 2 * D_FF)), _full(lw["w_down"].shape),
            _full((1, D_MODEL)),
        ],
        out_specs=pl.BlockSpec((None, t, D_MODEL), lambda bi, ti: (bi, ti + out_tile0, 0)),
        out_shape=jax.ShapeDtypeStruct((b, out_rows, D_MODEL), F32),
        scratch_shapes=[pltpu.VMEM((t, D_FF), BF16)],
        compiler_params=_params(2),
        name="conv_ffn",
    )(xc, xc, xc, mod_l, lw["g2"], lw["w_up"], lw["w_conv"], lw["b_conv"], lw["w_down"], g_final)


def _partner(head_dim):
    nf = head_dim // 4
    d = np.arange(head_dim)
    a, half, f = d // (2 * nf), (d % (2 * nf)) // nf, d % nf
    return a * 2 * nf + (1 - half) * nf + f, np.where(half == 0, -1.0, 1.0).astype(np.float32)


def _rope_tables(n_ctx, n_lat):
    def full(head_dim):
        nf = head_dim // 4
        _, sign = _partner(head_dim)
        rows = n_lat // GRID_W
        row = jnp.repeat(jnp.arange(rows, dtype=F32), GRID_W)
        col = jnp.tile(jnp.arange(GRID_W, dtype=F32), rows)
        inv = ROPE_THETA ** (-jnp.arange(nf, dtype=F32) / nf)
        ang = jnp.stack([row[:, None] * inv, col[:, None] * inv], axis=1)
        c = jnp.broadcast_to(jnp.cos(ang)[:, :, None, :], (n_lat, 2, 2, nf)).reshape(n_lat, head_dim)
        s = jnp.broadcast_to(jnp.sin(ang)[:, :, None, :], (n_lat, 2, 2, nf)).reshape(n_lat, head_dim) * sign
        c = jnp.concatenate([jnp.ones((n_ctx, head_dim), F32), c], axis=0)
        s = jnp.concatenate([jnp.zeros((n_ctx, head_dim), F32), s], axis=0)
        return c, s

    c64, s64 = full(B_HD)
    c32, s32 = full(A_ROPE)
    pad_a = lambda v: jnp.pad(v, ((0, 0), (A_NOPE, LANES - A_NOPE - A_ROPE)))
    return {
        "ck": jnp.tile(c64, (1, LANES // B_HD)), "sk": jnp.tile(s64, (1, LANES // B_HD)),
        "ca": pad_a(c32), "sa": pad_a(s32),
        "ctq": c64.T, "stq": s64.T, "cta": c32.T, "sta": s32.T,
    }


def _layer_weights(l, w_in, b_gate, g_norm1, g_q_a, w_q_b, g_kv_a, w_kv_b, g_qn, g_kn, w_branch, w_out, g_norm2,
                   w_up, w_conv, b_conv, w_down):
    cuts = np.cumsum((0,) + IN_SIZES)
    w = w_in[l]
    aq, akv, akr, bq, bk, bv, cq, ck, cv, gl = (w[:, cuts[i]:cuts[i + 1]] for i in range(len(IN_SIZES)))
    p64, _ = _partner(B_HD)
    p128 = np.concatenate([p64 + i * B_HD for i in range(LANES // B_HD)])
    p32, _ = _partner(A_ROPE)
    pad_a = lambda v: jnp.pad(v, ((0, 0), (A_NOPE, LANES - A_NOPE - A_ROPE)))
    w_std = jnp.concatenate([akv, bk, bk[:, p128], ck, ck[:, p128], pad_a(akr), pad_a(akr[:, p32])], axis=1)
    w_t = jnp.concatenate([aq, akv, bq, cq, bv, cv], axis=1).T
    wqb = w_q_b[l].reshape(A_Q_RANK, HA, A_NOPE + A_ROPE)
    wqb = jnp.pad(wqb, ((0, 0), (0, 0), (0, HEAD_PAD - A_NOPE - A_ROPE))).reshape(A_Q_RANK, HA * HEAD_PAD).T
    wkv = w_kv_b[l].reshape(A_KV_RANK, HA, A_NOPE + A_V)
    w_k = jnp.pad(wkv[:, :, :A_NOPE], ((0, 0), (0, 0), (0, HEAD_PAD - A_NOPE))).reshape(A_KV_RANK, HA * HEAD_PAD)
    w_v = wkv[:, :, A_NOPE:].reshape(A_KV_RANK, HA * A_V).T
    g_kn2 = jnp.tile(g_kn[l], LANES // B_HD)
    return {
        "g1": g_norm1[l].reshape(1, D_MODEL), "g2": g_norm2[l].reshape(1, D_MODEL),
        "w_std": w_std.astype(BF16), "w_t": w_t.astype(BF16), "w_qb": wqb.astype(BF16),
        "w_k": w_k.astype(BF16), "w_v": w_v.astype(BF16),
        "g_qa_col": g_q_a[l].reshape(A_Q_RANK, 1), "g_kv_row": g_kv_a[l].reshape(1, A_KV_RANK),
        "g_kv_col": g_kv_a[l].reshape(A_KV_RANK, 1), "g_qn_col": g_qn[l].reshape(B_HD, 1),
        "g_kn_row": g_kn2.reshape(1, LANES), "g_kn_rot_row": g_kn2[p128].reshape(1, LANES),
        "w_gate": gl.astype(BF16), "b_gate": b_gate[l].reshape(1, N_BRANCH * D_MODEL),
        "w_branch": w_branch[l].astype(BF16), "w_out": w_out[l].astype(BF16),
        "w_up": w_up[l].astype(BF16), "w_conv": w_conv[l], "b_conv": b_conv[l].reshape(1, 2 * D_FF),
        "w_down": w_down[l].astype(BF16),
    }


def kernel(x, c, ctx, c_ctx, w_mod, b_mod, g_norm1, w_in, b_gate, g_q_a, w_q_b, g_kv_a, w_kv_b, g_qn, g_kn, sink,
           w_branch, w_out, g_norm2, w_up, w_conv, b_conv, w_down, g_final):
    b, n_lat, d = x.shape
    n_ctx = ctx.shape[1]
    depth = w_mod.shape[0]
    t = TOKEN_TILE
    assert d == D_MODEL and n_ctx == t and n_lat % t == 0 and n_lat % GRID_W == 0 and b <= MOD_ROWS // 2
    s = n_ctx + n_lat
    n_t = s // t
    scale_a = 1.0 / math.sqrt(A_NOPE + A_ROPE)
    scale_h = 1.0 / math.sqrt(B_HD)

    cvec = jnp.zeros((MOD_ROWS, D_MODEL), F32).at[:b].set(c).at[MOD_ROWS // 2].set(c_ctx)
    mods = _modulation(cvec, w_mod, b_mod)
    tabs = _rope_tables(n_ctx, n_lat)
    xc = jnp.concatenate([ctx, x], axis=1)
    gf = g_final.reshape(1, D_MODEL)

    for l in range(depth):
        last = l == depth - 1
        lw = _layer_weights(l, w_in, b_gate, g_norm1, g_q_a, w_q_b, g_kv_a, w_kv_b, g_qn, g_kn, w_branch, w_out,
                            g_norm2, w_up, w_conv, b_conv, w_down)
        qa, ka, va, qb, kb, vb, qc, kc, vc = _project(xc, mods[l], lw, tabs, scale_a, scale_h)
        sink2 = jnp.broadcast_to((sink[l] * LOG2E).reshape(HC, 1, 1), (HC, 1, t)).astype(F32)
        lat = dict(n_keys=s, tile0=1, n_tiles=n_t - 1)
        oa = _attention(qa, ka, va, None, **lat)
        ob = _attention(qb, kb, vb, None, **lat)
        oc = _window_attention(qc, kc, vc, sink2, n_ctx=n_ctx)
        if not last:
            cx = dict(n_keys=n_ctx, tile0=0, n_tiles=1)
            oa = _attention(qa, ka, va, None, out=oa, **cx)
            ob = _attention(qb, kb, vb, None, out=ob, **cx)
            oc = _attention(qc, kc, vc, sink2, out=oc, **cx)
        tiles = dict(tile0=1, n_tiles=n_t - 1) if last else dict(tile0=0, n_tiles=n_t)
        xc = _merge(xc, mods[l], lw, oa, ob, oc, **tiles)
        xc = _conv_ffn(xc, mods[l], lw, gf, final_norm=last, **tiles)
    return xc
```

```python
import functools
import math

import jax
import jax.numpy as jnp
import numpy as np
from jax import lax
from jax.experimental import pallas as pl
from jax.experimental.pallas import tpu as pltpu

F32 = jnp.float32
BF16 = jnp.bfloat16

D_MODEL = 1024
GRID_W = 64
EPS = 1e-6
ROPE_THETA = 10000.0
WINDOW = 128
NEG = -1e30
HA, A_NOPE, A_ROPE, A_V, A_Q_RANK, A_KV_RANK = 8, 64, 32, 64, 256, 128
HB, HB_KV, B_HD = 8, 2, 64
HC, HC_KV, C_HD = 8, 2, 64
BRANCH_W = 512
N_BRANCH = 3
D_FF = 2816
CONV_W = 3
N_MOD = 6
IN_SIZES = (A_Q_RANK, A_KV_RANK, A_ROPE, HB * B_HD, HB_KV * B_HD, HB_KV * B_HD, HC * C_HD, HC_KV * C_HD,
            HC_KV * C_HD, N_BRANCH * D_MODEL)

LANES = 128
SUBLANES = 8
BF16_ROWS = 16
TOKEN_TILE = 256
MOD_ROWS = 16
HEAD_PAD = 128
V_ROWS = A_V + BF16_ROWS
FF_CHUNK = 256
KEY_CHUNK = 256
VMEM_LIMIT = 56 * 1024 * 1024
LOG2E = 1.4426950408889634


def _dot(a, b):
    return jnp.dot(a, b, preferred_element_type=F32)


def _dot_nt(a, b):
    return lax.dot_general(a, b, (((1,), (1,)), ((), ())), preferred_element_type=F32)


def _dot_tn(a, b):
    return lax.dot_general(a, b, (((0,), (0,)), ((), ())), preferred_element_type=F32)


def _sigmoid(x):
    return 1.0 / (1.0 + jnp.exp(-x))


def _modulated_norm(x, g, shift, scale):
    ms = jnp.mean(x * x, axis=-1, keepdims=True)
    return (x * lax.rsqrt(ms + EPS) * g) * (1.0 + scale) + shift


def _params(n_grid):
    return pltpu.CompilerParams(dimension_semantics=("arbitrary",) * n_grid, vmem_limit_bytes=VMEM_LIMIT)


def _full(shape):
    nd = len(shape)
    return pl.BlockSpec(shape, lambda *_: (0,) * nd)


def _layer(arr, l):
    nd = arr.ndim - 1
    return pl.BlockSpec((None,) + arr.shape[1:], lambda *_: (l,) + (0,) * nd)


def _mod_spec(l, tile0):
    return pl.BlockSpec((None, None, N_MOD, D_MODEL),
                        lambda bi, ti: (l, jnp.where(ti + tile0 == 0, MOD_ROWS // 2, bi), 0, 0))


def _token_specs(src, tile0):
    t = TOKEN_TILE
    if len(src) == 1:
        return [pl.BlockSpec((None, t, D_MODEL), lambda bi, ti: (bi, ti + tile0, 0))]
    assert tile0 == 0
    return [pl.BlockSpec((None, t, D_MODEL), lambda bi, ti: (bi, 0, 0)),
            pl.BlockSpec((None, t, D_MODEL), lambda bi, ti: (bi, jnp.maximum(ti - 1, 0), 0))]


def _token_tile(refs):
    if len(refs) == 1:
        return refs[0][...]
    return jnp.where(pl.program_id(1) == 0, refs[0][...], refs[1][...])


def _mod_kernel(c_ref, w_ref, b_ref, o_ref):
    c = c_ref[...]
    a = c * _sigmoid(c)
    o_ref[...] = jnp.dot(a, w_ref[...], precision=lax.Precision.HIGHEST, preferred_element_type=F32) + b_ref[...]


def _modulation(cvec, w_mod, b_mod):
    depth = w_mod.shape[0]
    n_col = N_MOD * D_MODEL
    out = pl.pallas_call(
        _mod_kernel,
        grid=(depth, N_MOD),
        in_specs=[
            pl.BlockSpec((MOD_ROWS, D_MODEL), lambda l, j: (0, 0)),
            pl.BlockSpec((None, D_MODEL, D_MODEL), lambda l, j: (l, 0, j)),
            pl.BlockSpec((None, 1, D_MODEL), lambda l, j: (l, 0, j)),
        ],
        out_specs=pl.BlockSpec((None, MOD_ROWS, D_MODEL), lambda l, j: (l, 0, j)),
        out_shape=jax.ShapeDtypeStruct((depth, MOD_ROWS, n_col), F32),
        compiler_params=_params(2),
        name="modulation",
    )(cvec, w_mod, b_mod.reshape(depth, 1, n_col))
    return out.reshape(depth, MOD_ROWS, N_MOD, D_MODEL)


STD_AKV, STD_BK, STD_BKR, STD_CK, STD_CKR, STD_KR, STD_KRR = (i * LANES for i in range(7))
STD_COLS = 7 * LANES
T_AQ = 0
T_AKV = T_AQ + A_Q_RANK
T_BQ = T_AKV + A_KV_RANK
T_CQ = T_BQ + HB * B_HD
T_BV = T_CQ + HC * C_HD
T_CV = T_BV + HB_KV * B_HD
T_ROWS = T_CV + HC_KV * C_HD


def _swap_halves(x, nf):
    parts = []
    for a in range(2):
        base = a * 2 * nf
        parts += [x[base + nf:base + 2 * nf], x[base:base + nf]]
    return jnp.concatenate(parts, axis=0)


def _ones_rows(t):
    row = lax.broadcasted_iota(jnp.int32, (BF16_ROWS, t), 0)
    return jnp.where(row == 0, 1.0, 0.0).astype(BF16)


def _proj_kernel(*refs, n_src, scale_a, scale_h):
    (mod_ref, g1_ref, wstd_ref, wt_ref, wqb_ref, wk_ref, wv_ref,
     gqa_ref, gkvr_ref, gkvc_ref, gqn_ref, gknr_ref, gknrr_ref,
     ck_ref, sk_ref, ca_ref, sa_ref, ctq_ref, stq_ref, cta_ref, sta_ref,
     qa_ref, ka_ref, va_ref, qb_ref, kb_ref, vb_ref, qc_ref, kc_ref, vc_ref) = refs[n_src:]
    t = TOKEN_TILE
    h = _modulated_norm(_token_tile(refs[:n_src]), g1_ref[...], mod_ref[0:1, :], mod_ref[1:2, :])
    hb = h.astype(BF16)
    ps = _dot(hb, wstd_ref[...])
    pt = _dot_nt(wt_ref[...], hb)
    ones = _ones_rows(t)

    aq = pt[T_AQ:T_AQ + A_Q_RANK]
    aqn = aq * lax.rsqrt(jnp.mean(aq * aq, axis=0, keepdims=True) + EPS) * gqa_ref[...]
    qt = _dot(wqb_ref[...], aqn.astype(BF16))
    cta, sta = cta_ref[...], sta_ref[...]
    qs = scale_a * LOG2E
    for hh in range(HA):
        base = hh * HEAD_PAD
        qa_ref[hh, 0:A_NOPE, :] = (qt[base:base + A_NOPE] * qs).astype(BF16)
        r = qt[base + A_NOPE:base + A_NOPE + A_ROPE]
        rr = r * cta + _swap_halves(r, A_ROPE // 4) * sta
        qa_ref[hh, A_NOPE:A_NOPE + A_ROPE, :] = (rr * qs).astype(BF16)
        qa_ref[hh, A_NOPE + A_ROPE:HEAD_PAD, :] = jnp.zeros((HEAD_PAD - A_NOPE - A_ROPE, t), BF16)

    akv = ps[:, STD_AKV:STD_AKV + LANES]
    akvn = akv * lax.rsqrt(jnp.mean(akv * akv, axis=-1, keepdims=True) + EPS) * gkvr_ref[...]
    kn = _dot(akvn.astype(BF16), wk_ref[...])
    kr = ps[:, STD_KR:STD_KR + LANES] * ca_ref[...] + ps[:, STD_KRR:STD_KRR + LANES] * sa_ref[...]
    for hh in range(HA):
        ka_ref[hh] = (kn[:, hh * HEAD_PAD:(hh + 1) * HEAD_PAD] + kr).astype(BF16)

    akvt = pt[T_AKV:T_AKV + A_KV_RANK]
    akvtn = akvt * lax.rsqrt(jnp.mean(akvt * akvt, axis=0, keepdims=True) + EPS) * gkvc_ref[...]
    vt = _dot(wv_ref[...], akvtn.astype(BF16))
    for hh in range(HA):
        va_ref[hh, 0:A_V, :] = vt[hh * A_V:(hh + 1) * A_V].astype(BF16)
        va_ref[hh, A_V:V_ROWS, :] = ones

    ctq, stq = ctq_ref[...], stq_ref[...]
    ck, sk = ck_ref[...], sk_ref[...]
    qsh = scale_h * LOG2E
    lane = lax.broadcasted_iota(jnp.int32, (t, LANES), 1)
    first = lane < B_HD

    for hh in range(HB):
        blk = pt[T_BQ + hh * B_HD:T_BQ + (hh + 1) * B_HD]
        y = blk * lax.rsqrt(jnp.mean(blk * blk, axis=0, keepdims=True) + EPS) * gqn_ref[...]
        qb_ref[hh] = ((y * ctq + _swap_halves(y, B_HD // 4) * stq) * qsh).astype(BF16)
        blk = pt[T_CQ + hh * C_HD:T_CQ + (hh + 1) * C_HD]
        qc_ref[hh] = ((blk * ctq + _swap_halves(blk, C_HD // 4) * stq) * qsh).astype(BF16)

    bk = ps[:, STD_BK:STD_BK + LANES]
    sq = bk * bk
    s0 = jnp.sum(jnp.where(first, sq, 0.0), axis=-1, keepdims=True)
    s1 = jnp.sum(jnp.where(first, 0.0, sq), axis=-1, keepdims=True)
    rk = lax.rsqrt(jnp.where(first, s0, s1) * (1.0 / B_HD) + EPS)
    kb = (bk * rk * gknr_ref[...]) * ck + (ps[:, STD_BKR:STD_BKR + LANES] * rk * gknrr_ref[...]) * sk
    kc = ps[:, STD_CK:STD_CK + LANES] * ck + ps[:, STD_CKR:STD_CKR + LANES] * sk
    for g in range(HB_KV):
        kb_ref[g] = kb[:, g * B_HD:(g + 1) * B_HD].astype(BF16)
        kc_ref[g] = kc[:, g * C_HD:(g + 1) * C_HD].astype(BF16)
        vb_ref[g, 0:B_HD, :] = pt[T_BV + g * B_HD:T_BV + (g + 1) * B_HD].astype(BF16)
        vb_ref[g, B_HD:V_ROWS, :] = ones
        vc_ref[g, 0:C_HD, :] = pt[T_CV + g * C_HD:T_CV + (g + 1) * C_HD].astype(BF16)
        vc_ref[g, C_HD:V_ROWS, :] = ones


def _project(src, mods, l, wts, tabs, scale_a, scale_h):
    b = src[-1].shape[0]
    s = sum(a.shape[1] for a in src)
    t = TOKEN_TILE
    n_t = s // t
    tok2 = lambda bi, ti: (ti, 0)
    feat2 = lambda bi, ti: (0, ti)
    names = ("g1", "w_std", "w_t", "w_qb", "w_k", "w_v", "g_qa_col", "g_kv_row", "g_kv_col", "g_qn_col", "g_kn_row",
             "g_kn_rot_row")
    in_specs = _token_specs(src, 0) + [_mod_spec(l, 0)] + [_layer(wts[n], l) for n in names] + [
        pl.BlockSpec((t, LANES), tok2), pl.BlockSpec((t, LANES), tok2),
        pl.BlockSpec((t, LANES), tok2), pl.BlockSpec((t, LANES), tok2),
        pl.BlockSpec((B_HD, t), feat2), pl.BlockSpec((B_HD, t), feat2),
        pl.BlockSpec((A_ROPE, t), feat2), pl.BlockSpec((A_ROPE, t), feat2),
    ]
    qspec = lambda heads, rows: pl.BlockSpec((None, heads, rows, t), lambda bi, ti: (bi, 0, 0, ti))
    kspec = lambda heads, cols: pl.BlockSpec((None, heads, t, cols), lambda bi, ti: (bi, 0, ti, 0))
    out_specs = [
        qspec(HA, HEAD_PAD), kspec(HA, HEAD_PAD), qspec(HA, V_ROWS),
        qspec(HB, B_HD), kspec(HB_KV, B_HD), qspec(HB_KV, V_ROWS),
        qspec(HC, C_HD), kspec(HC_KV, C_HD), qspec(HC_KV, V_ROWS),
    ]
    sd = jax.ShapeDtypeStruct
    out_shape = [
        sd((b, HA, HEAD_PAD, s), BF16), sd((b, HA, s, HEAD_PAD), BF16), sd((b, HA, V_ROWS, s), BF16),
        sd((b, HB, B_HD, s), BF16), sd((b, HB_KV, s, B_HD), BF16), sd((b, HB_KV, V_ROWS, s), BF16),
        sd((b, HC, C_HD, s), BF16), sd((b, HC_KV, s, C_HD), BF16), sd((b, HC_KV, V_ROWS, s), BF16),
    ]
    return pl.pallas_call(
        functools.partial(_proj_kernel, n_src=len(src), scale_a=scale_a, scale_h=scale_h),
        grid=(b, n_t),
        in_specs=in_specs,
        out_specs=out_specs,
        out_shape=out_shape,
        compiler_params=_params(2),
        name="projection",
    )(*src, mods, *(wts[n] for n in names),
      tabs["ck"], tabs["sk"], tabs["ca"], tabs["sa"], tabs["ctq"], tabs["stq"], tabs["cta"], tabs["sta"])


def _attn_kernel(*refs, n_heads, group, has_sink):
    if has_sink:
        q_ref, qn_ref, k_ref, v_ref, sink_ref, o_ref, s_scr, m_scr = refs
    else:
        q_ref, qn_ref, k_ref, v_ref, o_ref, s_scr, m_scr = refs
    n_keys, tq = s_scr.shape[1], s_scr.shape[2]
    chunk = min(KEY_CHUNK, n_keys)
    n_chunks = n_keys // chunk

    def step(h_s, h_o, m_o):
        if h_s is not None:
            head_s, slot_s = h_s
            q = jnp.where(head_s == n_heads, qn_ref[0], q_ref[jnp.minimum(head_s, n_heads - 1)])
            head_s = head_s % n_heads
            mrun = None
        if h_o is not None:
            head_o, slot_o = h_o
            if has_sink:
                snk = sink_ref[head_o]
                m_o = jnp.maximum(m_o, snk)
            acc = None
        for c in range(n_chunks):
            rows = slice(c * chunk, (c + 1) * chunk)
            if h_s is not None:
                s = _dot(k_ref[head_s // group, rows, :], q)
                s_scr[slot_s, rows, :] = s
                part = jnp.max(s.reshape(chunk // SUBLANES, SUBLANES, tq), axis=0)
                mrun = part if mrun is None else jnp.maximum(mrun, part)
            if h_o is not None:
                p = jnp.exp2(s_scr[slot_o, rows, :] - m_o).astype(BF16)
                part = _dot(v_ref[head_o // group, :, rows], p)
                acc = part if acc is None else acc + part
        if h_o is not None:
            denom = acc[A_V:A_V + 1]
            if has_sink:
                denom = denom + jnp.exp2(snk - m_o)
            out = acc[0:A_V] * (1.0 / denom)
            o_ref[pl.ds(pl.multiple_of(head_o * A_V, A_V), A_V), :] = out.astype(BF16)
        if h_s is not None:
            return jnp.max(mrun, axis=0, keepdims=True)

    def head_pair(j, m_even):
        h0 = 2 * j
        m_odd = step((h0 + 1, 1), (h0, 0), m_even)
        return step((h0 + 2, 0), (h0 + 1, 1), m_odd)

    @pl.when(pl.program_id(1) == 0)
    def _():
        m_scr[...] = step((0, 0), None, None)

    m_scr[...] = lax.fori_loop(0, n_heads // 2, head_pair, m_scr[...], unroll=True)


def _attention(qt, k, vt, sink2, *, n_keys, tile0, n_tiles, out=None):
    b, n_heads, dk, s = qt.shape
    hk = k.shape[1]
    t = TOKEN_TILE
    in_specs = [
        pl.BlockSpec((None, n_heads, dk, t), lambda bi, ti: (bi, 0, 0, ti + tile0)),
        pl.BlockSpec((None, 1, dk, t), lambda bi, ti: (bi, 0, 0, jnp.minimum(ti + 1, n_tiles - 1) + tile0)),
        pl.BlockSpec((None, hk, n_keys, k.shape[3]), lambda bi, ti: (bi, 0, 0, 0)),
        pl.BlockSpec((None, hk, V_ROWS, n_keys), lambda bi, ti: (bi, 0, 0, 0)),
    ]
    args = [qt, qt, k, vt]
    if sink2 is not None:
        in_specs.append(_full(sink2.shape))
        args.append(sink2)
    aliases = {}
    if out is not None:
        in_specs.append(pl.BlockSpec(memory_space=pl.ANY))
        args.append(out)
        aliases = {len(args) - 1: 0}
    kern = functools.partial(_attn_kernel, n_heads=n_heads, group=n_heads // hk, has_sink=sink2 is not None)
    if out is not None:
        kern = _drop_last_input(kern, len(args))
    return pl.pallas_call(
        kern,
        grid=(b, n_tiles),
        in_specs=in_specs,
        out_specs=pl.BlockSpec((None, n_heads * A_V, t), lambda bi, ti: (bi, 0, ti + tile0)),
        out_shape=jax.ShapeDtypeStruct((b, n_heads * A_V, s), BF16),
        input_output_aliases=aliases,
        scratch_shapes=[pltpu.VMEM((2, n_keys, t), F32), pltpu.VMEM((1, t), F32)],
        compiler_params=_params(2),
        name="attention",
    )(*args)


def _drop_last_input(kern, n_in):
    def wrapped(*refs):
        return kern(*refs[:n_in - 1], *refs[n_in:])
    return wrapped


def _window_kernel(q_ref, k_ref, v_ref, sink_ref, o_ref, k_scr, v_scr, cap_scr, s_scr, *, n_ctx, n_tiles):
    tq = q_ref.shape[2]
    ti = pl.program_id(1)
    start = pl.multiple_of(n_ctx + ti * tq, LANES)
    lo = pl.multiple_of(start - WINDOW, LANES)
    nxt = pl.multiple_of(jnp.minimum(start + tq, n_ctx + (n_tiles - 1) * tq + tq - WINDOW), LANES)
    n_band = WINDOW + tq
    n_keys = n_ctx + n_band + WINDOW

    row = lax.broadcasted_iota(jnp.int32, (n_keys, tq), 0)
    col = lax.broadcasted_iota(jnp.int32, (n_keys, tq), 1)
    rel = row - n_ctx - WINDOW - col
    in_band = (rel >= -WINDOW) & (rel <= WINDOW)
    ok_lo = (ti > 0) | (row >= n_ctx + WINDOW)
    ok_hi = (ti < n_tiles - 1) | (row < n_ctx + n_band)
    valid = (row < n_ctx) | (in_band & ok_lo & ok_hi)
    cap_scr[...] = jnp.where(valid, jnp.inf, NEG)

    for g in range(HC_KV):
        k_scr[g, 0:n_ctx, :] = k_ref[g, 0:n_ctx, :]
        k_scr[g, n_ctx:n_ctx + n_band, :] = k_ref[g, pl.ds(lo, n_band), :]
        k_scr[g, n_ctx + n_band:n_keys, :] = k_ref[g, pl.ds(nxt, WINDOW), :]
        v_scr[g, :, 0:n_ctx] = v_ref[g, :, 0:n_ctx]
        v_scr[g, :, n_ctx:n_ctx + n_band] = v_ref[g, :, pl.ds(lo, n_band)]
        v_scr[g, :, n_ctx + n_band:n_keys] = v_ref[g, :, pl.ds(nxt, WINDOW)]

    group = HC // HC_KV
    chunk = KEY_CHUNK
    n_chunks = n_keys // chunk

    def scores(hh):
        q = q_ref[hh]
        mrun = None
        for c in range(n_chunks):
            rows = slice(c * chunk, (c + 1) * chunk)
            s = jnp.minimum(_dot(k_scr[hh // group, rows, :], q), cap_scr[rows, :])
            s_scr[hh % 2, rows, :] = s
            part = jnp.max(s.reshape(chunk // SUBLANES, SUBLANES, tq), axis=0)
            mrun = part if mrun is None else jnp.maximum(mrun, part)
        return jnp.max(mrun, axis=0, keepdims=True)

    def output(hh, m):
        snk = sink_ref[hh]
        m = jnp.maximum(m, snk)
        acc = None
        for c in range(n_chunks):
            rows = slice(c * chunk, (c + 1) * chunk)
            p = jnp.exp2(s_scr[hh % 2, rows, :] - m).astype(BF16)
            part = _dot(v_scr[hh // group, :, rows], p)
            acc = part if acc is None else acc + part
        denom = acc[C_HD:C_HD + 1] + jnp.exp2(snk - m)
        o_ref[hh * C_HD:(hh + 1) * C_HD, :] = (acc[0:C_HD] * (1.0 / denom)).astype(BF16)

    m = scores(0)
    for hh in range(HC):
        m_next = scores(hh + 1) if hh + 1 < HC else None
        output(hh, m)
        m = m_next


def _window_attention(qt, k, vt, sink2, *, n_ctx):
    b, n_heads, dk, s = qt.shape
    hk = k.shape[1]
    t = TOKEN_TILE
    tile0 = n_ctx // t
    n_tiles = (s - n_ctx) // t
    n_win_keys = n_ctx + WINDOW + t + WINDOW
    return pl.pallas_call(
        functools.partial(_window_kernel, n_ctx=n_ctx, n_tiles=n_tiles),
        grid=(b, n_tiles),
        in_specs=[
            pl.BlockSpec((None, n_heads, dk, t), lambda bi, ti: (bi, 0, 0, ti + tile0)),
            pl.BlockSpec((None, hk, s, dk), lambda bi, ti: (bi, 0, 0, 0)),
            pl.BlockSpec((None, hk, V_ROWS, s), lambda bi, ti: (bi, 0, 0, 0)),
            _full(sink2.shape),
        ],
        out_specs=pl.BlockSpec((None, n_heads * C_HD, t), lambda bi, ti: (bi, 0, ti + tile0)),
        out_shape=jax.ShapeDtypeStruct((b, n_heads * C_HD, s), BF16),
        scratch_shapes=[
            pltpu.VMEM((hk, n_win_keys, dk), BF16), pltpu.VMEM((hk, V_ROWS, n_win_keys), BF16),
            pltpu.VMEM((n_win_keys, t), F32), pltpu.VMEM((2, n_win_keys, t), F32),
        ],
        compiler_params=_params(2),
        name="window_attention",
    )(qt, k, vt, sink2)


def _merge_kernel(*refs, n_src):
    mod_ref, g1_ref, oa_ref, ob_ref, oc_ref, wg_ref, bg_ref, wbr_ref, wout_ref, xo_ref = refs[n_src:]
    x = _token_tile(refs[:n_src])
    h = _modulated_norm(x, g1_ref[...], mod_ref[0:1, :], mod_ref[1:2, :])
    hb = h.astype(BF16)
    y = None
    for i, o_ref in enumerate((oa_ref, ob_ref, oc_ref)):
        cols = slice(i * D_MODEL, (i + 1) * D_MODEL)
        gate = _sigmoid(_dot(hb, wg_ref[:, cols]) + bg_ref[:, cols])
        term = gate * _dot_tn(o_ref[...], wbr_ref[i])
        y = term if y is None else y + term
    z = _dot(y.astype(BF16), wout_ref[...])
    xo_ref[...] = x + mod_ref[2:3, :] * z


def _merge(src, mods, l, wts, oa, ob, oc, *, tile0, n_tiles):
    b = src[-1].shape[0]
    s = sum(a.shape[1] for a in src)
    t = TOKEN_TILE
    feat = lambda bi, ti: (bi, 0, ti + tile0)
    names = ("w_gate", "b_gate", "w_branch", "w_out")
    return pl.pallas_call(
        functools.partial(_merge_kernel, n_src=len(src)),
        grid=(b, n_tiles),
        in_specs=_token_specs(src, tile0) + [_mod_spec(l, tile0), _layer(wts["g1"], l)] + [
            pl.BlockSpec((None, BRANCH_W, t), feat),
            pl.BlockSpec((None, BRANCH_W, t), feat),
            pl.BlockSpec((None, BRANCH_W, t), feat),
        ] + [_layer(wts[n], l) for n in names],
        out_specs=pl.BlockSpec((None, t, D_MODEL), lambda bi, ti: (bi, ti + tile0, 0)),
        out_shape=jax.ShapeDtypeStruct((b, s, D_MODEL), F32),
        input_output_aliases={0: 0} if len(src) == 1 else {},
        compiler_params=_params(2),
        name="merge",
    )(*src, mods, wts["g1"], oa, ob, oc, *(wts[n] for n in names))


def _ffn_kernel(xp_ref, x_ref, xn_ref, mod_ref, g2_ref, wup_ref, wconv_ref, bconv_ref, wdown_ref, gf_ref,
                o_ref, act_ref, *, tile0, n_seq_tiles, final_norm):
    t = x_ref.shape[0]
    halo = SUBLANES
    ta = pl.program_id(1) + tile0
    x = x_ref[...]
    xa = jnp.concatenate([xp_ref[...], x, xn_ref[...]], axis=0)
    h = _modulated_norm(xa, g2_ref[...], mod_ref[3:4, :], mod_ref[4:5, :])
    row = lax.broadcasted_iota(jnp.int32, (t + 2 * halo, 1), 0)
    keep_prev = (ta > 1).astype(F32)
    keep_next = ((ta > 0) & (ta < n_seq_tiles - 1)).astype(F32)
    keep = jnp.where(row < halo, keep_prev, jnp.where(row >= t + halo, keep_next, 1.0))
    hb = (h * keep).astype(BF16)

    def conv(u, cols):
        return (u[halo - 1:halo - 1 + t] * wconv_ref[0:1, cols] + u[halo:halo + t] * wconv_ref[1:2, cols]
                + u[halo + 1:halo + 1 + t] * wconv_ref[2:3, cols] + bconv_ref[:, cols])

    for c in range(D_FF // FF_CHUNK):
        ca = slice(c * FF_CHUNK, (c + 1) * FF_CHUNK)
        cg = slice(D_FF + c * FF_CHUNK, D_FF + (c + 1) * FF_CHUNK)
        a = conv(_dot(hb, wup_ref[:, ca]), ca)
        gv = conv(_dot(hb, wup_ref[:, cg]), cg)
        act_ref[:, ca] = (a * _sigmoid(a) * gv).astype(BF16)
    y = x + mod_ref[5:6, :] * _dot(act_ref[...], wdown_ref[...])
    if final_norm:
        y = y * lax.rsqrt(jnp.mean(y * y, axis=-1, keepdims=True) + EPS) * gf_ref[...]
    o_ref[...] = y


def _conv_ffn(xc, mods, l, wts, g_final, *, tile0, n_tiles, final_norm):
    b, s, _ = xc.shape
    t = TOKEN_TILE
    n_seq_tiles = s // t
    per = t // SUBLANES
    last_blk = s // SUBLANES - 1
    out_rows = n_tiles * t if final_norm else s
    out_tile0 = 0 if final_norm else tile0
    return pl.pallas_call(
        functools.partial(_ffn_kernel, tile0=tile0, n_seq_tiles=n_seq_tiles, final_norm=final_norm),
        grid=(b, n_tiles),
        in_specs=[
            pl.BlockSpec((None, SUBLANES, D_MODEL), lambda bi, ti: (bi, jnp.maximum((ti + tile0) * per - 1, 0), 0)),
            pl.BlockSpec((None, t, D_MODEL), lambda bi, ti: (bi, ti + tile0, 0)),
            pl.BlockSpec((None, SUBLANES, D_MODEL),
                         lambda bi, ti: (bi, jnp.minimum((ti + tile0 + 1) * per, last_blk), 0)),
            _mod_spec(l, tile0),
            _layer(wts["g2"], l), _layer(wts["w_up"], l), _layer(wts["w_conv"], l), _layer(wts["b_conv"], l),
            _layer(wts["w_down"], l),
            _full((1, D_MODEL)),
        ],
        out_specs=pl.BlockSpec((None, t, D_MODEL), lambda bi, ti: (bi, ti + out_tile0, 0)),
        out_shape=jax.ShapeDtypeStruct((b, out_rows, D_MODEL), F32),
        scratch_shapes=[pltpu.VMEM((t, D_FF), BF16)],
        compiler_params=_params(2),
        name="conv_ffn",
    )(xc, xc, xc, mods, wts["g2"], wts["w_up"], wts["w_conv"], wts["b_conv"], wts["w_down"], g_final)


def _partner(head_dim):
    nf = head_dim // 4
    d = np.arange(head_dim)
    a, half, f = d // (2 * nf), (d % (2 * nf)) // nf, d % nf
    return a * 2 * nf + (1 - half) * nf + f, np.where(half == 0, -1.0, 1.0).astype(np.float32)


def _rope_tables(n_ctx, n_lat):
    def full(head_dim):
        nf = head_dim // 4
        _, sign = _partner(head_dim)
        rows = n_lat // GRID_W
        row = jnp.repeat(jnp.arange(rows, dtype=F32), GRID_W)
        col = jnp.tile(jnp.arange(GRID_W, dtype=F32), rows)
        inv = ROPE_THETA ** (-jnp.arange(nf, dtype=F32) / nf)
        ang = jnp.stack([row[:, None] * inv, col[:, None] * inv], axis=1)
        c = jnp.broadcast_to(jnp.cos(ang)[:, :, None, :], (n_lat, 2, 2, nf)).reshape(n_lat, head_dim)
        s = jnp.broadcast_to(jnp.sin(ang)[:, :, None, :], (n_lat, 2, 2, nf)).reshape(n_lat, head_dim) * sign
        c = jnp.concatenate([jnp.ones((n_ctx, head_dim), F32), c], axis=0)
        s = jnp.concatenate([jnp.zeros((n_ctx, head_dim), F32), s], axis=0)
        return c, s

    c64, s64 = full(B_HD)
    c32, s32 = full(A_ROPE)
    pad_a = lambda v: jnp.pad(v, ((0, 0), (A_NOPE, LANES - A_NOPE - A_ROPE)))
    return {
        "ck": jnp.tile(c64, (1, LANES // B_HD)), "sk": jnp.tile(s64, (1, LANES // B_HD)),
        "ca": pad_a(c32), "sa": pad_a(s32),
        "ctq": c64.T, "stq": s64.T, "cta": c32.T, "sta": s32.T,
    }


def _rot_ranges(base, head_dim, n_heads):
    nf = head_dim // 4
    out = []
    for hh in range(n_heads):
        b0 = base + hh * head_dim
        for a in range(2):
            out += [(b0 + a * 2 * nf + nf, b0 + a * 2 * nf + 2 * nf), (b0 + a * 2 * nf, b0 + a * 2 * nf + nf)]
    return out


def _prepare_weights(w_in, b_gate, g_norm1, g_q_a, w_q_b, g_kv_a, w_kv_b, g_qn, g_kn, w_branch, w_out, g_norm2,
                     w_up, w_conv, b_conv, w_down):
    depth = w_in.shape[0]
    cuts = [int(v) for v in np.cumsum((0,) + IN_SIZES)]
    aq, akv, akr, bq, bk, bv, cq, ck, cv, gl = ((cuts[i], cuts[i + 1]) for i in range(len(IN_SIZES)))
    cols = lambda ranges: [w_in[:, :, lo:hi] for lo, hi in ranges]
    zeros = lambda n: [jnp.zeros((depth, D_MODEL, n), F32)]
    pad_a = lambda pieces: zeros(A_NOPE) + pieces + zeros(LANES - A_NOPE - A_ROPE)
    w_std = jnp.concatenate(
        cols([akv, bk]) + cols(_rot_ranges(bk[0], B_HD, HB_KV)) + cols([ck]) + cols(_rot_ranges(ck[0], C_HD, HC_KV))
        + pad_a(cols([akr])) + pad_a(cols(_rot_ranges(akr[0], A_ROPE, 1))), axis=2)
    w_t = jnp.swapaxes(jnp.concatenate(cols([aq, akv, bq, cq, bv, cv]), axis=2), 1, 2)
    wqb = w_q_b.reshape(depth, A_Q_RANK, HA, A_NOPE + A_ROPE)
    wqb = jnp.pad(wqb, ((0, 0), (0, 0), (0, 0), (0, HEAD_PAD - A_NOPE - A_ROPE)))
    wqb = jnp.swapaxes(wqb.reshape(depth, A_Q_RANK, HA * HEAD_PAD), 1, 2)
    wkv = w_kv_b.reshape(depth, A_KV_RANK, HA, A_NOPE + A_V)
    w_k = jnp.pad(wkv[..., :A_NOPE], ((0, 0), (0, 0), (0, 0), (0, HEAD_PAD - A_NOPE)))
    w_k = w_k.reshape(depth, A_KV_RANK, HA * HEAD_PAD)
    w_v = jnp.swapaxes(wkv[..., A_NOPE:].reshape(depth, A_KV_RANK, HA * A_V), 1, 2)
    g_kn2 = jnp.tile(g_kn, (1, LANES // B_HD))
    p64, _ = _partner(B_HD)
    p128 = np.concatenate([p64 + i * B_HD for i in range(LANES // B_HD)])
    return {
        "g1": g_norm1.reshape(depth, 1, D_MODEL), "g2": g_norm2.reshape(depth, 1, D_MODEL),
        "w_std": w_std.astype(BF16), "w_t": w_t.astype(BF16), "w_qb": wqb.astype(BF16),
        "w_k": w_k.astype(BF16), "w_v": w_v.astype(BF16),
        "g_qa_col": g_q_a.reshape(depth, A_Q_RANK, 1), "g_kv_row": g_kv_a.reshape(depth, 1, A_KV_RANK),
        "g_kv_col": g_kv_a.reshape(depth, A_KV_RANK, 1), "g_qn_col": g_qn.reshape(depth, B_HD, 1),
        "g_kn_row": g_kn2.reshape(depth, 1, LANES), "g_kn_rot_row": g_kn2[:, p128].reshape(depth, 1, LANES),
        "w_gate": w_in[:, :, gl[0]:gl[1]].astype(BF16), "b_gate": b_gate.reshape(depth, 1, N_BRANCH * D_MODEL),
        "w_branch": w_branch.astype(BF16), "w_out": w_out.astype(BF16),
        "w_up": w_up.astype(BF16), "w_conv": w_conv, "b_conv": b_conv.reshape(depth, 1, 2 * D_FF),
        "w_down": w_down.astype(BF16),
    }


def kernel(x, c, ctx, c_ctx, w_mod, b_mod, g_norm1, w_in, b_gate, g_q_a, w_q_b, g_kv_a, w_kv_b, g_qn, g_kn, sink,
           w_branch, w_out, g_norm2, w_up, w_conv, b_conv, w_down, g_final):
    b, n_lat, d = x.shape
    n_ctx = ctx.shape[1]
    depth = w_mod.shape[0]
    t = TOKEN_TILE
    assert d == D_MODEL and n_ctx == t and n_lat % t == 0 and n_lat % GRID_W == 0 and b <= MOD_ROWS // 2
    assert depth >= 2
    s = n_ctx + n_lat
    n_t = s // t
    scale_a = 1.0 / math.sqrt(A_NOPE + A_ROPE)
    scale_h = 1.0 / math.sqrt(B_HD)

    cvec = jnp.zeros((MOD_ROWS, D_MODEL), F32).at[:b].set(c).at[MOD_ROWS // 2].set(c_ctx)
    mods = _modulation(cvec, w_mod, b_mod)
    tabs = _rope_tables(n_ctx, n_lat)
    wts = _prepare_weights(w_in, b_gate, g_norm1, g_q_a, w_q_b, g_kv_a, w_kv_b, g_qn, g_kn, w_branch, w_out, g_norm2,
                           w_up, w_conv, b_conv, w_down)
    gf = g_final.reshape(1, D_MODEL)
    sinks = jnp.broadcast_to((sink * LOG2E).reshape(depth, HC, 1, 1), (depth, HC, 1, t)).astype(F32)

    src = (ctx, x)
    for l in range(depth):
        last = l == depth - 1
        qa, ka, va, qb, kb, vb, qc, kc, vc = _project(src, mods, l, wts, tabs, scale_a, scale_h)
        lat = dict(n_keys=s, tile0=1, n_tiles=n_t - 1)
        oa = _attention(qa, ka, va, None, **lat)
        ob = _attention(qb, kb, vb, None, **lat)
        oc = _window_attention(qc, kc, vc, sinks[l], n_ctx=n_ctx)
        if not last:
            cx = dict(n_keys=n_ctx, tile0=0, n_tiles=1)
            oa = _attention(qa, ka, va, None, out=oa, **cx)
            ob = _attention(qb, kb, vb, None, out=ob, **cx)
            oc = _attention(qc, kc, vc, sinks[l], out=oc, **cx)
        tiles = dict(tile0=1, n_tiles=n_t - 1) if last else dict(tile0=0, n_tiles=n_t)
        xc = _merge(src, mods, l, wts, oa, ob, oc, **tiles)
        src = (_conv_ffn(xc, mods, l, wts, gf, final_norm=last, **tiles),)
    return src[0]
```

```python
import functools
import math

import jax
import jax.numpy as jnp
import numpy as np
from jax import lax
from jax.experimental import pallas as pl
from jax.experimental.pallas import tpu as pltpu

F32 = jnp.float32
BF16 = jnp.bfloat16

D_MODEL = 1024
GRID_W = 64
EPS = 1e-6
ROPE_THETA = 10000.0
WINDOW = 128
NEG = -1e30
HA, A_NOPE, A_ROPE, A_V, A_Q_RANK, A_KV_RANK = 8, 64, 32, 64, 256, 128
HB, HB_KV, B_HD = 8, 2, 64
HC, HC_KV, C_HD = 8, 2, 64
BRANCH_W = 512
N_BRANCH = 3
D_FF = 2816
CONV_W = 3
N_MOD = 6
IN_SIZES = (A_Q_RANK, A_KV_RANK, A_ROPE, HB * B_HD, HB_KV * B_HD, HB_KV * B_HD, HC * C_HD, HC_KV * C_HD,
            HC_KV * C_HD, N_BRANCH * D_MODEL)

LANES = 128
SUBLANES = 8
BF16_ROWS = 16
TOKEN_TILE = 256
MOD_ROWS = 16
HEAD_PAD = 128
V_ROWS = A_V + BF16_ROWS
FF_CHUNK = 256
KEY_CHUNK = 256
OUTPUT_LAG = 3
WINDOW_LAG = 1
VMEM_LIMIT = 56 * 1024 * 1024
LOG2E = 1.4426950408889634


def _dot(a, b):
    return jnp.dot(a, b, preferred_element_type=F32)


def _dot_nt(a, b):
    return lax.dot_general(a, b, (((1,), (1,)), ((), ())), preferred_element_type=F32)


def _dot_tn(a, b):
    return lax.dot_general(a, b, (((0,), (0,)), ((), ())), preferred_element_type=F32)


def _sigmoid(x):
    return 1.0 / (1.0 + jnp.exp(-x))


def _modulated_norm(x, g, shift, scale):
    ms = jnp.mean(x * x, axis=-1, keepdims=True)
    return (x * lax.rsqrt(ms + EPS) * g) * (1.0 + scale) + shift


def _params(n_grid):
    return pltpu.CompilerParams(dimension_semantics=("arbitrary",) * n_grid, vmem_limit_bytes=VMEM_LIMIT)


def _full(shape):
    nd = len(shape)
    return pl.BlockSpec(shape, lambda *_: (0,) * nd)


def _layer(arr, l):
    nd = arr.ndim - 1
    return pl.BlockSpec((None,) + arr.shape[1:], lambda *_: (l,) + (0,) * nd)


def _mod_spec(l, tile0):
    return pl.BlockSpec((None, None, N_MOD, D_MODEL),
                        lambda bi, ti: (l, jnp.where(ti + tile0 == 0, MOD_ROWS // 2, bi), 0, 0))


def _token_specs(src, tile0):
    t = TOKEN_TILE
    if len(src) == 1:
        return [pl.BlockSpec((None, t, D_MODEL), lambda bi, ti: (bi, ti + tile0, 0))]
    assert tile0 == 0
    return [pl.BlockSpec((None, t, D_MODEL), lambda bi, ti: (bi, 0, 0)),
            pl.BlockSpec((None, t, D_MODEL), lambda bi, ti: (bi, jnp.maximum(ti - 1, 0), 0))]


def _token_tile(refs):
    if len(refs) == 1:
        return refs[0][...]
    return jnp.where(pl.program_id(1) == 0, refs[0][...], refs[1][...])


def _mod_kernel(c_ref, w_ref, b_ref, o_ref):
    c = c_ref[...]
    a = c * _sigmoid(c)
    o_ref[...] = jnp.dot(a, w_ref[...], precision=lax.Precision.HIGHEST, preferred_element_type=F32) + b_ref[...]


def _modulation(cvec, w_mod, b_mod):
    depth = w_mod.shape[0]
    n_col = N_MOD * D_MODEL
    out = pl.pallas_call(
        _mod_kernel,
        grid=(depth, N_MOD),
        in_specs=[
            pl.BlockSpec((MOD_ROWS, D_MODEL), lambda l, j: (0, 0)),
            pl.BlockSpec((None, D_MODEL, D_MODEL), lambda l, j: (l, 0, j)),
            pl.BlockSpec((None, 1, D_MODEL), lambda l, j: (l, 0, j)),
        ],
        out_specs=pl.BlockSpec((None, MOD_ROWS, D_MODEL), lambda l, j: (l, 0, j)),
        out_shape=jax.ShapeDtypeStruct((depth, MOD_ROWS, n_col), F32),
        compiler_params=_params(2),
        name="modulation",
    )(cvec, w_mod, b_mod.reshape(depth, 1, n_col))
    return out.reshape(depth, MOD_ROWS, N_MOD, D_MODEL)


STD_AKV, STD_BK, STD_BKR, STD_CK, STD_CKR, STD_KR, STD_KRR = (i * LANES for i in range(7))
STD_COLS = 7 * LANES
T_AQ = 0
T_AKV = T_AQ + A_Q_RANK
T_BQ = T_AKV + A_KV_RANK
T_CQ = T_BQ + HB * B_HD
T_BV = T_CQ + HC * C_HD
T_CV = T_BV + HB_KV * B_HD
T_ROWS = T_CV + HC_KV * C_HD


def _swap_halves(x, nf):
    parts = []
    for a in range(2):
        base = a * 2 * nf
        parts += [x[base + nf:base + 2 * nf], x[base:base + nf]]
    return jnp.concatenate(parts, axis=0)


def _ones_rows(t):
    row = lax.broadcasted_iota(jnp.int32, (BF16_ROWS, t), 0)
    return jnp.where(row == 0, 1.0, 0.0).astype(BF16)


def _proj_kernel(*refs, n_src, scale_a, scale_h):
    (mod_ref, g1_ref, wstd_ref, wt_ref, wqb_ref, wk_ref, wv_ref,
     gqa_ref, gkvr_ref, gkvc_ref, gqn_ref, gknr_ref, gknrr_ref,
     ck_ref, sk_ref, ca_ref, sa_ref, ctq_ref, stq_ref, cta_ref, sta_ref,
     qa_ref, ka_ref, va_ref, qb_ref, kb_ref, vb_ref, qc_ref, kc_ref, vc_ref) = refs[n_src:]
    t = TOKEN_TILE
    h = _modulated_norm(_token_tile(refs[:n_src]), g1_ref[...], mod_ref[0:1, :], mod_ref[1:2, :])
    hb = h.astype(BF16)
    ps = _dot(hb, wstd_ref[...])
    pt = _dot_nt(wt_ref[...], hb)
    ones = _ones_rows(t)

    aq = pt[T_AQ:T_AQ + A_Q_RANK]
    aqn = aq * lax.rsqrt(jnp.mean(aq * aq, axis=0, keepdims=True) + EPS) * gqa_ref[...]
    qt = _dot(wqb_ref[...], aqn.astype(BF16))
    cta, sta = cta_ref[...], sta_ref[...]
    qs = scale_a * LOG2E
    for hh in range(HA):
        base = hh * HEAD_PAD
        qa_ref[hh, 0:A_NOPE, :] = (qt[base:base + A_NOPE] * qs).astype(BF16)
        r = qt[base + A_NOPE:base + A_NOPE + A_ROPE]
        rr = r * cta + _swap_halves(r, A_ROPE // 4) * sta
        qa_ref[hh, A_NOPE:A_NOPE + A_ROPE, :] = (rr * qs).astype(BF16)
        qa_ref[hh, A_NOPE + A_ROPE:HEAD_PAD, :] = jnp.zeros((HEAD_PAD - A_NOPE - A_ROPE, t), BF16)

    akv = ps[:, STD_AKV:STD_AKV + LANES]
    akvn = akv * lax.rsqrt(jnp.mean(akv * akv, axis=-1, keepdims=True) + EPS) * gkvr_ref[...]
    kn = _dot(akvn.astype(BF16), wk_ref[...])
    kr = ps[:, STD_KR:STD_KR + LANES] * ca_ref[...] + ps[:, STD_KRR:STD_KRR + LANES] * sa_ref[...]
    for hh in range(HA):
        ka_ref[hh] = (kn[:, hh * HEAD_PAD:(hh + 1) * HEAD_PAD] + kr).astype(BF16)

    akvt = pt[T_AKV:T_AKV + A_KV_RANK]
    akvtn = akvt * lax.rsqrt(jnp.mean(akvt * akvt, axis=0, keepdims=True) + EPS) * gkvc_ref[...]
    vt = _dot(wv_ref[...], akvtn.astype(BF16))
    for hh in range(HA):
        va_ref[hh, 0:A_V, :] = vt[hh * A_V:(hh + 1) * A_V].astype(BF16)
        va_ref[hh, A_V:V_ROWS, :] = ones

    ctq, stq = ctq_ref[...], stq_ref[...]
    ck, sk = ck_ref[...], sk_ref[...]
    qsh = scale_h * LOG2E
    lane = lax.broadcasted_iota(jnp.int32, (t, LANES), 1)
    first = lane < B_HD

    for hh in range(HB):
        blk = pt[T_BQ + hh * B_HD:T_BQ + (hh + 1) * B_HD]
        y = blk * lax.rsqrt(jnp.mean(blk * blk, axis=0, keepdims=True) + EPS) * gqn_ref[...]
        qb_ref[hh] = ((y * ctq + _swap_halves(y, B_HD // 4) * stq) * qsh).astype(BF16)
        blk = pt[T_CQ + hh * C_HD:T_CQ + (hh + 1) * C_HD]
        qc_ref[hh] = ((blk * ctq + _swap_halves(blk, C_HD // 4) * stq) * qsh).astype(BF16)

    bk = ps[:, STD_BK:STD_BK + LANES]
    sq = bk * bk
    s0 = jnp.sum(jnp.where(first, sq, 0.0), axis=-1, keepdims=True)
    s1 = jnp.sum(jnp.where(first, 0.0, sq), axis=-1, keepdims=True)
    rk = lax.rsqrt(jnp.where(first, s0, s1) * (1.0 / B_HD) + EPS)
    kb = (bk * rk * gknr_ref[...]) * ck + (ps[:, STD_BKR:STD_BKR + LANES] * rk * gknrr_ref[...]) * sk
    kc = ps[:, STD_CK:STD_CK + LANES] * ck + ps[:, STD_CKR:STD_CKR + LANES] * sk
    for g in range(HB_KV):
        kb_ref[g] = kb[:, g * B_HD:(g + 1) * B_HD].astype(BF16)
        kc_ref[g] = kc[:, g * C_HD:(g + 1) * C_HD].astype(BF16)
        vb_ref[g, 0:B_HD, :] = pt[T_BV + g * B_HD:T_BV + (g + 1) * B_HD].astype(BF16)
        vb_ref[g, B_HD:V_ROWS, :] = ones
        vc_ref[g, 0:C_HD, :] = pt[T_CV + g * C_HD:T_CV + (g + 1) * C_HD].astype(BF16)
        vc_ref[g, C_HD:V_ROWS, :] = ones


def _project(src, mods, l, wts, tabs, scale_a, scale_h):
    b = src[-1].shape[0]
    s = sum(a.shape[1] for a in src)
    t = TOKEN_TILE
    n_t = s // t
    tok2 = lambda bi, ti: (ti, 0)
    feat2 = lambda bi, ti: (0, ti)
    names = ("g1", "w_std", "w_t", "w_qb", "w_k", "w_v", "g_qa_col", "g_kv_row", "g_kv_col", "g_qn_col", "g_kn_row",
             "g_kn_rot_row")
    in_specs = _token_specs(src, 0) + [_mod_spec(l, 0)] + [_layer(wts[n], l) for n in names] + [
        pl.BlockSpec((t, LANES), tok2), pl.BlockSpec((t, LANES), tok2),
        pl.BlockSpec((t, LANES), tok2), pl.BlockSpec((t, LANES), tok2),
        pl.BlockSpec((B_HD, t), feat2), pl.BlockSpec((B_HD, t), feat2),
        pl.BlockSpec((A_ROPE, t), feat2), pl.BlockSpec((A_ROPE, t), feat2),
    ]
    qspec = lambda heads, rows: pl.BlockSpec((None, heads, rows, t), lambda bi, ti: (bi, 0, 0, ti))
    kspec = lambda heads, cols: pl.BlockSpec((None, heads, t, cols), lambda bi, ti: (bi, 0, ti, 0))
    out_specs = [
        qspec(HA, HEAD_PAD), kspec(HA, HEAD_PAD), qspec(HA, V_ROWS),
        qspec(HB, B_HD), kspec(HB_KV, B_HD), qspec(HB_KV, V_ROWS),
        qspec(HC, C_HD), kspec(HC_KV, C_HD), qspec(HC_KV, V_ROWS),
    ]
    sd = jax.ShapeDtypeStruct
    out_shape = [
        sd((b, HA, HEAD_PAD, s), BF16), sd((b, HA, s, HEAD_PAD), BF16), sd((b, HA, V_ROWS, s), BF16),
        sd((b, HB, B_HD, s), BF16), sd((b, HB_KV, s, B_HD), BF16), sd((b, HB_KV, V_ROWS, s), BF16),
        sd((b, HC, C_HD, s), BF16), sd((b, HC_KV, s, C_HD), BF16), sd((b, HC_KV, V_ROWS, s), BF16),
    ]
    return pl.pallas_call(
        functools.partial(_proj_kernel, n_src=len(src), scale_a=scale_a, scale_h=scale_h),
        grid=(b, n_t),
        in_specs=in_specs,
        out_specs=out_specs,
        out_shape=out_shape,
        compiler_params=_params(2),
        name="projection",
    )(*src, mods, *(wts[n] for n in names),
      tabs["ck"], tabs["sk"], tabs["ca"], tabs["sa"], tabs["ctq"], tabs["stq"], tabs["cta"], tabs["sta"])


def _attn_kernel(*refs, n_heads, group, has_sink):
    if has_sink:
        q_ref, qn_ref, k_ref, v_ref, sink_ref, o_ref, s_scr, m_scr = refs
    else:
        q_ref, qn_ref, k_ref, v_ref, o_ref, s_scr, m_scr = refs
    n_keys, tq = s_scr.shape[1], s_scr.shape[2]
    chunk = min(KEY_CHUNK, n_keys)
    n_chunks = n_keys // chunk
    lag = min(OUTPUT_LAG, n_chunks - 1)

    def score_chunk(head, c, mrun):
        q = qn_ref[0] if head == n_heads else q_ref[head]
        rows = slice(c * chunk, (c + 1) * chunk)
        s = _dot(k_ref[(head % n_heads) // group, rows, :], q)
        s_scr[head % 2, rows, :] = s
        part = jnp.max(s.reshape(chunk // SUBLANES, SUBLANES, tq), axis=0)
        return part if mrun is None else jnp.maximum(mrun, part)

    def output_chunk(head, c, m, acc):
        rows = slice(c * chunk, (c + 1) * chunk)
        p = jnp.exp2(s_scr[head % 2, rows, :] - m).astype(BF16)
        part = _dot(v_ref[head // group, :, rows], p)
        return part if acc is None else acc + part

    @pl.when(pl.program_id(1) == 0)
    def _():
        mrun = None
        for c in range(n_chunks):
            mrun = score_chunk(0, c, mrun)
        m_scr[...] = jnp.max(mrun, axis=0, keepdims=True)

    maxes = {0: m_scr[...]}
    mrun = acc = m = snk = None
    for g in range(n_heads * n_chunks + lag):
        if g < n_heads * n_chunks:
            hs, cs = divmod(g, n_chunks)
            mrun = score_chunk(hs + 1, cs, mrun)
            if cs == n_chunks - 1:
                maxes[hs + 1] = jnp.max(mrun, axis=0, keepdims=True)
                mrun = None
        if g >= lag:
            ho, co = divmod(g - lag, n_chunks)
            if co == 0:
                m = maxes.pop(ho)
                if has_sink:
                    snk = sink_ref[ho]
                    m = jnp.maximum(m, snk)
                acc = None
            acc = output_chunk(ho, co, m, acc)
            if co == n_chunks - 1:
                denom = acc[A_V:A_V + 1]
                if has_sink:
                    denom = denom + jnp.exp2(snk - m)
                o_ref[ho * A_V:(ho + 1) * A_V, :] = (acc[0:A_V] * (1.0 / denom)).astype(BF16)
    m_scr[...] = maxes[n_heads]


def _attention(qt, k, vt, sink2, *, n_keys, tile0, n_tiles, out=None):
    b, n_heads, dk, s = qt.shape
    hk = k.shape[1]
    t = TOKEN_TILE
    in_specs = [
        pl.BlockSpec((None, n_heads, dk, t), lambda bi, ti: (bi, 0, 0, ti + tile0)),
        pl.BlockSpec((None, 1, dk, t), lambda bi, ti: (bi, 0, 0, jnp.minimum(ti + 1, n_tiles - 1) + tile0)),
        pl.BlockSpec((None, hk, n_keys, k.shape[3]), lambda bi, ti: (bi, 0, 0, 0)),
        pl.BlockSpec((None, hk, V_ROWS, n_keys), lambda bi, ti: (bi, 0, 0, 0)),
    ]
    args = [qt, qt, k, vt]
    if sink2 is not None:
        in_specs.append(_full(sink2.shape))
        args.append(sink2)
    aliases = {}
    if out is not None:
        in_specs.append(pl.BlockSpec(memory_space=pl.ANY))
        args.append(out)
        aliases = {len(args) - 1: 0}
    kern = functools.partial(_attn_kernel, n_heads=n_heads, group=n_heads // hk, has_sink=sink2 is not None)
    if out is not None:
        kern = _drop_last_input(kern, len(args))
    return pl.pallas_call(
        kern,
        grid=(b, n_tiles),
        in_specs=in_specs,
        out_specs=pl.BlockSpec((None, n_heads * A_V, t), lambda bi, ti: (bi, 0, ti + tile0)),
        out_shape=jax.ShapeDtypeStruct((b, n_heads * A_V, s), BF16),
        input_output_aliases=aliases,
        scratch_shapes=[pltpu.VMEM((2, n_keys, t), F32), pltpu.VMEM((1, t), F32)],
        compiler_params=_params(2),
        name="attention",
    )(*args)


def _drop_last_input(kern, n_in):
    def wrapped(*refs):
        return kern(*refs[:n_in - 1], *refs[n_in:])
    return wrapped


def _window_caps(n_ctx, tq):
    n_band = WINDOW + tq
    n_keys = n_ctx + n_band + WINDOW
    row = jnp.arange(n_keys, dtype=jnp.int32)[:, None]
    col = jnp.arange(tq, dtype=jnp.int32)[None, :]
    rel = row - n_ctx - WINDOW - col
    in_band = (rel >= -WINDOW) & (rel <= WINDOW)
    caps = []
    for first in (False, True):
        for last in (False, True):
            ok_lo = (row >= n_ctx + WINDOW) | (not first)
            ok_hi = (row < n_ctx + n_band) | (not last)
            valid = (row < n_ctx) | (in_band & ok_lo & ok_hi)
            caps.append(jnp.where(valid, jnp.inf, NEG).astype(F32))
    return jnp.stack(caps)


def _window_kernel(q_ref, k_ref, v_ref, sink_ref, cap_ref, o_ref, k_scr, v_scr, s_scr, *, n_ctx, n_tiles):
    tq = q_ref.shape[2]
    ti = pl.program_id(1)
    start = pl.multiple_of(n_ctx + ti * tq, LANES)
    lo = pl.multiple_of(start - WINDOW, LANES)
    nxt = pl.multiple_of(jnp.minimum(start + tq, n_ctx + (n_tiles - 1) * tq + tq - WINDOW), LANES)
    n_band = WINDOW + tq
    n_keys = n_ctx + n_band + WINDOW

    for g in range(HC_KV):
        k_scr[g, 0:n_ctx, :] = k_ref[g, 0:n_ctx, :]
        k_scr[g, n_ctx:n_ctx + n_band, :] = k_ref[g, pl.ds(lo, n_band), :]
        k_scr[g, n_ctx + n_band:n_keys, :] = k_ref[g, pl.ds(nxt, WINDOW), :]
        v_scr[g, :, 0:n_ctx] = v_ref[g, :, 0:n_ctx]
        v_scr[g, :, n_ctx:n_ctx + n_band] = v_ref[g, :, pl.ds(lo, n_band)]
        v_scr[g, :, n_ctx + n_band:n_keys] = v_ref[g, :, pl.ds(nxt, WINDOW)]

    group = HC // HC_KV
    chunk = KEY_CHUNK
    n_chunks = n_keys // chunk

    def score_chunk(hh, c, mrun):
        rows = slice(c * chunk, (c + 1) * chunk)
        s = jnp.minimum(_dot(k_scr[hh // group, rows, :], q_ref[hh]), cap_ref[rows, :])
        s_scr[hh % 2, rows, :] = s
        part = jnp.max(s.reshape(chunk // SUBLANES, SUBLANES, tq), axis=0)
        return part if mrun is None else jnp.maximum(mrun, part)

    def output_chunk(hh, c, m, acc):
        rows = slice(c * chunk, (c + 1) * chunk)
        p = jnp.exp2(s_scr[hh % 2, rows, :] - m).astype(BF16)
        part = _dot(v_scr[hh // group, :, rows], p)
        return part if acc is None else acc + part

    lag = n_chunks + WINDOW_LAG
    maxes = {}
    mrun = acc = m = snk = None
    for g in range(HC * n_chunks + lag):
        if g < HC * n_chunks:
            hs, cs = divmod(g, n_chunks)
            mrun = score_chunk(hs, cs, mrun)
            if cs == n_chunks - 1:
                maxes[hs] = jnp.max(mrun, axis=0, keepdims=True)
                mrun = None
        if g >= lag:
            ho, co = divmod(g - lag, n_chunks)
            if co == 0:
                snk = sink_ref[ho]
                m = jnp.maximum(maxes.pop(ho), snk)
                acc = None
            acc = output_chunk(ho, co, m, acc)
            if co == n_chunks - 1:
                denom = acc[C_HD:C_HD + 1] + jnp.exp2(snk - m)
                o_ref[ho * C_HD:(ho + 1) * C_HD, :] = (acc[0:C_HD] * (1.0 / denom)).astype(BF16)


def _window_attention(qt, k, vt, sink2, caps, *, n_ctx):
    b, n_heads, dk, s = qt.shape
    hk = k.shape[1]
    t = TOKEN_TILE
    tile0 = n_ctx // t
    n_tiles = (s - n_ctx) // t
    n_win_keys = n_ctx + WINDOW + t + WINDOW
    return pl.pallas_call(
        functools.partial(_window_kernel, n_ctx=n_ctx, n_tiles=n_tiles),
        grid=(b, n_tiles),
        in_specs=[
            pl.BlockSpec((None, n_heads, dk, t), lambda bi, ti: (bi, 0, 0, ti + tile0)),
            pl.BlockSpec((None, hk, s, dk), lambda bi, ti: (bi, 0, 0, 0)),
            pl.BlockSpec((None, hk, V_ROWS, s), lambda bi, ti: (bi, 0, 0, 0)),
            _full(sink2.shape),
            pl.BlockSpec((None, n_win_keys, t),
                         lambda bi, ti: (2 * (ti == 0).astype(jnp.int32) + (ti == n_tiles - 1).astype(jnp.int32), 0, 0)),
        ],
        out_specs=pl.BlockSpec((None, n_heads * C_HD, t), lambda bi, ti: (bi, 0, ti + tile0)),
        out_shape=jax.ShapeDtypeStruct((b, n_heads * C_HD, s), BF16),
        scratch_shapes=[
            pltpu.VMEM((hk, n_win_keys, dk), BF16), pltpu.VMEM((hk, V_ROWS, n_win_keys), BF16),
            pltpu.VMEM((2, n_win_keys, t), F32),
        ],
        compiler_params=_params(2),
        name="window_attention",
    )(qt, k, vt, sink2, caps)


def _merge_kernel(*refs, n_src):
    mod_ref, g1_ref, oa_ref, ob_ref, oc_ref, wg_ref, bg_ref, wbr_ref, wout_ref, xo_ref = refs[n_src:]
    x = _token_tile(refs[:n_src])
    h = _modulated_norm(x, g1_ref[...], mod_ref[0:1, :], mod_ref[1:2, :])
    hb = h.astype(BF16)
    y = None
    for i, o_ref in enumerate((oa_ref, ob_ref, oc_ref)):
        cols = slice(i * D_MODEL, (i + 1) * D_MODEL)
        gate = _sigmoid(_dot(hb, wg_ref[:, cols]) + bg_ref[:, cols])
        term = gate * _dot_tn(o_ref[...], wbr_ref[i])
        y = term if y is None else y + term
    z = _dot(y.astype(BF16), wout_ref[...])
    xo_ref[...] = x + mod_ref[2:3, :] * z


def _merge(src, mods, l, wts, oa, ob, oc, *, tile0, n_tiles):
    b = src[-1].shape[0]
    s = sum(a.shape[1] for a in src)
    t = TOKEN_TILE
    feat = lambda bi, ti: (bi, 0, ti + tile0)
    names = ("w_gate", "b_gate", "w_branch", "w_out")
    return pl.pallas_call(
        functools.partial(_merge_kernel, n_src=len(src)),
        grid=(b, n_tiles),
        in_specs=_token_specs(src, tile0) + [_mod_spec(l, tile0), _layer(wts["g1"], l)] + [
            pl.BlockSpec((None, BRANCH_W, t), feat),
            pl.BlockSpec((None, BRANCH_W, t), feat),
            pl.BlockSpec((None, BRANCH_W, t), feat),
        ] + [_layer(wts[n], l) for n in names],
        out_specs=pl.BlockSpec((None, t, D_MODEL), lambda bi, ti: (bi, ti + tile0, 0)),
        out_shape=jax.ShapeDtypeStruct((b, s, D_MODEL), F32),
        input_output_aliases={0: 0} if len(src) == 1 else {},
        compiler_params=_params(2),
        name="merge",
    )(*src, mods, wts["g1"], oa, ob, oc, *(wts[n] for n in names))


def _ffn_kernel(xp_ref, x_ref, xn_ref, mod_ref, g2_ref, wup_ref, wconv_ref, bconv_ref, wdown_ref, gf_ref,
                o_ref, act_ref, *, tile0, n_seq_tiles, final_norm):
    t = x_ref.shape[0]
    halo = SUBLANES
    ta = pl.program_id(1) + tile0
    x = x_ref[...]
    xa = jnp.concatenate([xp_ref[...], x, xn_ref[...]], axis=0)
    h = _modulated_norm(xa, g2_ref[...], mod_ref[3:4, :], mod_ref[4:5, :])
    row = lax.broadcasted_iota(jnp.int32, (t + 2 * halo, 1), 0)
    keep_prev = (ta > 1).astype(F32)
    keep_next = ((ta > 0) & (ta < n_seq_tiles - 1)).astype(F32)
    keep = jnp.where(row < halo, keep_prev, jnp.where(row >= t + halo, keep_next, 1.0))
    hb = (h * keep).astype(BF16)

    def conv(u, cols):
        return (u[halo - 1:halo - 1 + t] * wconv_ref[0:1, cols] + u[halo:halo + t] * wconv_ref[1:2, cols]
                + u[halo + 1:halo + 1 + t] * wconv_ref[2:3, cols] + bconv_ref[:, cols])

    for c in range(D_FF // FF_CHUNK):
        ca = slice(c * FF_CHUNK, (c + 1) * FF_CHUNK)
        cg = slice(D_FF + c * FF_CHUNK, D_FF + (c + 1) * FF_CHUNK)
        a = conv(_dot(hb, wup_ref[:, ca]), ca)
        gv = conv(_dot(hb, wup_ref[:, cg]), cg)
        act_ref[:, ca] = (a * _sigmoid(a) * gv).astype(BF16)
    y = x + mod_ref[5:6, :] * _dot(act_ref[...], wdown_ref[...])
    if final_norm:
        y = y * lax.rsqrt(jnp.mean(y * y, axis=-1, keepdims=True) + EPS) * gf_ref[...]
    o_ref[...] = y


def _conv_ffn(xc, mods, l, wts, g_final, *, tile0, n_tiles, final_norm):
    b, s, _ = xc.shape
    t = TOKEN_TILE
    n_seq_tiles = s // t
    per = t // SUBLANES
    last_blk = s // SUBLANES - 1
    out_rows = n_tiles * t if final_norm else s
    out_tile0 = 0 if final_norm else tile0
    return pl.pallas_call(
        functools.partial(_ffn_kernel, tile0=tile0, n_seq_tiles=n_seq_tiles, final_norm=final_norm),
        grid=(b, n_tiles),
        in_specs=[
            pl.BlockSpec((None, SUBLANES, D_MODEL), lambda bi, ti: (bi, jnp.maximum((ti + tile0) * per - 1, 0), 0)),
            pl.BlockSpec((None, t, D_MODEL), lambda bi, ti: (bi, ti + tile0, 0)),
            pl.BlockSpec((None, SUBLANES, D_MODEL),
                         lambda bi, ti: (bi, jnp.minimum((ti + tile0 + 1) * per, last_blk), 0)),
            _mod_spec(l, tile0),
            _layer(wts["g2"], l), _layer(wts["w_up"], l), _layer(wts["w_conv"], l), _layer(wts["b_conv"], l),
            _layer(wts["w_down"], l),
            _full((1, D_MODEL)),
        ],
        out_specs=pl.BlockSpec((None, t, D_MODEL), lambda bi, ti: (bi, ti + out_tile0, 0)),
        out_shape=jax.ShapeDtypeStruct((b, out_rows, D_MODEL), F32),
        scratch_shapes=[pltpu.VMEM((t, D_FF), BF16)],
        compiler_params=_params(2),
        name="conv_ffn",
    )(xc, xc, xc, mods, wts["g2"], wts["w_up"], wts["w_conv"], wts["b_conv"], wts["w_down"], g_final)


def _partner(head_dim):
    nf = head_dim // 4
    d = np.arange(head_dim)
    a, half, f = d // (2 * nf), (d % (2 * nf)) // nf, d % nf
    return a * 2 * nf + (1 - half) * nf + f, np.where(half == 0, -1.0, 1.0).astype(np.float32)


def _rope_tables(n_ctx, n_lat):
    def full(head_dim):
        nf = head_dim // 4
        _, sign = _partner(head_dim)
        rows = n_lat // GRID_W
        row = jnp.repeat(jnp.arange(rows, dtype=F32), GRID_W)
        col = jnp.tile(jnp.arange(GRID_W, dtype=F32), rows)
        inv = ROPE_THETA ** (-jnp.arange(nf, dtype=F32) / nf)
        ang = jnp.stack([row[:, None] * inv, col[:, None] * inv], axis=1)
        c = jnp.broadcast_to(jnp.cos(ang)[:, :, None, :], (n_lat, 2, 2, nf)).reshape(n_lat, head_dim)
        s = jnp.broadcast_to(jnp.sin(ang)[:, :, None, :], (n_lat, 2, 2, nf)).reshape(n_lat, head_dim) * sign
        c = jnp.concatenate([jnp.ones((n_ctx, head_dim), F32), c], axis=0)
        s = jnp.concatenate([jnp.zeros((n_ctx, head_dim), F32), s], axis=0)
        return c, s

    c64, s64 = full(B_HD)
    c32, s32 = full(A_ROPE)
    pad_a = lambda v: jnp.pad(v, ((0, 0), (A_NOPE, LANES - A_NOPE - A_ROPE)))
    return {
        "ck": jnp.tile(c64, (1, LANES // B_HD)), "sk": jnp.tile(s64, (1, LANES // B_HD)),
        "ca": pad_a(c32), "sa": pad_a(s32),
        "ctq": c64.T, "stq": s64.T, "cta": c32.T, "sta": s32.T,
    }


def _rot_ranges(base, head_dim, n_heads):
    nf = head_dim // 4
    out = []
    for hh in range(n_heads):
        b0 = base + hh * head_dim
        for a in range(2):
            out += [(b0 + a * 2 * nf + nf, b0 + a * 2 * nf + 2 * nf), (b0 + a * 2 * nf, b0 + a * 2 * nf + nf)]
    return out


def _prepare_weights(w_in, b_gate, g_norm1, g_q_a, w_q_b, g_kv_a, w_kv_b, g_qn, g_kn, w_branch, w_out, g_norm2,
                     w_up, w_conv, b_conv, w_down):
    depth = w_in.shape[0]
    cuts = [int(v) for v in np.cumsum((0,) + IN_SIZES)]
    aq, akv, akr, bq, bk, bv, cq, ck, cv, gl = ((cuts[i], cuts[i + 1]) for i in range(len(IN_SIZES)))
    cols = lambda ranges: [w_in[:, :, lo:hi] for lo, hi in ranges]
    zeros = lambda n: [jnp.zeros((depth, D_MODEL, n), F32)]
    pad_a = lambda pieces: zeros(A_NOPE) + pieces + zeros(LANES - A_NOPE - A_ROPE)
    w_std = jnp.concatenate(
        cols([akv, bk]) + cols(_rot_ranges(bk[0], B_HD, HB_KV)) + cols([ck]) + cols(_rot_ranges(ck[0], C_HD, HC_KV))
        + pad_a(cols([akr])) + pad_a(cols(_rot_ranges(akr[0], A_ROPE, 1))), axis=2)
    w_t = jnp.swapaxes(jnp.concatenate(cols([aq, akv, bq, cq, bv, cv]), axis=2), 1, 2)
    wqb = w_q_b.reshape(depth, A_Q_RANK, HA, A_NOPE + A_ROPE)
    wqb = jnp.pad(wqb, ((0, 0), (0, 0), (0, 0), (0, HEAD_PAD - A_NOPE - A_ROPE)))
    wqb = jnp.swapaxes(wqb.reshape(depth, A_Q_RANK, HA * HEAD_PAD), 1, 2)
    wkv = w_kv_b.reshape(depth, A_KV_RANK, HA, A_NOPE + A_V)
    w_k = jnp.pad(wkv[..., :A_NOPE], ((0, 0), (0, 0), (0, 0), (0, HEAD_PAD - A_NOPE)))
    w_k = w_k.reshape(depth, A_KV_RANK, HA * HEAD_PAD)
    w_v = jnp.swapaxes(wkv[..., A_NOPE:].reshape(depth, A_KV_RANK, HA * A_V), 1, 2)
    g_kn2 = jnp.tile(g_kn, (1, LANES // B_HD))
    p64, _ = _partner(B_HD)
    p128 = np.concatenate([p64 + i * B_HD for i in range(LANES // B_HD)])
    return {
        "g1": g_norm1.reshape(depth, 1, D_MODEL), "g2": g_norm2.reshape(depth, 1, D_MODEL),
        "w_std": w_std.astype(BF16), "w_t": w_t.astype(BF16), "w_qb": wqb.astype(BF16),
        "w_k": w_k.astype(BF16), "w_v": w_v.astype(BF16),
        "g_qa_col": g_q_a.reshape(depth, A_Q_RANK, 1), "g_kv_row": g_kv_a.reshape(depth, 1, A_KV_RANK),
        "g_kv_col": g_kv_a.reshape(depth, A_KV_RANK, 1), "g_qn_col": g_qn.reshape(depth, B_HD, 1),
        "g_kn_row": g_kn2.reshape(depth, 1, LANES), "g_kn_rot_row": g_kn2[:, p128].reshape(depth, 1, LANES),
        "w_gate": w_in[:, :, gl[0]:gl[1]].astype(BF16), "b_gate": b_gate.reshape(depth, 1, N_BRANCH * D_MODEL),
        "w_branch": w_branch.astype(BF16), "w_out": w_out.astype(BF16),
        "w_up": w_up.astype(BF16), "w_conv": w_conv, "b_conv": b_conv.reshape(depth, 1, 2 * D_FF),
        "w_down": w_down.astype(BF16),
    }


def kernel(x, c, ctx, c_ctx, w_mod, b_mod, g_norm1, w_in, b_gate, g_q_a, w_q_b, g_kv_a, w_kv_b, g_qn, g_kn, sink,
           w_branch, w_out, g_norm2, w_up, w_conv, b_conv, w_down, g_final):
    b, n_lat, d = x.shape
    n_ctx = ctx.shape[1]
    depth = w_mod.shape[0]
    t = TOKEN_TILE
    assert d == D_MODEL and n_ctx == t and n_lat % t == 0 and n_lat % GRID_W == 0 and b <= MOD_ROWS // 2
    assert depth >= 2
    s = n_ctx + n_lat
    n_t = s // t
    scale_a = 1.0 / math.sqrt(A_NOPE + A_ROPE)
    scale_h = 1.0 / math.sqrt(B_HD)

    cvec = jnp.zeros((MOD_ROWS, D_MODEL), F32).at[:b].set(c).at[MOD_ROWS // 2].set(c_ctx)
    mods = _modulation(cvec, w_mod, b_mod)
    tabs = _rope_tables(n_ctx, n_lat)
    wts = _prepare_weights(w_in, b_gate, g_norm1, g_q_a, w_q_b, g_kv_a, w_kv_b, g_qn, g_kn, w_branch, w_out, g_norm2,
                           w_up, w_conv, b_conv, w_down)
    gf = g_final.reshape(1, D_MODEL)
    caps = _window_caps(n_ctx, t)
    sinks =jnp.broadcast_to((sink * LOG2E).reshape(depth, HC, 1, 1), (depth, HC, 1, t)).astype(F32)

    src = (ctx, x)
    for l in range(depth):
        last = l == depth - 1
        qa, ka, va, qb, kb, vb, qc, kc, vc = _project(src, mods, l, wts, tabs, scale_a, scale_h)
        lat = dict(n_keys=s, tile0=1, n_tiles=n_t - 1)
        oa = _attention(qa, ka, va, None, **lat)
        ob = _attention(qb, kb, vb, None, **lat)
        oc = _window_attention(qc, kc, vc, sinks[l], caps, n_ctx=n_ctx)
        if not last:
            cx = dict(n_keys=n_ctx, tile0=0, n_tiles=1)
            oa = _attention(qa, ka, va, None, out=oa, **cx)
            ob = _attention(qb, kb, vb, None, out=ob, **cx)
            oc = _attention(qc, kc, vc, sinks[l], out=oc, **cx)
        tiles = dict(tile0=1, n_tiles=n_t - 1) if last else dict(tile0=0, n_tiles=n_t)
        xc = _merge(src, mods, l, wts, oa, ob, oc, **tiles)
        src = (_conv_ffn(xc, mods, l, wts, gf, final_norm=last, **tiles),)
    return src[0]
```

```python
import functools
import math

import jax
import jax.numpy as jnp
import numpy as np
from jax import lax
from jax.experimental import pallas as pl
from jax.experimental.pallas import tpu as pltpu

F32 = jnp.float32
BF16 = jnp.bfloat16

D_MODEL = 1024
GRID_W = 64
EPS = 1e-6
ROPE_THETA = 10000.0
WINDOW = 128
NEG = -1e30
HA, A_NOPE, A_ROPE, A_V, A_Q_RANK, A_KV_RANK = 8, 64, 32, 64, 256, 128
HB, HB_KV, B_HD = 8, 2, 64
HC, HC_KV, C_HD = 8, 2, 64
BRANCH_W = 512
N_BRANCH = 3
D_FF = 2816
CONV_W = 3
N_MOD = 6
IN_SIZES = (A_Q_RANK, A_KV_RANK, A_ROPE, HB * B_HD, HB_KV * B_HD, HB_KV * B_HD, HC * C_HD, HC_KV * C_HD,
            HC_KV * C_HD, N_BRANCH * D_MODEL)

LANES = 128
SUBLANES = 8
BF16_ROWS = 16
TOKEN_TILE = 256
MOD_ROWS = 16
HEAD_PAD = 128
V_ROWS = A_V + BF16_ROWS
FF_CHUNK = 256
KEY_CHUNK = 256
OUTPUT_LAG = 3
WINDOW_LAG = 1
VMEM_LIMIT = 56 * 1024 * 1024
LOG2E = 1.4426950408889634


def _dot(a, b):
    return jnp.dot(a, b, preferred_element_type=F32)


def _dot_nt(a, b):
    return lax.dot_general(a, b, (((1,), (1,)), ((), ())), preferred_element_type=F32)


def _dot_tn(a, b):
    return lax.dot_general(a, b, (((0,), (0,)), ((), ())), preferred_element_type=F32)


def _sigmoid(x):
    return 1.0 / (1.0 + jnp.exp(-x))


def _modulated_norm(x, g, shift, scale):
    ms = jnp.mean(x * x, axis=-1, keepdims=True)
    return (x * lax.rsqrt(ms + EPS) * g) * (1.0 + scale) + shift


def _params(n_grid):
    return pltpu.CompilerParams(dimension_semantics=("arbitrary",) * n_grid, vmem_limit_bytes=VMEM_LIMIT)


def _full(shape):
    nd = len(shape)
    return pl.BlockSpec(shape, lambda *_: (0,) * nd)


def _layer(arr, l):
    nd = arr.ndim - 1
    return pl.BlockSpec((None,) + arr.shape[1:], lambda *_: (l,) + (0,) * nd)


def _mod_spec(l, tile0):
    return pl.BlockSpec((None, None, N_MOD, D_MODEL),
                        lambda bi, ti: (l, jnp.where(ti + tile0 == 0, MOD_ROWS // 2, bi), 0, 0))


def _token_specs(src, tile0):
    t = TOKEN_TILE
    if len(src) == 1:
        return [pl.BlockSpec((None, t, D_MODEL), lambda bi, ti: (bi, ti + tile0, 0))]
    assert tile0 == 0
    return [pl.BlockSpec((None, t, D_MODEL), lambda bi, ti: (bi, 0, 0)),
            pl.BlockSpec((None, t, D_MODEL), lambda bi, ti: (bi, jnp.maximum(ti - 1, 0), 0))]


def _token_tile(refs, is_ctx):
    if len(refs) == 1:
        return refs[0][...]
    return jnp.where(is_ctx, refs[0][...], refs[1][...])


def _mod_kernel(c_ref, w_ref, b_ref, o_ref):
    c = c_ref[...]
    a = c * _sigmoid(c)
    o_ref[...] = jnp.dot(a, w_ref[...], precision=lax.Precision.HIGHEST, preferred_element_type=F32) + b_ref[...]


def _modulation(cvec, w_mod, b_mod):
    depth = w_mod.shape[0]
    n_col = N_MOD * D_MODEL
    out = pl.pallas_call(
        _mod_kernel,
        grid=(depth, N_MOD),
        in_specs=[
            pl.BlockSpec((MOD_ROWS, D_MODEL), lambda l, j: (0, 0)),
            pl.BlockSpec((None, D_MODEL, D_MODEL), lambda l, j: (l, 0, j)),
            pl.BlockSpec((None, 1, D_MODEL), lambda l, j: (l, 0, j)),
        ],
        out_specs=pl.BlockSpec((None, MOD_ROWS, D_MODEL), lambda l, j: (l, 0, j)),
        out_shape=jax.ShapeDtypeStruct((depth, MOD_ROWS, n_col), F32),
        compiler_params=_params(2),
        name="modulation",
    )(cvec, w_mod, b_mod.reshape(depth, 1, n_col))
    return out.reshape(depth, MOD_ROWS, N_MOD, D_MODEL)


STD_AKV, STD_BK, STD_BKR, STD_CK, STD_CKR, STD_KR, STD_KRR = (i * LANES for i in range(7))
STD_COLS = 7 * LANES
T_AQ = 0
T_AKV = T_AQ + A_Q_RANK
T_BQ = T_AKV + A_KV_RANK
T_CQ = T_BQ + HB * B_HD
T_BV = T_CQ + HC * C_HD
T_CV = T_BV + HB_KV * B_HD
T_ROWS = T_CV + HC_KV * C_HD


def _swap_halves(x, nf):
    parts = []
    for a in range(2):
        base = a * 2 * nf
        parts += [x[base + nf:base + 2 * nf], x[base:base + nf]]
    return jnp.concatenate(parts, axis=0)


def _ones_rows(t):
    row = lax.broadcasted_iota(jnp.int32, (BF16_ROWS, t), 0)
    return jnp.where(row == 0, 1.0, 0.0).astype(BF16)


def _proj_kernel(*refs, n_src, n_tiles, n_flat, scale_a, scale_h):
    (mod_ref, g1_ref, wstd_ref, wt_ref, wqb_ref, wk_ref, wv_ref,
     gqa_ref, gkvr_ref, gkvc_ref, gqn_ref, gknr_ref, gknrr_ref,
     ck_ref, sk_ref, ca_ref, sa_ref, ctq_ref, stq_ref, cta_ref, sta_ref,
     qa_ref, ka_ref, va_ref, qb_ref, kb_ref, vb_ref, qc_ref, kc_ref, vc_ref, ps_scr, pt_scr) = refs[n_src:]
    t = TOKEN_TILE
    j = pl.program_id(0)

    @pl.when(j == 0)
    def _():
        ps_scr[1] = jnp.zeros(ps_scr.shape[1:], F32)
        pt_scr[1] = jnp.zeros(pt_scr.shape[1:], F32)

    def stage1(slot):
        is_ctx = jnp.minimum(j, n_flat - 1) % n_tiles == 0
        h = _modulated_norm(_token_tile(refs[:n_src], is_ctx), g1_ref[...], mod_ref[0:1, :], mod_ref[1:2, :])
        hb = h.astype(BF16)
        ps_scr[slot] = _dot(hb, wstd_ref[...])
        pt_scr[slot] = _dot_nt(wt_ref[...], hb)

    def stage2(slot):
        ps, pt = ps_scr.at[slot], pt_scr.at[slot]
        ones = _ones_rows(t)

        aq = pt[T_AQ:T_AQ + A_Q_RANK]
        aqn = aq * lax.rsqrt(jnp.mean(aq * aq, axis=0, keepdims=True) + EPS) * gqa_ref[...]
        qt = _dot(wqb_ref[...], aqn.astype(BF16))
        cta, sta = cta_ref[...], sta_ref[...]
        qs = scale_a * LOG2E
        for hh in range(HA):
            base = hh * HEAD_PAD
            qa_ref[hh, 0:A_NOPE, :] = (qt[base:base + A_NOPE] * qs).astype(BF16)
            r = qt[base + A_NOPE:base + A_NOPE + A_ROPE]
            rr = r * cta + _swap_halves(r, A_ROPE // 4) * sta
            qa_ref[hh, A_NOPE:A_NOPE + A_ROPE, :] = (rr * qs).astype(BF16)
            qa_ref[hh, A_NOPE + A_ROPE:HEAD_PAD, :] = jnp.zeros((HEAD_PAD - A_NOPE - A_ROPE, t), BF16)

        akv = ps[:, STD_AKV:STD_AKV + LANES]
        akvn = akv * lax.rsqrt(jnp.mean(akv * akv, axis=-1, keepdims=True) + EPS) * gkvr_ref[...]
        kn = _dot(akvn.astype(BF16), wk_ref[...])
        kr = ps[:, STD_KR:STD_KR + LANES] * ca_ref[...] + ps[:, STD_KRR:STD_KRR + LANES] * sa_ref[...]
        for hh in range(HA):
            ka_ref[hh] = (kn[:, hh * HEAD_PAD:(hh + 1) * HEAD_PAD] + kr).astype(BF16)

        akvt = pt[T_AKV:T_AKV + A_KV_RANK]
        akvtn = akvt * lax.rsqrt(jnp.mean(akvt * akvt, axis=0, keepdims=True) + EPS) * gkvc_ref[...]
        vt = _dot(wv_ref[...], akvtn.astype(BF16))
        for hh in range(HA):
            va_ref[hh, 0:A_V, :] = vt[hh * A_V:(hh + 1) * A_V].astype(BF16)
            va_ref[hh, A_V:V_ROWS, :] = ones

        ctq, stq = ctq_ref[...], stq_ref[...]
        ck, sk = ck_ref[...], sk_ref[...]
        qsh = scale_h * LOG2E
        lane = lax.broadcasted_iota(jnp.int32, (t, LANES), 1)
        first = lane < B_HD

        for hh in range(HB):
            blk = pt[T_BQ + hh * B_HD:T_BQ + (hh + 1) * B_HD]
            y = blk * lax.rsqrt(jnp.mean(blk * blk, axis=0, keepdims=True) + EPS) * gqn_ref[...]
            qb_ref[hh] = ((y * ctq + _swap_halves(y, B_HD // 4) * stq) * qsh).astype(BF16)
            blk = pt[T_CQ + hh * C_HD:T_CQ + (hh + 1) * C_HD]
            qc_ref[hh] = ((blk * ctq + _swap_halves(blk, C_HD // 4) * stq) * qsh).astype(BF16)

        bk = ps[:, STD_BK:STD_BK + LANES]
        sq = bk * bk
        s0 = jnp.sum(jnp.where(first, sq, 0.0), axis=-1, keepdims=True)
        s1 = jnp.sum(jnp.where(first, 0.0, sq), axis=-1, keepdims=True)
        rk = lax.rsqrt(jnp.where(first, s0, s1) * (1.0 / B_HD) + EPS)
        kb = (bk * rk * gknr_ref[...]) * ck + (ps[:, STD_BKR:STD_BKR + LANES] * rk * gknrr_ref[...]) * sk
        kc = ps[:, STD_CK:STD_CK + LANES] * ck + ps[:, STD_CKR:STD_CKR + LANES] * sk
        for g in range(HB_KV):
            kb_ref[g] = kb[:, g * B_HD:(g + 1) * B_HD].astype(BF16)
            kc_ref[g] = kc[:, g * C_HD:(g + 1) * C_HD].astype(BF16)
            vb_ref[g, 0:B_HD, :] = pt[T_BV + g * B_HD:T_BV + (g + 1) * B_HD].astype(BF16)
            vb_ref[g, B_HD:V_ROWS, :] = ones
            vc_ref[g, 0:C_HD, :] = pt[T_CV + g * C_HD:T_CV + (g + 1) * C_HD].astype(BF16)
            vc_ref[g, C_HD:V_ROWS, :] = ones

    @pl.when(j % 2 == 0)
    def _():
        stage1(0)
        stage2(1)

    @pl.when(j % 2 == 1)
    def _():
        stage1(1)
        stage2(0)


def _project(src, mods, l, wts, tabs, scale_a, scale_h):
    b = src[-1].shape[0]
    s = sum(a.shape[1] for a in src)
    t = TOKEN_TILE
    n_t = s // t
    n_flat = b * n_t
    tile1 = lambda j: jnp.minimum(j, n_flat - 1)
    tile2 = lambda j: jnp.maximum(j - 1, 0)
    if len(src) == 1:
        tok_specs = [pl.BlockSpec((None, t, D_MODEL), lambda j: (tile1(j) // n_t, tile1(j) % n_t, 0))]
    else:
        tok_specs = [pl.BlockSpec((None, t, D_MODEL), lambda j: (tile1(j) // n_t, 0, 0)),
                     pl.BlockSpec((None, t, D_MODEL),
                                  lambda j: (tile1(j) // n_t, jnp.maximum(tile1(j) % n_t - 1, 0), 0))]
    mod_spec = pl.BlockSpec(
        (None, None, N_MOD, D_MODEL),
        lambda j: (l, jnp.where(tile1(j) % n_t == 0, MOD_ROWS // 2, tile1(j) // n_t), 0, 0))
    tok2 = lambda j: (tile2(j) % n_t, 0)
    feat2 = lambda j: (0, tile2(j) % n_t)
    names = ("g1", "w_std", "w_t", "w_qb", "w_k", "w_v", "g_qa_col", "g_kv_row", "g_kv_col", "g_qn_col", "g_kn_row",
             "g_kn_rot_row")
    in_specs = tok_specs + [mod_spec] + [_layer(wts[n], l) for n in names] + [
        pl.BlockSpec((t, LANES), tok2), pl.BlockSpec((t, LANES), tok2),
        pl.BlockSpec((t, LANES), tok2), pl.BlockSpec((t, LANES), tok2),
        pl.BlockSpec((B_HD, t), feat2), pl.BlockSpec((B_HD, t), feat2),
        pl.BlockSpec((A_ROPE, t), feat2), pl.BlockSpec((A_ROPE, t), feat2),
    ]
    qspec = lambda heads, rows: pl.BlockSpec((None, heads, rows, t),
                                             lambda j: (tile2(j) // n_t, 0, 0, tile2(j) % n_t))
    kspec = lambda heads, cols: pl.BlockSpec((None, heads, t, cols),
                                             lambda j: (tile2(j) // n_t, 0, tile2(j) % n_t, 0))
    out_specs = [
        qspec(HA, HEAD_PAD), kspec(HA, HEAD_PAD), qspec(HA, V_ROWS),
        qspec(HB, B_HD), kspec(HB_KV, B_HD), qspec(HB_KV, V_ROWS),
        qspec(HC, C_HD), kspec(HC_KV, C_HD), qspec(HC_KV, V_ROWS),
    ]
    sd = jax.ShapeDtypeStruct
    out_shape = [
        sd((b, HA, HEAD_PAD, s), BF16), sd((b, HA, s, HEAD_PAD), BF16), sd((b, HA, V_ROWS, s), BF16),
        sd((b, HB, B_HD, s), BF16), sd((b, HB_KV, s, B_HD), BF16), sd((b, HB_KV, V_ROWS, s), BF16),
        sd((b, HC, C_HD, s), BF16), sd((b, HC_KV, s, C_HD), BF16), sd((b, HC_KV, V_ROWS, s), BF16),
    ]
    return pl.pallas_call(
        functools.partial(_proj_kernel, n_src=len(src), n_tiles=n_t, n_flat=n_flat, scale_a=scale_a,
                          scale_h=scale_h),
        grid=(n_flat + 1,),
        in_specs=in_specs,
        out_specs=out_specs,
        out_shape=out_shape,
        scratch_shapes=[pltpu.VMEM((2, t, STD_COLS), F32), pltpu.VMEM((2, T_ROWS, t), F32)],
        compiler_params=_params(1),
        name="projection",
    )(*src, mods, *(wts[n] for n in names),
      tabs["ck"], tabs["sk"], tabs["ca"], tabs["sa"], tabs["ctq"], tabs["stq"], tabs["cta"], tabs["sta"])


def _attn_kernel(*refs, n_heads, group, has_sink):
    if has_sink:
        q_ref, qn_ref, k_ref, v_ref, sink_ref, o_ref, s_scr, m_scr = refs
    else:
        q_ref, qn_ref, k_ref, v_ref, o_ref, s_scr, m_scr = refs
    n_keys, tq = s_scr.shape[1], s_scr.shape[2]
    chunk = min(KEY_CHUNK, n_keys)
    n_chunks = n_keys // chunk
    lag = min(OUTPUT_LAG, n_chunks - 1)

    def score_chunk(head, c, mrun):
        q = qn_ref[0] if head == n_heads else q_ref[head]
        rows = slice(c * chunk, (c + 1) * chunk)
        s = _dot(k_ref[(head % n_heads) // group, rows, :], q)
        s_scr[head % 2, rows, :] = s
        part = jnp.max(s.reshape(chunk // SUBLANES, SUBLANES, tq), axis=0)
        return part if mrun is None else jnp.maximum(mrun, part)

    def output_chunk(head, c, m, acc):
        rows = slice(c * chunk, (c + 1) * chunk)
        p = jnp.exp2(s_scr[head % 2, rows, :] - m).astype(BF16)
        part = _dot(v_ref[head // group, :, rows], p)
        return part if acc is None else acc + part

    @pl.when(pl.program_id(1) == 0)
    def _():
        mrun = None
        for c in range(n_chunks):
            mrun = score_chunk(0, c, mrun)
        m_scr[...] = jnp.max(mrun, axis=0, keepdims=True)

    maxes = {0: m_scr[...]}
    mrun = acc = m = snk = None
    for g in range(n_heads * n_chunks + lag):
        if g < n_heads * n_chunks:
            hs, cs = divmod(g, n_chunks)
            mrun = score_chunk(hs + 1, cs, mrun)
            if cs == n_chunks - 1:
                maxes[hs + 1] = jnp.max(mrun, axis=0, keepdims=True)
                mrun = None
        if g >= lag:
            ho, co = divmod(g - lag, n_chunks)
            if co == 0:
                m = maxes.pop(ho)
                if has_sink:
                    snk = sink_ref[ho]
                    m = jnp.maximum(m, snk)
                acc = None
            acc = output_chunk(ho, co, m, acc)
            if co == n_chunks - 1:
                denom = acc[A_V:A_V + 1]
                if has_sink:
                    denom = denom + jnp.exp2(snk - m)
                o_ref[ho * A_V:(ho + 1) * A_V, :] = (acc[0:A_V] * (1.0 / denom)).astype(BF16)
    m_scr[...] = maxes[n_heads]


def _attention(qt, k, vt, sink2, *, n_keys, tile0, n_tiles, out=None):
    b, n_heads, dk, s = qt.shape
    hk = k.shape[1]
    t = TOKEN_TILE
    in_specs = [
        pl.BlockSpec((None, n_heads, dk, t), lambda bi, ti: (bi, 0, 0, ti + tile0)),
        pl.BlockSpec((None, 1, dk, t), lambda bi, ti: (bi, 0, 0, jnp.minimum(ti + 1, n_tiles - 1) + tile0)),
        pl.BlockSpec((None, hk, n_keys, k.shape[3]), lambda bi, ti: (bi, 0, 0, 0)),
        pl.BlockSpec((None, hk, V_ROWS, n_keys), lambda bi, ti: (bi, 0, 0, 0)),
    ]
    args = [qt, qt, k, vt]
    if sink2 is not None:
        in_specs.append(_full(sink2.shape))
        args.append(sink2)
    aliases = {}
    if out is not None:
        in_specs.append(pl.BlockSpec(memory_space=pl.ANY))
        args.append(out)
        aliases = {len(args) - 1: 0}
    kern = functools.partial(_attn_kernel, n_heads=n_heads, group=n_heads // hk, has_sink=sink2 is not None)
    if out is not None:
        kern = _drop_last_input(kern, len(args))
    return pl.pallas_call(
        kern,
        grid=(b, n_tiles),
        in_specs=in_specs,
        out_specs=pl.BlockSpec((None, n_heads * A_V, t), lambda bi, ti: (bi, 0, ti + tile0)),
        out_shape=jax.ShapeDtypeStruct((b, n_heads * A_V, s), BF16),
        input_output_aliases=aliases,
        scratch_shapes=[pltpu.VMEM((2, n_keys, t), F32), pltpu.VMEM((1, t), F32)],
        compiler_params=_params(2),
        name="attention",
    )(*args)


def _drop_last_input(kern, n_in):
    def wrapped(*refs):
        return kern(*refs[:n_in - 1], *refs[n_in:])
    return wrapped


def _window_caps(n_ctx, tq):
    n_band = WINDOW + tq
    n_keys = n_ctx + n_band + WINDOW
    row = jnp.arange(n_keys, dtype=jnp.int32)[:, None]
    col = jnp.arange(tq, dtype=jnp.int32)[None, :]
    rel = row - n_ctx - WINDOW - col
    in_band = (rel >= -WINDOW) & (rel <= WINDOW)
    caps = []
    for first in (False, True):
        for last in (False, True):
            ok_lo = (row >= n_ctx + WINDOW) | (not first)
            ok_hi = (row < n_ctx + n_band) | (not last)
            valid = (row < n_ctx) | (in_band & ok_lo & ok_hi)
            caps.append(jnp.where(valid, jnp.inf, NEG).astype(F32))
    return jnp.stack(caps)


def _window_kernel(q_ref, k_ref, v_ref, sink_ref, cap_ref, o_ref, k_scr, v_scr, s_scr, *, n_ctx, n_tiles):
    tq = q_ref.shape[2]
    ti = pl.program_id(1)
    start = pl.multiple_of(n_ctx + ti * tq, LANES)
    lo = pl.multiple_of(start - WINDOW, LANES)
    nxt = pl.multiple_of(jnp.minimum(start + tq, n_ctx + (n_tiles - 1) * tq + tq - WINDOW), LANES)
    n_band = WINDOW + tq
    n_keys = n_ctx + n_band + WINDOW

    for g in range(HC_KV):
        k_scr[g, 0:n_ctx, :] = k_ref[g, 0:n_ctx, :]
        k_scr[g, n_ctx:n_ctx + n_band, :] = k_ref[g, pl.ds(lo, n_band), :]
        k_scr[g, n_ctx + n_band:n_keys, :] = k_ref[g, pl.ds(nxt, WINDOW), :]
        v_scr[g, :, 0:n_ctx] = v_ref[g, :, 0:n_ctx]
        v_scr[g, :, n_ctx:n_ctx + n_band] = v_ref[g, :, pl.ds(lo, n_band)]
        v_scr[g, :, n_ctx + n_band:n_keys] = v_ref[g, :, pl.ds(nxt, WINDOW)]

    group = HC // HC_KV
    chunk = KEY_CHUNK
    n_chunks = n_keys // chunk

    def score_chunk(hh, c, mrun):
        rows = slice(c * chunk, (c + 1) * chunk)
        s = jnp.minimum(_dot(k_scr[hh // group, rows, :], q_ref[hh]), cap_ref[rows, :])
        s_scr[hh % 2, rows, :] = s
        part = jnp.max(s.reshape(chunk // SUBLANES, SUBLANES, tq), axis=0)
        return part if mrun is None else jnp.maximum(mrun, part)

    def output_chunk(hh, c, m, acc):
        rows = slice(c * chunk, (c + 1) * chunk)
        p = jnp.exp2(s_scr[hh % 2, rows, :] - m).astype(BF16)
        part = _dot(v_scr[hh // group, :, rows], p)
        return part if acc is None else acc + part

    lag = n_chunks + WINDOW_LAG
    maxes = {}
    mrun = acc = m = snk = None
    for g in range(HC * n_chunks + lag):
        if g < HC * n_chunks:
            hs, cs = divmod(g, n_chunks)
            mrun = score_chunk(hs, cs, mrun)
            if cs == n_chunks - 1:
                maxes[hs] = jnp.max(mrun, axis=0, keepdims=True)
                mrun = None
        if g >= lag:
            ho, co = divmod(g - lag, n_chunks)
            if co == 0:
                snk = sink_ref[ho]
                m = jnp.maximum(maxes.pop(ho), snk)
                acc = None
            acc = output_chunk(ho, co, m, acc)
            if co == n_chunks - 1:
                denom = acc[C_HD:C_HD + 1] + jnp.exp2(snk - m)
                o_ref[ho * C_HD:(ho + 1) * C_HD, :] = (acc[0:C_HD] * (1.0 / denom)).astype(BF16)


def _window_attention(qt, k, vt, sink2, caps, *, n_ctx):
    b, n_heads, dk, s = qt.shape
    hk = k.shape[1]
    t = TOKEN_TILE
    tile0 = n_ctx // t
    n_tiles = (s - n_ctx) // t
    n_win_keys = n_ctx + WINDOW + t + WINDOW
    return pl.pallas_call(
        functools.partial(_window_kernel, n_ctx=n_ctx, n_tiles=n_tiles),
        grid=(b, n_tiles),
        in_specs=[
            pl.BlockSpec((None, n_heads, dk, t), lambda bi, ti: (bi, 0, 0, ti + tile0)),
            pl.BlockSpec((None, hk, s, dk), lambda bi, ti: (bi, 0, 0, 0)),
            pl.BlockSpec((None, hk, V_ROWS, s), lambda bi, ti: (bi, 0, 0, 0)),
            _full(sink2.shape),
            pl.BlockSpec((None, n_win_keys, t),
                         lambda bi, ti: (2 * (ti == 0).astype(jnp.int32) + (ti == n_tiles - 1).astype(jnp.int32), 0, 0)),
        ],
        out_specs=pl.BlockSpec((None, n_heads * C_HD, t), lambda bi, ti: (bi, 0, ti + tile0)),
        out_shape=jax.ShapeDtypeStruct((b, n_heads * C_HD, s), BF16),
        scratch_shapes=[
            pltpu.VMEM((hk, n_win_keys, dk), BF16), pltpu.VMEM((hk, V_ROWS, n_win_keys), BF16),
            pltpu.VMEM((2, n_win_keys, t), F32),
        ],
        compiler_params=_params(2),
        name="window_attention",
    )(qt, k, vt, sink2, caps)


def _merge_kernel(*refs, n_src):
    mod_ref, g1_ref, oa_ref, ob_ref, oc_ref, wg_ref, bg_ref, wbr_ref, wout_ref, xo_ref = refs[n_src:]
    x = _token_tile(refs[:n_src], pl.program_id(1) == 0)
    h = _modulated_norm(x, g1_ref[...], mod_ref[0:1, :], mod_ref[1:2, :])
    hb = h.astype(BF16)
    y = None
    for i, o_ref in enumerate((oa_ref, ob_ref, oc_ref)):
        cols = slice(i * D_MODEL, (i + 1) * D_MODEL)
        gate = _sigmoid(_dot(hb, wg_ref[:, cols]) + bg_ref[:, cols])
        term = gate * _dot_tn(o_ref[...], wbr_ref[i])
        y = term if y is None else y + term
    z = _dot(y.astype(BF16), wout_ref[...])
    xo_ref[...] = x + mod_ref[2:3, :] * z


def _merge(src, mods, l, wts, oa, ob, oc, *, tile0, n_tiles):
    b = src[-1].shape[0]
    s = sum(a.shape[1] for a in src)
    t = TOKEN_TILE
    feat = lambda bi, ti: (bi, 0, ti + tile0)
    names = ("w_gate", "b_gate", "w_branch", "w_out")
    return pl.pallas_call(
        functools.partial(_merge_kernel, n_src=len(src)),
        grid=(b, n_tiles),
        in_specs=_token_specs(src, tile0) + [_mod_spec(l, tile0), _layer(wts["g1"], l)] + [
            pl.BlockSpec((None, BRANCH_W, t), feat),
            pl.BlockSpec((None, BRANCH_W, t), feat),
            pl.BlockSpec((None, BRANCH_W, t), feat),
        ] + [_layer(wts[n], l) for n in names],
        out_specs=pl.BlockSpec((None, t, D_MODEL), lambda bi, ti: (bi, ti + tile0, 0)),
        out_shape=jax.ShapeDtypeStruct((b, s, D_MODEL), F32),
        input_output_aliases={0: 0} if len(src) == 1 else {},
        compiler_params=_params(2),
        name="merge",
    )(*src, mods, wts["g1"], oa, ob, oc, *(wts[n] for n in names))


def _ffn_kernel(xp_ref, x_ref, xn_ref, mod_ref, g2_ref, wup_ref, wconv_ref, bconv_ref, wdown_ref, gf_ref,
                o_ref, act_ref, *, tile0, n_seq_tiles, final_norm):
    t = x_ref.shape[0]
    halo = SUBLANES
    ta = pl.program_id(1) + tile0
    x = x_ref[...]
    xa = jnp.concatenate([xp_ref[...], x, xn_ref[...]], axis=0)
    h = _modulated_norm(xa, g2_ref[...], mod_ref[3:4, :], mod_ref[4:5, :])
    row = lax.broadcasted_iota(jnp.int32, (t + 2 * halo, 1), 0)
    keep_prev = (ta > 1).astype(F32)
    keep_next = ((ta > 0) & (ta < n_seq_tiles - 1)).astype(F32)
    keep = jnp.where(row < halo, keep_prev, jnp.where(row >= t + halo, keep_next, 1.0))
    hb = (h * keep).astype(BF16)

    def conv(u, cols):
        return (u[halo - 1:halo - 1 + t] * wconv_ref[0:1, cols] + u[halo:halo + t] * wconv_ref[1:2, cols]
                + u[halo + 1:halo + 1 + t] * wconv_ref[2:3, cols] + bconv_ref[:, cols])

    for c in range(D_FF // FF_CHUNK):
        ca = slice(c * FF_CHUNK, (c + 1) * FF_CHUNK)
        cg = slice(D_FF + c * FF_CHUNK, D_FF + (c + 1) * FF_CHUNK)
        a = conv(_dot(hb, wup_ref[:, ca]), ca)
        gv = conv(_dot(hb, wup_ref[:, cg]), cg)
        act_ref[:, ca] = (a * _sigmoid(a) * gv).astype(BF16)
    y = x + mod_ref[5:6, :] * _dot(act_ref[...], wdown_ref[...])
    if final_norm:
        y = y * lax.rsqrt(jnp.mean(y * y, axis=-1, keepdims=True) + EPS) * gf_ref[...]
    o_ref[...] = y


def _conv_ffn(xc, mods, l, wts, g_final, *, tile0, n_tiles, final_norm):
    b, s, _ = xc.shape
    t = TOKEN_TILE
    n_seq_tiles = s // t
    per = t // SUBLANES
    last_blk = s // SUBLANES - 1
    out_rows = n_tiles * t if final_norm else s
    out_tile0 = 0 if final_norm else tile0
    return pl.pallas_call(
        functools.partial(_ffn_kernel, tile0=tile0, n_seq_tiles=n_seq_tiles, final_norm=final_norm),
        grid=(b, n_tiles),
        in_specs=[
            pl.BlockSpec((None, SUBLANES, D_MODEL), lambda bi, ti: (bi, jnp.maximum((ti + tile0) * per - 1, 0), 0)),
            pl.BlockSpec((None, t, D_MODEL), lambda bi, ti: (bi, ti + tile0, 0)),
            pl.BlockSpec((None, SUBLANES, D_MODEL),
                         lambda bi, ti: (bi, jnp.minimum((ti + tile0 + 1) * per, last_blk), 0)),
            _mod_spec(l, tile0),
            _layer(wts["g2"], l), _layer(wts["w_up"], l), _layer(wts["w_conv"], l), _layer(wts["b_conv"], l),
            _layer(wts["w_down"], l),
            _full((1, D_MODEL)),
        ],
        out_specs=pl.BlockSpec((None, t, D_MODEL), lambda bi, ti: (bi, ti + out_tile0, 0)),
        out_shape=jax.ShapeDtypeStruct((b, out_rows, D_MODEL), F32),
        scratch_shapes=[pltpu.VMEM((t, D_FF), BF16)],
        compiler_params=_params(2),
        name="conv_ffn",
    )(xc, xc, xc, mods, wts["g2"], wts["w_up"], wts["w_conv"], wts["b_conv"], wts["w_down"], g_final)


def _partner(head_dim):
    nf = head_dim // 4
    d = np.arange(head_dim)
    a, half, f = d // (2 * nf), (d % (2 * nf)) // nf, d % nf
    return a * 2 * nf + (1 - half) * nf + f, np.where(half == 0, -1.0, 1.0).astype(np.float32)


def _rope_tables(n_ctx, n_lat):
    def full(head_dim):
        nf = head_dim // 4
        _, sign = _partner(head_dim)
        rows = n_lat // GRID_W
        row = jnp.repeat(jnp.arange(rows, dtype=F32), GRID_W)
        col = jnp.tile(jnp.arange(GRID_W, dtype=F32), rows)
        inv = ROPE_THETA ** (-jnp.arange(nf, dtype=F32) / nf)
        ang = jnp.stack([row[:, None] * inv, col[:, None] * inv], axis=1)
        c = jnp.broadcast_to(jnp.cos(ang)[:, :, None, :], (n_lat, 2, 2, nf)).reshape(n_lat, head_dim)
        s = jnp.broadcast_to(jnp.sin(ang)[:, :, None, :], (n_lat, 2, 2, nf)).reshape(n_lat, head_dim) * sign
        c = jnp.concatenate([jnp.ones((n_ctx, head_dim), F32), c], axis=0)
        s = jnp.concatenate([jnp.zeros((n_ctx, head_dim), F32), s], axis=0)
        return c, s

    c64, s64 = full(B_HD)
    c32, s32 = full(A_ROPE)
    pad_a = lambda v: jnp.pad(v, ((0, 0), (A_NOPE, LANES - A_NOPE - A_ROPE)))
    return {
        "ck": jnp.tile(c64, (1, LANES // B_HD)), "sk": jnp.tile(s64, (1, LANES // B_HD)),
        "ca": pad_a(c32), "sa": pad_a(s32),
        "ctq": c64.T, "stq": s64.T, "cta": c32.T, "sta": s32.T,
    }


def _rot_ranges(base, head_dim, n_heads):
    nf = head_dim // 4
    out = []
    for hh in range(n_heads):
        b0 = base + hh * head_dim
        for a in range(2):
            out += [(b0 + a * 2 * nf + nf, b0 + a * 2 * nf + 2 * nf), (b0 + a * 2 * nf, b0 + a * 2 * nf + nf)]
    return out


def _prepare_weights(w_in, b_gate, g_norm1, g_q_a, w_q_b, g_kv_a, w_kv_b, g_qn, g_kn, w_branch, w_out, g_norm2,
                     w_up, w_conv, b_conv, w_down):
    depth = w_in.shape[0]
    cuts = [int(v) for v in np.cumsum((0,) + IN_SIZES)]
    aq, akv, akr, bq, bk, bv, cq, ck, cv, gl = ((cuts[i], cuts[i + 1]) for i in range(len(IN_SIZES)))
    cols = lambda ranges: [w_in[:, :, lo:hi] for lo, hi in ranges]
    zeros = lambda n: [jnp.zeros((depth, D_MODEL, n), F32)]
    pad_a = lambda pieces: zeros(A_NOPE) + pieces + zeros(LANES - A_NOPE - A_ROPE)
    w_std = jnp.concatenate(
        cols([akv, bk]) + cols(_rot_ranges(bk[0], B_HD, HB_KV)) + cols([ck]) + cols(_rot_ranges(ck[0], C_HD, HC_KV))
        + pad_a(cols([akr])) + pad_a(cols(_rot_ranges(akr[0], A_ROPE, 1))), axis=2)
    w_t = jnp.swapaxes(jnp.concatenate(cols([aq, akv, bq, cq, bv, cv]), axis=2), 1, 2)
    wqb = w_q_b.reshape(depth, A_Q_RANK, HA, A_NOPE + A_ROPE)
    wqb = jnp.pad(wqb, ((0, 0), (0, 0), (0, 0), (0, HEAD_PAD - A_NOPE - A_ROPE)))
    wqb = jnp.swapaxes(wqb.reshape(depth, A_Q_RANK, HA * HEAD_PAD), 1, 2)
    wkv = w_kv_b.reshape(depth, A_KV_RANK, HA, A_NOPE + A_V)
    w_k = jnp.pad(wkv[..., :A_NOPE], ((0, 0), (0, 0), (0, 0), (0, HEAD_PAD - A_NOPE)))
    w_k = w_k.reshape(depth, A_KV_RANK, HA * HEAD_PAD)
    w_v = jnp.swapaxes(wkv[..., A_NOPE:].reshape(depth, A_KV_RANK, HA * A_V), 1, 2)
    g_kn2 = jnp.tile(g_kn, (1, LANES // B_HD))
    p64, _ = _partner(B_HD)
    p128 = np.concatenate([p64 + i * B_HD for i in range(LANES // B_HD)])
    return {
        "g1": g_norm1.reshape(depth, 1, D_MODEL), "g2": g_norm2.reshape(depth, 1, D_MODEL),
        "w_std": w_std.astype(BF16), "w_t": w_t.astype(BF16), "w_qb": wqb.astype(BF16),
        "w_k": w_k.astype(BF16), "w_v": w_v.astype(BF16),
        "g_qa_col": g_q_a.reshape(depth, A_Q_RANK, 1), "g_kv_row": g_kv_a.reshape(depth, 1, A_KV_RANK),
        "g_kv_col": g_kv_a.reshape(depth, A_KV_RANK, 1), "g_qn_col": g_qn.reshape(depth, B_HD, 1),
        "g_kn_row": g_kn2.reshape(depth, 1, LANES), "g_kn_rot_row": g_kn2[:, p128].reshape(depth, 1, LANES),
        "w_gate": w_in[:, :, gl[0]:gl[1]].astype(BF16), "b_gate": b_gate.reshape(depth, 1, N_BRANCH * D_MODEL),
        "w_branch": w_branch.astype(BF16), "w_out": w_out.astype(BF16),
        "w_up": w_up.astype(BF16), "w_conv": w_conv, "b_conv": b_conv.reshape(depth, 1, 2 * D_FF),
        "w_down": w_down.astype(BF16),
    }


def kernel(x, c, ctx, c_ctx, w_mod, b_mod, g_norm1, w_in, b_gate, g_q_a, w_q_b, g_kv_a, w_kv_b, g_qn, g_kn, sink,
           w_branch, w_out, g_norm2, w_up, w_conv, b_conv, w_down, g_final):
    b, n_lat, d = x.shape
    n_ctx = ctx.shape[1]
    depth = w_mod.shape[0]
    t = TOKEN_TILE
    assert d == D_MODEL and n_ctx == t and n_lat % t == 0 and n_lat % GRID_W == 0 and b <= MOD_ROWS // 2
    assert depth >= 2
    s = n_ctx + n_lat
    n_t = s // t
    scale_a = 1.0 / math.sqrt(A_NOPE + A_ROPE)
    scale_h = 1.0 / math.sqrt(B_HD)

    cvec = jnp.zeros((MOD_ROWS, D_MODEL), F32).at[:b].set(c).at[MOD_ROWS // 2].set(c_ctx)
    mods = _modulation(cvec, w_mod, b_mod)
    tabs = _rope_tables(n_ctx, n_lat)
    wts = _prepare_weights(w_in, b_gate, g_norm1, g_q_a, w_q_b, g_kv_a, w_kv_b, g_qn, g_kn, w_branch, w_out, g_norm2,
                           w_up, w_conv, b_conv, w_down)
    gf = g_final.reshape(1, D_MODEL)
    caps = _window_caps(n_ctx, t)
    sinks =jnp.broadcast_to((sink * LOG2E).reshape(depth, HC, 1, 1), (depth, HC, 1, t)).astype(F32)

    src = (ctx, x)
    for l in range(depth):
        last = l == depth - 1
        qa, ka, va, qb, kb, vb, qc, kc, vc = _project(src, mods, l, wts, tabs, scale_a, scale_h)
        lat = dict(n_keys=s, tile0=1, n_tiles=n_t - 1)
        oa = _attention(qa, ka, va, None, **lat)
        ob = _attention(qb, kb, vb, None, **lat)
        oc = _window_attention(qc, kc, vc, sinks[l], caps, n_ctx=n_ctx)
        if not last:
            cx = dict(n_keys=n_ctx, tile0=0, n_tiles=1)
            oa = _attention(qa, ka, va, None, out=oa, **cx)
            ob = _attention(qb, kb, vb, None, out=ob, **cx)
            oc = _attention(qc, kc, vc, sinks[l], out=oc, **cx)
        tiles = dict(tile0=1, n_tiles=n_t - 1) if last else dict(tile0=0, n_tiles=n_t)
        xc = _merge(src, mods, l, wts, oa, ob, oc, **tiles)
        src = (_conv_ffn(xc, mods, l, wts, gf, final_norm=last, **tiles),)
    return src[0]
```

```python
import functools
import math

import jax
import jax.numpy as jnp
import numpy as np
from jax import lax
from jax.experimental import pallas as pl
from jax.experimental.pallas import tpu as pltpu

F32 = jnp.float32
BF16 = jnp.bfloat16

D_MODEL = 1024
GRID_W = 64
EPS = 1e-6
ROPE_THETA = 10000.0
WINDOW = 128
NEG = -1e30
HA, A_NOPE, A_ROPE, A_V, A_Q_RANK, A_KV_RANK = 8, 64, 32, 64, 256, 128
HB, HB_KV, B_HD = 8, 2, 64
HC, HC_KV, C_HD = 8, 2, 64
BRANCH_W = 512
N_BRANCH = 3
D_FF = 2816
CONV_W = 3
N_MOD = 6
IN_SIZES = (A_Q_RANK, A_KV_RANK, A_ROPE, HB * B_HD, HB_KV * B_HD, HB_KV * B_HD, HC * C_HD, HC_KV * C_HD,
            HC_KV * C_HD, N_BRANCH * D_MODEL)

LANES = 128
SUBLANES = 8
BF16_ROWS = 16
TOKEN_TILE = 256
QUERY_TILE = 512
MOD_ROWS = 16
HEAD_PAD = 128
V_ROWS = A_V + BF16_ROWS
FF_CHUNK = 256
KEY_CHUNK = 256
OUTPUT_LAG = 3
WINDOW_LAG = 1
VMEM_LIMIT = 56 * 1024 * 1024
LOG2E = 1.4426950408889634


def _dot(a, b):
    return jnp.dot(a, b, preferred_element_type=F32)


def _dot_nt(a, b):
    return lax.dot_general(a, b, (((1,), (1,)), ((), ())), preferred_element_type=F32)


def _dot_tn(a, b):
    return lax.dot_general(a, b, (((0,), (0,)), ((), ())), preferred_element_type=F32)


def _sigmoid(x):
    return 1.0 / (1.0 + jnp.exp(-x))


def _modulated_norm(x, g, shift, scale):
    ms = jnp.mean(x * x, axis=-1, keepdims=True)
    return (x * lax.rsqrt(ms + EPS) * g) * (1.0 + scale) + shift


def _params(n_grid):
    return pltpu.CompilerParams(dimension_semantics=("arbitrary",) * n_grid, vmem_limit_bytes=VMEM_LIMIT)


def _full(shape):
    nd = len(shape)
    return pl.BlockSpec(shape, lambda *_: (0,) * nd)


def _layer(arr, l):
    nd = arr.ndim - 1
    return pl.BlockSpec((None,) + arr.shape[1:], lambda *_: (l,) + (0,) * nd)


def _mod_spec(l, tile0):
    return pl.BlockSpec((None, None, N_MOD, D_MODEL),
                        lambda bi, ti: (l, jnp.where(ti + tile0 == 0, MOD_ROWS // 2, bi), 0, 0))


def _token_specs(src, tile0):
    t = TOKEN_TILE
    if len(src) == 1:
        return [pl.BlockSpec((None, t, D_MODEL), lambda bi, ti: (bi, ti + tile0, 0))]
    assert tile0 == 0
    return [pl.BlockSpec((None, t, D_MODEL), lambda bi, ti: (bi, 0, 0)),
            pl.BlockSpec((None, t, D_MODEL), lambda bi, ti: (bi, jnp.maximum(ti - 1, 0), 0))]


def _q_slot(tile, n_seq_tiles):
    return (tile + n_seq_tiles - 1) % n_seq_tiles


def _token_tile(refs, is_ctx):
    if len(refs) == 1:
        return refs[0][...]
    return jnp.where(is_ctx, refs[0][...], refs[1][...])


def _mod_kernel(c_ref, w_ref, b_ref, o_ref):
    c = c_ref[...]
    a = c * _sigmoid(c)
    o_ref[...] = jnp.dot(a, w_ref[...], precision=lax.Precision.HIGHEST, preferred_element_type=F32) + b_ref[...]


def _modulation(cvec, w_mod, b_mod):
    depth = w_mod.shape[0]
    n_col = N_MOD * D_MODEL
    out = pl.pallas_call(
        _mod_kernel,
        grid=(depth, N_MOD),
        in_specs=[
            pl.BlockSpec((MOD_ROWS, D_MODEL), lambda l, j: (0, 0)),
            pl.BlockSpec((None, D_MODEL, D_MODEL), lambda l, j: (l, 0, j)),
            pl.BlockSpec((None, 1, D_MODEL), lambda l, j: (l, 0, j)),
        ],
        out_specs=pl.BlockSpec((None, MOD_ROWS, D_MODEL), lambda l, j: (l, 0, j)),
        out_shape=jax.ShapeDtypeStruct((depth, MOD_ROWS, n_col), F32),
        compiler_params=_params(2),
        name="modulation",
    )(cvec, w_mod, b_mod.reshape(depth, 1, n_col))
    return out.reshape(depth, MOD_ROWS, N_MOD, D_MODEL)


STD_AKV, STD_BK, STD_BKR, STD_CK, STD_CKR, STD_KR, STD_KRR = (i * LANES for i in range(7))
STD_COLS = 7 * LANES
T_AQ = 0
T_AKV = T_AQ + A_Q_RANK
T_BQ = T_AKV + A_KV_RANK
T_CQ = T_BQ + HB * B_HD
T_BV = T_CQ + HC * C_HD
T_CV = T_BV + HB_KV * B_HD
T_ROWS = T_CV + HC_KV * C_HD


def _swap_halves(x, nf):
    parts = []
    for a in range(2):
        base = a * 2 * nf
        parts += [x[base + nf:base + 2 * nf], x[base:base + nf]]
    return jnp.concatenate(parts, axis=0)


def _ones_rows(t):
    row = lax.broadcasted_iota(jnp.int32, (BF16_ROWS, t), 0)
    return jnp.where(row == 0, 1.0, 0.0).astype(BF16)


def _proj_kernel(*refs, n_src, n_tiles, n_flat, scale_a, scale_h):
    (mod_ref, g1_ref, wstd_ref, wt_ref, wqb_ref, wk_ref, wv_ref,
     gqa_ref, gkvr_ref, gkvc_ref, gqn_ref, gknr_ref, gknrr_ref,
     ck_ref, sk_ref, ca_ref, sa_ref, ctq_ref, stq_ref, cta_ref, sta_ref,
     qa_ref, ka_ref, va_ref, qb_ref, kb_ref, vb_ref, qc_ref, kc_ref, vc_ref, ps_scr, pt_scr) = refs[n_src:]
    t = TOKEN_TILE
    j = pl.program_id(0)

    @pl.when(j == 0)
    def _():
        ps_scr[1] = jnp.zeros(ps_scr.shape[1:], F32)
        pt_scr[1] = jnp.zeros(pt_scr.shape[1:], F32)

    def stage1(slot):
        is_ctx = jnp.minimum(j, n_flat - 1) % n_tiles == 0
        h = _modulated_norm(_token_tile(refs[:n_src], is_ctx), g1_ref[...], mod_ref[0:1, :], mod_ref[1:2, :])
        hb = h.astype(BF16)
        ps_scr[slot] = _dot(hb, wstd_ref[...])
        pt_scr[slot] = _dot_nt(wt_ref[...], hb)

    def stage2(slot):
        ps, pt = ps_scr.at[slot], pt_scr.at[slot]
        ones = _ones_rows(t)

        aq = pt[T_AQ:T_AQ + A_Q_RANK]
        aqn = aq * lax.rsqrt(jnp.mean(aq * aq, axis=0, keepdims=True) + EPS) * gqa_ref[...]
        qt = _dot(wqb_ref[...], aqn.astype(BF16))
        cta, sta = cta_ref[...], sta_ref[...]
        qs = scale_a * LOG2E
        for hh in range(HA):
            base = hh * HEAD_PAD
            qa_ref[hh, 0:A_NOPE, :] = (qt[base:base + A_NOPE] * qs).astype(BF16)
            r = qt[base + A_NOPE:base + A_NOPE + A_ROPE]
            rr = r * cta + _swap_halves(r, A_ROPE // 4) * sta
            qa_ref[hh, A_NOPE:A_NOPE + A_ROPE, :] = (rr * qs).astype(BF16)
            qa_ref[hh, A_NOPE + A_ROPE:HEAD_PAD, :] = jnp.zeros((HEAD_PAD - A_NOPE - A_ROPE, t), BF16)

        akv = ps[:, STD_AKV:STD_AKV + LANES]
        akvn = akv * lax.rsqrt(jnp.mean(akv * akv, axis=-1, keepdims=True) + EPS) * gkvr_ref[...]
        kn = _dot(akvn.astype(BF16), wk_ref[...])
        kr = ps[:, STD_KR:STD_KR + LANES] * ca_ref[...] + ps[:, STD_KRR:STD_KRR + LANES] * sa_ref[...]
        for hh in range(HA):
            ka_ref[hh] = (kn[:, hh * HEAD_PAD:(hh + 1) * HEAD_PAD] + kr).astype(BF16)

        akvt = pt[T_AKV:T_AKV + A_KV_RANK]
        akvtn = akvt * lax.rsqrt(jnp.mean(akvt * akvt, axis=0, keepdims=True) + EPS) * gkvc_ref[...]
        vt = _dot(wv_ref[...], akvtn.astype(BF16))
        for hh in range(HA):
            va_ref[hh, 0:A_V, :] = vt[hh * A_V:(hh + 1) * A_V].astype(BF16)
            va_ref[hh, A_V:V_ROWS, :] = ones

        ctq, stq = ctq_ref[...], stq_ref[...]
        ck, sk = ck_ref[...], sk_ref[...]
        qsh = scale_h * LOG2E
        lane = lax.broadcasted_iota(jnp.int32, (t, LANES), 1)
        first = lane < B_HD

        for hh in range(HB):
            blk = pt[T_BQ + hh * B_HD:T_BQ + (hh + 1) * B_HD]
            y = blk * lax.rsqrt(jnp.mean(blk * blk, axis=0, keepdims=True) + EPS) * gqn_ref[...]
            qb_ref[hh] = ((y * ctq + _swap_halves(y, B_HD // 4) * stq) * qsh).astype(BF16)
            blk = pt[T_CQ + hh * C_HD:T_CQ + (hh + 1) * C_HD]
            qc_ref[hh] = ((blk * ctq + _swap_halves(blk, C_HD // 4) * stq) * qsh).astype(BF16)

        bk = ps[:, STD_BK:STD_BK + LANES]
        sq = bk * bk
        s0 = jnp.sum(jnp.where(first, sq, 0.0), axis=-1, keepdims=True)
        s1 = jnp.sum(jnp.where(first, 0.0, sq), axis=-1, keepdims=True)
        rk = lax.rsqrt(jnp.where(first, s0, s1) * (1.0 / B_HD) + EPS)
        kb = (bk * rk * gknr_ref[...]) * ck + (ps[:, STD_BKR:STD_BKR + LANES] * rk * gknrr_ref[...]) * sk
        kc = ps[:, STD_CK:STD_CK + LANES] * ck + ps[:, STD_CKR:STD_CKR + LANES] * sk
        for g in range(HB_KV):
            kb_ref[g] = kb[:, g * B_HD:(g + 1) * B_HD].astype(BF16)
            kc_ref[g] = kc[:, g * C_HD:(g + 1) * C_HD].astype(BF16)
            vb_ref[g, 0:B_HD, :] = pt[T_BV + g * B_HD:T_BV + (g + 1) * B_HD].astype(BF16)
            vb_ref[g, B_HD:V_ROWS, :] = ones
            vc_ref[g, 0:C_HD, :] = pt[T_CV + g * C_HD:T_CV + (g + 1) * C_HD].astype(BF16)
            vc_ref[g, C_HD:V_ROWS, :] = ones

    @pl.when(j % 2 == 0)
    def _():
        stage1(0)
        stage2(1)

    @pl.when(j % 2 == 1)
    def _():
        stage1(1)
        stage2(0)


def _project(src, mods, l, wts, tabs, scale_a, scale_h):
    b = src[-1].shape[0]
    s = sum(a.shape[1] for a in src)
    t = TOKEN_TILE
    n_t = s // t
    n_flat = b * n_t
    tile1 = lambda j: jnp.minimum(j, n_flat - 1)
    tile2 = lambda j: jnp.maximum(j - 1, 0)
    if len(src) == 1:
        tok_specs = [pl.BlockSpec((None, t, D_MODEL), lambda j: (tile1(j) // n_t, tile1(j) % n_t, 0))]
    else:
        tok_specs = [pl.BlockSpec((None, t, D_MODEL), lambda j: (tile1(j) // n_t, 0, 0)),
                     pl.BlockSpec((None, t, D_MODEL),
                                  lambda j: (tile1(j) // n_t, jnp.maximum(tile1(j) % n_t - 1, 0), 0))]
    mod_spec = pl.BlockSpec(
        (None, None, N_MOD, D_MODEL),
        lambda j: (l, jnp.where(tile1(j) % n_t == 0, MOD_ROWS // 2, tile1(j) // n_t), 0, 0))
    tok2 = lambda j: (tile2(j) % n_t, 0)
    feat2 = lambda j: (0, tile2(j) % n_t)
    names = ("g1", "w_std", "w_t", "w_qb", "w_k", "w_v", "g_qa_col", "g_kv_row", "g_kv_col", "g_qn_col", "g_kn_row",
             "g_kn_rot_row")
    in_specs = tok_specs + [mod_spec] + [_layer(wts[n], l) for n in names] + [
        pl.BlockSpec((t, LANES), tok2), pl.BlockSpec((t, LANES), tok2),
        pl.BlockSpec((t, LANES), tok2), pl.BlockSpec((t, LANES), tok2),
        pl.BlockSpec((B_HD, t), feat2), pl.BlockSpec((B_HD, t), feat2),
        pl.BlockSpec((A_ROPE, t), feat2), pl.BlockSpec((A_ROPE, t), feat2),
    ]
    qspec = lambda heads, rows: pl.BlockSpec((None, heads, rows, t),
                                             lambda j: (tile2(j) // n_t, 0, 0, _q_slot(tile2(j) % n_t, n_t)))
    vspec = lambda heads, rows: pl.BlockSpec((None, heads, rows, t),
                                             lambda j: (tile2(j) // n_t, 0, 0, tile2(j) % n_t))
    kspec = lambda heads, cols: pl.BlockSpec((None, heads, t, cols),
                                             lambda j: (tile2(j) // n_t, 0, tile2(j) % n_t, 0))
    out_specs = [
        qspec(HA, HEAD_PAD), kspec(HA, HEAD_PAD), vspec(HA, V_ROWS),
        qspec(HB, B_HD), kspec(HB_KV, B_HD), vspec(HB_KV, V_ROWS),
        qspec(HC, C_HD), kspec(HC_KV, C_HD), vspec(HC_KV, V_ROWS),
    ]
    sd = jax.ShapeDtypeStruct
    out_shape = [
        sd((b, HA, HEAD_PAD, s), BF16), sd((b, HA, s, HEAD_PAD), BF16), sd((b, HA, V_ROWS, s), BF16),
        sd((b, HB, B_HD, s), BF16), sd((b, HB_KV, s, B_HD), BF16), sd((b, HB_KV, V_ROWS, s), BF16),
        sd((b, HC, C_HD, s), BF16), sd((b, HC_KV, s, C_HD), BF16), sd((b, HC_KV, V_ROWS, s), BF16),
    ]
    return pl.pallas_call(
        functools.partial(_proj_kernel, n_src=len(src), n_tiles=n_t, n_flat=n_flat, scale_a=scale_a,
                          scale_h=scale_h),
        grid=(n_flat + 1,),
        in_specs=in_specs,
        out_specs=out_specs,
        out_shape=out_shape,
        scratch_shapes=[pltpu.VMEM((2, t, STD_COLS), F32), pltpu.VMEM((2, T_ROWS, t), F32)],
        compiler_params=_params(1),
        name="projection",
    )(*src, mods, *(wts[n] for n in names),
      tabs["ck"], tabs["sk"], tabs["ca"], tabs["sa"], tabs["ctq"], tabs["stq"], tabs["cta"], tabs["sta"])


def _attn_kernel(*refs, n_heads, group, has_sink):
    if has_sink:
        q_ref, qn_ref, k_ref, v_ref, sink_ref, o_ref, s_scr, m_scr = refs
    else:
        q_ref, qn_ref, k_ref, v_ref, o_ref, s_scr, m_scr = refs
    n_keys, tq = s_scr.shape[1], s_scr.shape[2]
    chunk = min(KEY_CHUNK, n_keys)
    n_chunks = n_keys // chunk
    lag = min(OUTPUT_LAG, n_chunks - 1)

    def score_chunk(head, c, mrun):
        q = qn_ref[0] if head == n_heads else q_ref[head]
        rows = slice(c * chunk, (c + 1) * chunk)
        s = _dot(k_ref[(head % n_heads) // group, rows, :], q)
        s_scr[head % 2, rows, :] = s
        part = jnp.max(s.reshape(chunk // SUBLANES, SUBLANES, tq), axis=0)
        return part if mrun is None else jnp.maximum(mrun, part)

    def output_chunk(head, c, m, acc):
        rows = slice(c * chunk, (c + 1) * chunk)
        p = jnp.exp2(s_scr[head % 2, rows, :] - m).astype(BF16)
        part = _dot(v_ref[head // group, :, rows], p)
        return part if acc is None else acc + part

    @pl.when(pl.program_id(1) == 0)
    def _():
        mrun = None
        for c in range(n_chunks):
            mrun = score_chunk(0, c, mrun)
        m_scr[...] = jnp.max(mrun, axis=0, keepdims=True)

    maxes = {0: m_scr[...]}
    mrun = acc = m = snk = None
    for g in range(n_heads * n_chunks + lag):
        if g < n_heads * n_chunks:
            hs, cs = divmod(g, n_chunks)
            mrun = score_chunk(hs + 1, cs, mrun)
            if cs == n_chunks - 1:
                maxes[hs + 1] = jnp.max(mrun, axis=0, keepdims=True)
                mrun = None
        if g >= lag:
            ho, co = divmod(g - lag, n_chunks)
            if co == 0:
                m = maxes.pop(ho)
                if has_sink:
                    snk = sink_ref[ho]
                    m = jnp.maximum(m, snk)
                acc = None
            acc = output_chunk(ho, co, m, acc)
            if co == n_chunks - 1:
                denom = acc[A_V:A_V + 1]
                if has_sink:
                    denom = denom + jnp.exp2(snk - m)
                o_ref[ho * A_V:(ho + 1) * A_V, :] = (acc[0:A_V] * (1.0 / denom)).astype(BF16)
    m_scr[...] = maxes[n_heads]


def _attention(qt, k, vt, sink2, *, n_keys, tq, q0, n_tiles, out=None):
    b, n_heads, dk, s = qt.shape
    hk = k.shape[1]
    t = tq
    tile0 = q0
    in_specs = [
        pl.BlockSpec((None, n_heads, dk, t), lambda bi, ti: (bi, 0, 0, ti + tile0)),
        pl.BlockSpec((None, 1, dk, t), lambda bi, ti: (bi, 0, 0, jnp.minimum(ti + 1, n_tiles - 1) + tile0)),
        pl.BlockSpec((None, hk, n_keys, k.shape[3]), lambda bi, ti: (bi, 0, 0, 0)),
        pl.BlockSpec((None, hk, V_ROWS, n_keys), lambda bi, ti: (bi, 0, 0, 0)),
    ]
    args = [qt, qt, k, vt]
    if sink2 is not None:
        in_specs.append(_full(sink2.shape))
        args.append(sink2)
    aliases = {}
    if out is not None:
        in_specs.append(pl.BlockSpec(memory_space=pl.ANY))
        args.append(out)
        aliases = {len(args) - 1: 0}
    kern = functools.partial(_attn_kernel, n_heads=n_heads, group=n_heads // hk, has_sink=sink2 is not None)
    if out is not None:
        kern = _drop_last_input(kern, len(args))
    return pl.pallas_call(
        kern,
        grid=(b, n_tiles),
        in_specs=in_specs,
        out_specs=pl.BlockSpec((None, n_heads * A_V, t), lambda bi, ti: (bi, 0, ti + tile0)),
        out_shape=jax.ShapeDtypeStruct((b, n_heads * A_V, s), BF16),
        input_output_aliases=aliases,
        scratch_shapes=[pltpu.VMEM((2, n_keys, t), F32), pltpu.VMEM((1, t), F32)],
        compiler_params=_params(2),
        name="attention",
    )(*args)


def _drop_last_input(kern, n_in):
    def wrapped(*refs):
        return kern(*refs[:n_in - 1], *refs[n_in:])
    return wrapped


def _window_caps(n_ctx, tq):
    n_band = WINDOW + tq
    n_keys = n_ctx + n_band + WINDOW
    row = jnp.arange(n_keys, dtype=jnp.int32)[:, None]
    col = jnp.arange(tq, dtype=jnp.int32)[None, :]
    rel = row - n_ctx - WINDOW - col
    in_band = (rel >= -WINDOW) & (rel <= WINDOW)
    caps = []
    for first in (False, True):
        for last in (False, True):
            ok_lo = (row >= n_ctx + WINDOW) | (not first)
            ok_hi = (row < n_ctx + n_band) | (not last)
            valid = (row < n_ctx) | (in_band & ok_lo & ok_hi)
            caps.append(jnp.where(valid, jnp.inf, NEG).astype(F32))
    return jnp.stack(caps)


def _window_kernel(q_ref, k_ref, v_ref, sink_ref, cap_ref, o_ref, k_scr, v_scr, s_scr, *, n_ctx, n_tiles):
    tq = q_ref.shape[2]
    ti = pl.program_id(1)
    start = pl.multiple_of(n_ctx + ti * tq, LANES)
    lo = pl.multiple_of(start - WINDOW, LANES)
    nxt = pl.multiple_of(jnp.minimum(start + tq, n_ctx + (n_tiles - 1) * tq + tq - WINDOW), LANES)
    n_band = WINDOW + tq
    n_keys = n_ctx + n_band + WINDOW

    for g in range(HC_KV):
        k_scr[g, 0:n_ctx, :] = k_ref[g, 0:n_ctx, :]
        k_scr[g, n_ctx:n_ctx + n_band, :] = k_ref[g, pl.ds(lo, n_band), :]
        k_scr[g, n_ctx + n_band:n_keys, :] = k_ref[g, pl.ds(nxt, WINDOW), :]
        v_scr[g, :, 0:n_ctx] = v_ref[g, :, 0:n_ctx]
        v_scr[g, :, n_ctx:n_ctx + n_band] = v_ref[g, :, pl.ds(lo, n_band)]
        v_scr[g, :, n_ctx + n_band:n_keys] = v_ref[g, :, pl.ds(nxt, WINDOW)]

    group = HC // HC_KV
    chunk = KEY_CHUNK
    n_chunks = n_keys // chunk

    def score_chunk(hh, c, mrun):
        rows = slice(c * chunk, (c + 1) * chunk)
        s = jnp.minimum(_dot(k_scr[hh // group, rows, :], q_ref[hh]), cap_ref[rows, :])
        s_scr[hh % 2, rows, :] = s
        part = jnp.max(s.reshape(chunk // SUBLANES, SUBLANES, tq), axis=0)
        return part if mrun is None else jnp.maximum(mrun, part)

    def output_chunk(hh, c, m, acc):
        rows = slice(c * chunk, (c + 1) * chunk)
        p = jnp.exp2(s_scr[hh % 2, rows, :] - m).astype(BF16)
        part = _dot(v_scr[hh // group, :, rows], p)
        return part if acc is None else acc + part

    lag = n_chunks + WINDOW_LAG
    maxes = {}
    mrun = acc = m = snk = None
    for g in range(HC * n_chunks + lag):
        if g < HC * n_chunks:
            hs, cs = divmod(g, n_chunks)
            mrun = score_chunk(hs, cs, mrun)
            if cs == n_chunks - 1:
                maxes[hs] = jnp.max(mrun, axis=0, keepdims=True)
                mrun = None
        if g >= lag:
            ho, co = divmod(g - lag, n_chunks)
            if co == 0:
                snk = sink_ref[ho]
                m = jnp.maximum(maxes.pop(ho), snk)
                acc = None
            acc = output_chunk(ho, co, m, acc)
            if co == n_chunks - 1:
                denom = acc[C_HD:C_HD + 1] + jnp.exp2(snk - m)
                o_ref[ho * C_HD:(ho + 1) * C_HD, :] = (acc[0:C_HD] * (1.0 / denom)).astype(BF16)


def _window_attention(qt, k, vt, sink2, caps, *, n_ctx):
    b, n_heads, dk, s = qt.shape
    hk = k.shape[1]
    t = TOKEN_TILE
    n_tiles = (s - n_ctx) // t
    n_win_keys = n_ctx + WINDOW + t + WINDOW
    return pl.pallas_call(
        functools.partial(_window_kernel, n_ctx=n_ctx, n_tiles=n_tiles),
        grid=(b, n_tiles),
        in_specs=[
            pl.BlockSpec((None, n_heads, dk, t), lambda bi, ti: (bi, 0, 0, ti)),
            pl.BlockSpec((None, hk, s, dk), lambda bi, ti: (bi, 0, 0, 0)),
            pl.BlockSpec((None, hk, V_ROWS, s), lambda bi, ti: (bi, 0, 0, 0)),
            _full(sink2.shape),
            pl.BlockSpec((None, n_win_keys, t),
                         lambda bi, ti: (2 * (ti == 0).astype(jnp.int32) + (ti == n_tiles - 1).astype(jnp.int32), 0, 0)),
        ],
        out_specs=pl.BlockSpec((None, n_heads * C_HD, t), lambda bi, ti: (bi, 0, ti)),
        out_shape=jax.ShapeDtypeStruct((b, n_heads * C_HD, s), BF16),
        scratch_shapes=[
            pltpu.VMEM((hk, n_win_keys, dk), BF16), pltpu.VMEM((hk, V_ROWS, n_win_keys), BF16),
            pltpu.VMEM((2, n_win_keys, t), F32),
        ],
        compiler_params=_params(2),
        name="window_attention",
    )(qt, k, vt, sink2, caps)


def _merge_kernel(*refs, n_src):
    mod_ref, g1_ref, oa_ref, ob_ref, oc_ref, wg_ref, bg_ref, wbr_ref, wout_ref, xo_ref = refs[n_src:]
    x = _token_tile(refs[:n_src], pl.program_id(1) == 0)
    h = _modulated_norm(x, g1_ref[...], mod_ref[0:1, :], mod_ref[1:2, :])
    hb = h.astype(BF16)
    y = None
    for i, o_ref in enumerate((oa_ref, ob_ref, oc_ref)):
        cols = slice(i * D_MODEL, (i + 1) * D_MODEL)
        gate = _sigmoid(_dot(hb, wg_ref[:, cols]) + bg_ref[:, cols])
        term = gate * _dot_tn(o_ref[...], wbr_ref[i])
        y = term if y is None else y + term
    z = _dot(y.astype(BF16), wout_ref[...])
    xo_ref[...] = x + mod_ref[2:3, :] * z


def _merge(src, mods, l, wts, oa, ob, oc, *, tile0, n_tiles):
    b = src[-1].shape[0]
    s = sum(a.shape[1] for a in src)
    t = TOKEN_TILE
    feat = lambda bi, ti: (bi, 0, _q_slot(ti + tile0, s // t))
    names = ("w_gate", "b_gate", "w_branch", "w_out")
    return pl.pallas_call(
        functools.partial(_merge_kernel, n_src=len(src)),
        grid=(b, n_tiles),
        in_specs=_token_specs(src, tile0) + [_mod_spec(l, tile0), _layer(wts["g1"], l)] + [
            pl.BlockSpec((None, BRANCH_W, t), feat),
            pl.BlockSpec((None, BRANCH_W, t), feat),
            pl.BlockSpec((None, BRANCH_W, t), feat),
        ] + [_layer(wts[n], l) for n in names],
        out_specs=pl.BlockSpec((None, t, D_MODEL), lambda bi, ti: (bi, ti + tile0, 0)),
        out_shape=jax.ShapeDtypeStruct((b, s, D_MODEL), F32),
        input_output_aliases={0: 0} if len(src) == 1 else {},
        compiler_params=_params(2),
        name="merge",
    )(*src, mods, wts["g1"], oa, ob, oc, *(wts[n] for n in names))


def _ffn_kernel(xp_ref, x_ref, xn_ref, mod_ref, g2_ref, wup_ref, wconv_ref, bconv_ref, wdown_ref, gf_ref,
                o_ref, act_ref, *, tile0, n_seq_tiles, final_norm):
    t = x_ref.shape[0]
    halo = SUBLANES
    ta = pl.program_id(1) + tile0
    x = x_ref[...]
    xa = jnp.concatenate([xp_ref[...], x, xn_ref[...]], axis=0)
    h = _modulated_norm(xa, g2_ref[...], mod_ref[3:4, :], mod_ref[4:5, :])
    row = lax.broadcasted_iota(jnp.int32, (t + 2 * halo, 1), 0)
    keep_prev = (ta > 1).astype(F32)
    keep_next = ((ta > 0) & (ta < n_seq_tiles - 1)).astype(F32)
    keep = jnp.where(row < halo, keep_prev, jnp.where(row >= t + halo, keep_next, 1.0))
    hb = (h * keep).astype(BF16)

    def conv(u, cols):
        return (u[halo - 1:halo - 1 + t] * wconv_ref[0:1, cols] + u[halo:halo + t] * wconv_ref[1:2, cols]
                + u[halo + 1:halo + 1 + t] * wconv_ref[2:3, cols] + bconv_ref[:, cols])

    for c in range(D_FF // FF_CHUNK):
        ca = slice(c * FF_CHUNK, (c + 1) * FF_CHUNK)
        cg = slice(D_FF + c * FF_CHUNK, D_FF + (c + 1) * FF_CHUNK)
        a = conv(_dot(hb, wup_ref[:, ca]), ca)
        gv = conv(_dot(hb, wup_ref[:, cg]), cg)
        act_ref[:, ca] = (a * _sigmoid(a) * gv).astype(BF16)
    y = x + mod_ref[5:6, :] * _dot(act_ref[...], wdown_ref[...])
    if final_norm:
        y = y * lax.rsqrt(jnp.mean(y * y, axis=-1, keepdims=True) + EPS) * gf_ref[...]
    o_ref[...] = y


def _conv_ffn(xc, mods, l, wts, g_final, *, tile0, n_tiles, final_norm):
    b, s, _ = xc.shape
    t = TOKEN_TILE
    n_seq_tiles = s // t
    per = t // SUBLANES
    last_blk = s // SUBLANES - 1
    out_rows = n_tiles * t if final_norm else s
    out_tile0 = 0 if final_norm else tile0
    return pl.pallas_call(
        functools.partial(_ffn_kernel, tile0=tile0, n_seq_tiles=n_seq_tiles, final_norm=final_norm),
        grid=(b, n_tiles),
        in_specs=[
            pl.BlockSpec((None, SUBLANES, D_MODEL), lambda bi, ti: (bi, jnp.maximum((ti + tile0) * per - 1, 0), 0)),
            pl.BlockSpec((None, t, D_MODEL), lambda bi, ti: (bi, ti + tile0, 0)),
            pl.BlockSpec((None, SUBLANES, D_MODEL),
                         lambda bi, ti: (bi, jnp.minimum((ti + tile0 + 1) * per, last_blk), 0)),
            _mod_spec(l, tile0),
            _layer(wts["g2"], l), _layer(wts["w_up"], l), _layer(wts["w_conv"], l), _layer(wts["b_conv"], l),
            _layer(wts["w_down"], l),
            _full((1, D_MODEL)),
        ],
        out_specs=pl.BlockSpec((None, t, D_MODEL), lambda bi, ti: (bi, ti + out_tile0, 0)),
        out_shape=jax.ShapeDtypeStruct((b, out_rows, D_MODEL), F32),
        scratch_shapes=[pltpu.VMEM((t, D_FF), BF16)],
        compiler_params=_params(2),
        name="conv_ffn",
    )(xc, xc, xc, mods, wts["g2"], wts["w_up"], wts["w_conv"], wts["b_conv"], wts["w_down"], g_final)


def _partner(head_dim):
    nf = head_dim // 4
    d = np.arange(head_dim)
    a, half, f = d // (2 * nf), (d % (2 * nf)) // nf, d % nf
    return a * 2 * nf + (1 - half) * nf + f, np.where(half == 0, -1.0, 1.0).astype(np.float32)


def _rope_tables(n_ctx, n_lat):
    def full(head_dim):
        nf = head_dim // 4
        _, sign = _partner(head_dim)
        rows = n_lat // GRID_W
        row = jnp.repeat(jnp.arange(rows, dtype=F32), GRID_W)
        col = jnp.tile(jnp.arange(GRID_W, dtype=F32), rows)
        inv = ROPE_THETA ** (-jnp.arange(nf, dtype=F32) / nf)
        ang = jnp.stack([row[:, None] * inv, col[:, None] * inv], axis=1)
        c = jnp.broadcast_to(jnp.cos(ang)[:, :, None, :], (n_lat, 2, 2, nf)).reshape(n_lat, head_dim)
        s = jnp.broadcast_to(jnp.sin(ang)[:, :, None, :], (n_lat, 2, 2, nf)).reshape(n_lat, head_dim) * sign
        c = jnp.concatenate([jnp.ones((n_ctx, head_dim), F32), c], axis=0)
        s = jnp.concatenate([jnp.zeros((n_ctx, head_dim), F32), s], axis=0)
        return c, s

    c64, s64 = full(B_HD)
    c32, s32 = full(A_ROPE)
    pad_a = lambda v: jnp.pad(v, ((0, 0), (A_NOPE, LANES - A_NOPE - A_ROPE)))
    return {
        "ck": jnp.tile(c64, (1, LANES // B_HD)), "sk": jnp.tile(s64, (1, LANES // B_HD)),
        "ca": pad_a(c32), "sa": pad_a(s32),
        "ctq": c64.T, "stq": s64.T, "cta": c32.T, "sta": s32.T,
    }


def _rot_ranges(base, head_dim, n_heads):
    nf = head_dim // 4
    out = []
    for hh in range(n_heads):
        b0 = base + hh * head_dim
        for a in range(2):
            out += [(b0 + a * 2 * nf + nf, b0 + a * 2 * nf + 2 * nf), (b0 + a * 2 * nf, b0 + a * 2 * nf + nf)]
    return out


def _prepare_weights(w_in, b_gate, g_norm1, g_q_a, w_q_b, g_kv_a, w_kv_b, g_qn, g_kn, w_branch, w_out, g_norm2,
                     w_up, w_conv, b_conv, w_down):
    depth = w_in.shape[0]
    cuts = [int(v) for v in np.cumsum((0,) + IN_SIZES)]
    aq, akv, akr, bq, bk, bv, cq, ck, cv, gl = ((cuts[i], cuts[i + 1]) for i in range(len(IN_SIZES)))
    cols = lambda ranges: [w_in[:, :, lo:hi] for lo, hi in ranges]
    zeros = lambda n: [jnp.zeros((depth, D_MODEL, n), F32)]
    pad_a = lambda pieces: zeros(A_NOPE) + pieces + zeros(LANES - A_NOPE - A_ROPE)
    w_std = jnp.concatenate(
        cols([akv, bk]) + cols(_rot_ranges(bk[0], B_HD, HB_KV)) + cols([ck]) + cols(_rot_ranges(ck[0], C_HD, HC_KV))
        + pad_a(cols([akr])) + pad_a(cols(_rot_ranges(akr[0], A_ROPE, 1))), axis=2)
    w_t = jnp.swapaxes(jnp.concatenate(cols([aq, akv, bq, cq, bv, cv]), axis=2), 1, 2)
    wqb = w_q_b.reshape(depth, A_Q_RANK, HA, A_NOPE + A_ROPE)
    wqb = jnp.pad(wqb, ((0, 0), (0, 0), (0, 0), (0, HEAD_PAD - A_NOPE - A_ROPE)))
    wqb = jnp.swapaxes(wqb.reshape(depth, A_Q_RANK, HA * HEAD_PAD), 1, 2)
    wkv = w_kv_b.reshape(depth, A_KV_RANK, HA, A_NOPE + A_V)
    w_k = jnp.pad(wkv[..., :A_NOPE], ((0, 0), (0, 0), (0, 0), (0, HEAD_PAD - A_NOPE)))
    w_k = w_k.reshape(depth, A_KV_RANK, HA * HEAD_PAD)
    w_v = jnp.swapaxes(wkv[..., A_NOPE:].reshape(depth, A_KV_RANK, HA * A_V), 1, 2)
    g_kn2 = jnp.tile(g_kn, (1, LANES // B_HD))
    p64, _ = _partner(B_HD)
    p128 = np.concatenate([p64 + i * B_HD for i in range(LANES // B_HD)])
    return {
        "g1": g_norm1.reshape(depth, 1, D_MODEL), "g2": g_norm2.reshape(depth, 1, D_MODEL),
        "w_std": w_std.astype(BF16), "w_t": w_t.astype(BF16), "w_qb": wqb.astype(BF16),
        "w_k": w_k.astype(BF16), "w_v": w_v.astype(BF16),
        "g_qa_col": g_q_a.reshape(depth, A_Q_RANK, 1), "g_kv_row": g_kv_a.reshape(depth, 1, A_KV_RANK),
        "g_kv_col": g_kv_a.reshape(depth, A_KV_RANK, 1), "g_qn_col": g_qn.reshape(depth, B_HD, 1),
        "g_kn_row": g_kn2.reshape(depth, 1, LANES), "g_kn_rot_row": g_kn2[:, p128].reshape(depth, 1, LANES),
        "w_gate": w_in[:, :, gl[0]:gl[1]].astype(BF16), "b_gate": b_gate.reshape(depth, 1, N_BRANCH * D_MODEL),
        "w_branch": w_branch.astype(BF16), "w_out": w_out.astype(BF16),
        "w_up": w_up.astype(BF16), "w_conv": w_conv, "b_conv": b_conv.reshape(depth, 1, 2 * D_FF),
        "w_down": w_down.astype(BF16),
    }


def kernel(x, c, ctx, c_ctx, w_mod, b_mod, g_norm1, w_in, b_gate, g_q_a, w_q_b, g_kv_a, w_kv_b, g_qn, g_kn, sink,
           w_branch, w_out, g_norm2, w_up, w_conv, b_conv, w_down, g_final):
    b, n_lat, d = x.shape
    n_ctx = ctx.shape[1]
    depth = w_mod.shape[0]
    t = TOKEN_TILE
    assert d == D_MODEL and n_ctx == t and n_lat % t == 0 and n_lat % GRID_W == 0 and b <= MOD_ROWS // 2
    assert n_lat % QUERY_TILE == 0
    assert depth >= 2
    s = n_ctx + n_lat
    n_t = s // t
    scale_a = 1.0 / math.sqrt(A_NOPE + A_ROPE)
    scale_h = 1.0 / math.sqrt(B_HD)

    cvec = jnp.zeros((MOD_ROWS, D_MODEL), F32).at[:b].set(c).at[MOD_ROWS // 2].set(c_ctx)
    mods = _modulation(cvec, w_mod, b_mod)
    tabs = _rope_tables(n_ctx, n_lat)
    wts = _prepare_weights(w_in, b_gate, g_norm1, g_q_a, w_q_b, g_kv_a, w_kv_b, g_qn, g_kn, w_branch, w_out, g_norm2,
                           w_up, w_conv, b_conv, w_down)
    gf = g_final.reshape(1, D_MODEL)
    caps = _window_caps(n_ctx, t)
    sinks =jnp.broadcast_to((sink * LOG2E).reshape(depth, HC, 1, 1), (depth, HC, 1, t)).astype(F32)

    src = (ctx, x)
    for l in range(depth):
        last = l == depth - 1
        qa, ka, va, qb, kb, vb, qc, kc, vc = _project(src, mods, l, wts, tabs, scale_a, scale_h)
        lat = dict(n_keys=s, tq=QUERY_TILE, q0=0, n_tiles=n_lat // QUERY_TILE)
        oa = _attention(qa, ka, va, None, **lat)
        ob = _attention(qb, kb, vb, None, **lat)
        oc = _window_attention(qc, kc, vc, sinks[l], caps, n_ctx=n_ctx)
        if not last:
            cx = dict(n_keys=n_ctx, tq=t, q0=n_t - 1, n_tiles=1)
            oa = _attention(qa, ka, va, None, out=oa, **cx)
            ob = _attention(qb, kb, vb, None, out=ob, **cx)
            oc = _attention(qc, kc, vc, sinks[l], out=oc, **cx)
        tiles = dict(tile0=1, n_tiles=n_t - 1) if last else dict(tile0=0, n_tiles=n_t)
        xc = _merge(src, mods, l, wts, oa, ob, oc, **tiles)
        src = (_conv_ffn(xc, mods, l, wts, gf, final_norm=last, **tiles),)
    return src[0]
```

```python
import functools
import math

import jax
import jax.numpy as jnp
import numpy as np
from jax import lax
from jax.experimental import pallas as pl
from jax.experimental.pallas import tpu as pltpu

F32 = jnp.float32
BF16 = jnp.bfloat16

D_MODEL = 1024
GRID_W = 64
EPS = 1e-6
ROPE_THETA = 10000.0
WINDOW = 128
NEG = -1e30
HA, A_NOPE, A_ROPE, A_V, A_Q_RANK, A_KV_RANK = 8, 64, 32, 64, 256, 128
HB, HB_KV, B_HD = 8, 2, 64
HC, HC_KV, C_HD = 8, 2, 64
BRANCH_W = 512
N_BRANCH = 3
D_FF = 2816
CONV_W = 3
N_MOD = 6
IN_SIZES = (A_Q_RANK, A_KV_RANK, A_ROPE, HB * B_HD, HB_KV * B_HD, HB_KV * B_HD, HC * C_HD, HC_KV * C_HD,
            HC_KV * C_HD, N_BRANCH * D_MODEL)

LANES = 128
SUBLANES = 8
BF16_ROWS = 16
TOKEN_TILE = 256
QUERY_TILE = 512
MOD_ROWS = 16
HEAD_PAD = 128
V_ROWS = A_V + BF16_ROWS
FF_CHUNK = 256
KEY_CHUNK = 256
OUTPUT_LAG = 3
WINDOW_LAG = 1
VMEM_LIMIT = 56 * 1024 * 1024
LOG2E = 1.4426950408889634


def _dot(a, b):
    return jnp.dot(a, b, preferred_element_type=F32)


def _dot_nt(a, b):
    return lax.dot_general(a, b, (((1,), (1,)), ((), ())), preferred_element_type=F32)


def _dot_tn(a, b):
    return lax.dot_general(a, b, (((0,), (0,)), ((), ())), preferred_element_type=F32)


def _sigmoid(x):
    return 1.0 / (1.0 + jnp.exp(-x))


def _modulated_norm(x, g, shift, scale):
    ms = jnp.mean(x * x, axis=-1, keepdims=True)
    return (x * lax.rsqrt(ms + EPS) * g) * (1.0 + scale) + shift


def _params(n_grid):
    return pltpu.CompilerParams(dimension_semantics=("arbitrary",) * n_grid, vmem_limit_bytes=VMEM_LIMIT)


def _full(shape):
    nd = len(shape)
    return pl.BlockSpec(shape, lambda *_: (0,) * nd)


def _layer(arr, l):
    nd = arr.ndim - 1
    return pl.BlockSpec((None,) + arr.shape[1:], lambda *_: (l,) + (0,) * nd)


def _mod_spec(l, tile0):
    return pl.BlockSpec((None, None, N_MOD, D_MODEL),
                        lambda bi, ti: (l, jnp.where(ti + tile0 == 0, MOD_ROWS // 2, bi), 0, 0))


def _token_specs(src, tile0):
    t = TOKEN_TILE
    if len(src) == 1:
        return [pl.BlockSpec((None, t, D_MODEL), lambda bi, ti: (bi, ti + tile0, 0))]
    assert tile0 == 0
    return [pl.BlockSpec((None, t, D_MODEL), lambda bi, ti: (bi, 0, 0)),
            pl.BlockSpec((None, t, D_MODEL), lambda bi, ti: (bi, jnp.maximum(ti - 1, 0), 0))]


def _q_slot(tile, n_seq_tiles):
    return (tile + n_seq_tiles - 1) % n_seq_tiles


def _token_tile(refs, is_ctx):
    if len(refs) == 1:
        return refs[0][...]
    return jnp.where(is_ctx, refs[0][...], refs[1][...])


def _mod_kernel(c_ref, w_ref, b_ref, o_ref):
    c = c_ref[...]
    a = c * _sigmoid(c)
    o_ref[...] = jnp.dot(a, w_ref[...], precision=lax.Precision.HIGHEST, preferred_element_type=F32) + b_ref[...]


def _modulation(cvec, w_mod, b_mod):
    depth = w_mod.shape[0]
    n_col = N_MOD * D_MODEL
    out = pl.pallas_call(
        _mod_kernel,
        grid=(depth, N_MOD),
        in_specs=[
            pl.BlockSpec((MOD_ROWS, D_MODEL), lambda l, j: (0, 0)),
            pl.BlockSpec((None, D_MODEL, D_MODEL), lambda l, j: (l, 0, j)),
            pl.BlockSpec((None, 1, D_MODEL), lambda l, j: (l, 0, j)),
        ],
        out_specs=pl.BlockSpec((None, MOD_ROWS, D_MODEL), lambda l, j: (l, 0, j)),
        out_shape=jax.ShapeDtypeStruct((depth, MOD_ROWS, n_col), F32),
        compiler_params=_params(2),
        name="modulation",
    )(cvec, w_mod, b_mod.reshape(depth, 1, n_col))
    return out.reshape(depth, MOD_ROWS, N_MOD, D_MODEL)


STD_AKV, STD_BK, STD_BKR, STD_CK, STD_CKR, STD_KR, STD_KRR = (i * LANES for i in range(7))
STD_COLS = 7 * LANES
T_AQ = 0
T_AKV = T_AQ + A_Q_RANK
T_BQ = T_AKV + A_KV_RANK
T_CQ = T_BQ + HB * B_HD
T_BV = T_CQ + HC * C_HD
T_CV = T_BV + HB_KV * B_HD
T_ROWS = T_CV + HC_KV * C_HD


def _swap_halves(x, nf):
    parts = []
    for a in range(2):
        base = a * 2 * nf
        parts += [x[base + nf:base + 2 * nf], x[base:base + nf]]
    return jnp.concatenate(parts, axis=0)


def _ones_rows(t):
    row = lax.broadcasted_iota(jnp.int32, (BF16_ROWS, t), 0)
    return jnp.where(row == 0, 1.0, 0.0).astype(BF16)


def _proj_kernel(*refs, n_src, n_tiles, n_flat, scale_a, scale_h):
    (mod_ref, g1_ref, wstd_ref, wt_ref, wqb_ref, wk_ref, wv_ref,
     gqa_ref, gkvr_ref, gkvc_ref, gqn_ref, gknr_ref, gknrr_ref,
     ck_ref, sk_ref, ca_ref, sa_ref, ctq_ref, stq_ref, cta_ref, sta_ref,
     qa_ref, ka_ref, va_ref, qb_ref, kb_ref, vb_ref, qc_ref, kc_ref, vc_ref, ps_scr, pt_scr) = refs[n_src:]
    t = TOKEN_TILE
    j = pl.program_id(0)

    @pl.when(j == 0)
    def _():
        ps_scr[1] = jnp.zeros(ps_scr.shape[1:], F32)
        pt_scr[1] = jnp.zeros(pt_scr.shape[1:], F32)

    def stage1(slot):
        is_ctx = jnp.minimum(j, n_flat - 1) % n_tiles == 0
        h = _modulated_norm(_token_tile(refs[:n_src], is_ctx), g1_ref[...], mod_ref[0:1, :], mod_ref[1:2, :])
        hb = h.astype(BF16)
        ps_scr[slot] = _dot(hb, wstd_ref[...])
        pt_scr[slot] = _dot_nt(wt_ref[...], hb)

    def stage2(slot):
        ps, pt = ps_scr.at[slot], pt_scr.at[slot]
        ones = _ones_rows(t)

        aq = pt[T_AQ:T_AQ + A_Q_RANK]
        aqn = aq * lax.rsqrt(jnp.mean(aq * aq, axis=0, keepdims=True) + EPS) * gqa_ref[...]
        qt = _dot(wqb_ref[...], aqn.astype(BF16))
        cta, sta = cta_ref[...], sta_ref[...]
        qs = scale_a * LOG2E
        for hh in range(HA):
            base = hh * HEAD_PAD
            qa_ref[hh, 0:A_NOPE, :] = (qt[base:base + A_NOPE] * qs).astype(BF16)
            r = qt[base + A_NOPE:base + A_NOPE + A_ROPE]
            rr = r * cta + _swap_halves(r, A_ROPE // 4) * sta
            qa_ref[hh, A_NOPE:A_NOPE + A_ROPE, :] = (rr * qs).astype(BF16)
            qa_ref[hh, A_NOPE + A_ROPE:HEAD_PAD, :] = jnp.zeros((HEAD_PAD - A_NOPE - A_ROPE, t), BF16)

        akv = ps[:, STD_AKV:STD_AKV + LANES]
        akvn = akv * lax.rsqrt(jnp.mean(akv * akv, axis=-1, keepdims=True) + EPS) * gkvr_ref[...]
        kn = _dot(akvn.astype(BF16), wk_ref[...])
        kr = ps[:, STD_KR:STD_KR + LANES] * ca_ref[...] + ps[:, STD_KRR:STD_KRR + LANES] * sa_ref[...]
        for hh in range(HA):
            ka_ref[hh] = (kn[:, hh * HEAD_PAD:(hh + 1) * HEAD_PAD] + kr).astype(BF16)

        akvt = pt[T_AKV:T_AKV + A_KV_RANK]
        akvtn = akvt * lax.rsqrt(jnp.mean(akvt * akvt, axis=0, keepdims=True) + EPS) * gkvc_ref[...]
        vt = _dot(wv_ref[...], akvtn.astype(BF16))
        for hh in range(HA):
            va_ref[hh, 0:A_V, :] = vt[hh * A_V:(hh + 1) * A_V].astype(BF16)
            va_ref[hh, A_V:V_ROWS, :] = ones

        ctq, stq = ctq_ref[...], stq_ref[...]
        ck, sk = ck_ref[...], sk_ref[...]
        qsh = scale_h * LOG2E
        lane = lax.broadcasted_iota(jnp.int32, (t, LANES), 1)
        first = lane < B_HD

        for hh in range(HB):
            blk = pt[T_BQ + hh * B_HD:T_BQ + (hh + 1) * B_HD]
            y = blk * lax.rsqrt(jnp.mean(blk * blk, axis=0, keepdims=True) + EPS) * gqn_ref[...]
            qb_ref[hh] = ((y * ctq + _swap_halves(y, B_HD // 4) * stq) * qsh).astype(BF16)
            blk = pt[T_CQ + hh * C_HD:T_CQ + (hh + 1) * C_HD]
            qc_ref[hh] = ((blk * ctq + _swap_halves(blk, C_HD // 4) * stq) * qsh).astype(BF16)

        bk = ps[:, STD_BK:STD_BK + LANES]
        sq = bk * bk
        s0 = jnp.sum(jnp.where(first, sq, 0.0), axis=-1, keepdims=True)
        s1 = jnp.sum(jnp.where(first, 0.0, sq), axis=-1, keepdims=True)
        rk = lax.rsqrt(jnp.where(first, s0, s1) * (1.0 / B_HD) + EPS)
        kb = (bk * rk * gknr_ref[...]) * ck + (ps[:, STD_BKR:STD_BKR + LANES] * rk * gknrr_ref[...]) * sk
        kc = ps[:, STD_CK:STD_CK + LANES] * ck + ps[:, STD_CKR:STD_CKR + LANES] * sk
        for g in range(HB_KV):
            kb_ref[g] = kb[:, g * B_HD:(g + 1) * B_HD].astype(BF16)
            kc_ref[g] = kc[:, g * C_HD:(g + 1) * C_HD].astype(BF16)
            vb_ref[g, 0:B_HD, :] = pt[T_BV + g * B_HD:T_BV + (g + 1) * B_HD].astype(BF16)
            vb_ref[g, B_HD:V_ROWS, :] = ones
            vc_ref[g, 0:C_HD, :] = pt[T_CV + g * C_HD:T_CV + (g + 1) * C_HD].astype(BF16)
            vc_ref[g, C_HD:V_ROWS, :] = ones

    @pl.when(j % 2 == 0)
    def _():
        stage1(0)
        stage2(1)

    @pl.when(j % 2 == 1)
    def _():
        stage1(1)
        stage2(0)


def _project(src, mods, l, wts, tabs, scale_a, scale_h):
    b = src[-1].shape[0]
    s = sum(a.shape[1] for a in src)
    t = TOKEN_TILE
    n_t = s // t
    n_flat = b * n_t
    tile1 = lambda j: jnp.minimum(j, n_flat - 1)
    tile2 = lambda j: jnp.maximum(j - 1, 0)
    if len(src) == 1:
        tok_specs = [pl.BlockSpec((None, t, D_MODEL), lambda j: (tile1(j) // n_t, tile1(j) % n_t, 0))]
    else:
        tok_specs = [pl.BlockSpec((None, t, D_MODEL), lambda j: (tile1(j) // n_t, 0, 0)),
                     pl.BlockSpec((None, t, D_MODEL),
                                  lambda j: (tile1(j) // n_t, jnp.maximum(tile1(j) % n_t - 1, 0), 0))]
    mod_spec = pl.BlockSpec(
        (None, None, N_MOD, D_MODEL),
        lambda j: (l, jnp.where(tile1(j) % n_t == 0, MOD_ROWS // 2, tile1(j) // n_t), 0, 0))
    tok2 = lambda j: (tile2(j) % n_t, 0)
    feat2 = lambda j: (0, tile2(j) % n_t)
    names = ("g1", "w_std", "w_t", "w_qb", "w_k", "w_v", "g_qa_col", "g_kv_row", "g_kv_col", "g_qn_col", "g_kn_row",
             "g_kn_rot_row")
    in_specs = tok_specs + [mod_spec] + [_layer(wts[n], l) for n in names] + [
        pl.BlockSpec((t, LANES), tok2), pl.BlockSpec((t, LANES), tok2),
        pl.BlockSpec((t, LANES), tok2), pl.BlockSpec((t, LANES), tok2),
        pl.BlockSpec((B_HD, t), feat2), pl.BlockSpec((B_HD, t), feat2),
        pl.BlockSpec((A_ROPE, t), feat2), pl.BlockSpec((A_ROPE, t), feat2),
    ]
    qspec = lambda heads, rows: pl.BlockSpec((None, heads, rows, t),
                                             lambda j: (tile2(j) // n_t, 0, 0, _q_slot(tile2(j) % n_t, n_t)))
    vspec = lambda heads, rows: pl.BlockSpec((None, heads, rows, t),
                                             lambda j: (tile2(j) // n_t, 0, 0, tile2(j) % n_t))
    kspec = lambda heads, cols: pl.BlockSpec((None, heads, t, cols),
                                             lambda j: (tile2(j) // n_t, 0, tile2(j) % n_t, 0))
    out_specs = [
        qspec(HA, HEAD_PAD), kspec(HA, HEAD_PAD), vspec(HA, V_ROWS),
        qspec(HB, B_HD), kspec(HB_KV, B_HD), vspec(HB_KV, V_ROWS),
        qspec(HC, C_HD), kspec(HC_KV, C_HD), vspec(HC_KV, V_ROWS),
    ]
    sd = jax.ShapeDtypeStruct
    out_shape = [
        sd((b, HA, HEAD_PAD, s), BF16), sd((b, HA, s, HEAD_PAD), BF16), sd((b, HA, V_ROWS, s), BF16),
        sd((b, HB, B_HD, s), BF16), sd((b, HB_KV, s, B_HD), BF16), sd((b, HB_KV, V_ROWS, s), BF16),
        sd((b, HC, C_HD, s), BF16), sd((b, HC_KV, s, C_HD), BF16), sd((b, HC_KV, V_ROWS, s), BF16),
    ]
    return pl.pallas_call(
        functools.partial(_proj_kernel, n_src=len(src), n_tiles=n_t, n_flat=n_flat, scale_a=scale_a,
                          scale_h=scale_h),
        grid=(n_flat + 1,),
        in_specs=in_specs,
        out_specs=out_specs,
        out_shape=out_shape,
        scratch_shapes=[pltpu.VMEM((2, t, STD_COLS), F32), pltpu.VMEM((2, T_ROWS, t), F32)],
        compiler_params=_params(1),
        name="projection",
    )(*src, mods, *(wts[n] for n in names),
      tabs["ck"], tabs["sk"], tabs["ca"], tabs["sa"], tabs["ctq"], tabs["stq"], tabs["cta"], tabs["sta"])


def _attn_kernel(*refs, n_heads, group, has_sink):
    if has_sink:
        q_ref, qn_ref, k_ref, v_ref, sink_ref, o_ref, s_scr, m_scr = refs
    else:
        q_ref, qn_ref, k_ref, v_ref, o_ref, s_scr, m_scr = refs
    n_keys, tq = s_scr.shape[1], s_scr.shape[2]
    chunk = min(KEY_CHUNK, n_keys)
    n_chunks = n_keys // chunk
    lag = min(OUTPUT_LAG, n_chunks - 1)

    def score_chunk(head, c, mrun):
        q = qn_ref[0] if head == n_heads else q_ref[head]
        rows = slice(c * chunk, (c + 1) * chunk)
        s = _dot(k_ref[(head % n_heads) // group, rows, :], q)
        s_scr[head % 2, rows, :] = s
        part = jnp.max(s.reshape(chunk // SUBLANES, SUBLANES, tq), axis=0)
        return part if mrun is None else jnp.maximum(mrun, part)

    def output_chunk(head, c, m, acc):
        rows = slice(c * chunk, (c + 1) * chunk)
        p = jnp.exp2(s_scr[head % 2, rows, :] - m).astype(BF16)
        part = _dot(v_ref[head // group, :, rows], p)
        return part if acc is None else acc + part

    @pl.when(pl.program_id(1) == 0)
    def _():
        mrun = None
        for c in range(n_chunks):
            mrun = score_chunk(0, c, mrun)
        m_scr[...] = jnp.max(mrun, axis=0, keepdims=True)

    maxes = {0: m_scr[...]}
    mrun = acc = m = snk = None
    for g in range(n_heads * n_chunks + lag):
        if g < n_heads * n_chunks:
            hs, cs = divmod(g, n_chunks)
            mrun = score_chunk(hs + 1, cs, mrun)
            if cs == n_chunks - 1:
                maxes[hs + 1] = jnp.max(mrun, axis=0, keepdims=True)
                mrun = None
        if g >= lag:
            ho, co = divmod(g - lag, n_chunks)
            if co == 0:
                m = maxes.pop(ho)
                if has_sink:
                    snk = sink_ref[ho]
                    m = jnp.maximum(m, snk)
                acc = None
            acc = output_chunk(ho, co, m, acc)
            if co == n_chunks - 1:
                denom = acc[A_V:A_V + 1]
                if has_sink:
                    denom = denom + jnp.exp2(snk - m)
                o_ref[ho * A_V:(ho + 1) * A_V, :] = (acc[0:A_V] * (1.0 / denom)).astype(BF16)
    m_scr[...] = maxes[n_heads]


def _attention(qt, k, vt, sink2, *, n_keys, tq, q0, n_tiles, out=None):
    b, n_heads, dk, s = qt.shape
    hk = k.shape[1]
    in_specs = [
        pl.BlockSpec((None, n_heads, dk, tq), lambda bi, ti: (bi, 0, 0, ti + q0)),
        pl.BlockSpec((None, 1, dk, tq), lambda bi, ti: (bi, 0, 0, jnp.minimum(ti + 1, n_tiles - 1) + q0)),
        pl.BlockSpec((None, hk, n_keys, k.shape[3]), lambda bi, ti: (bi, 0, 0, 0)),
        pl.BlockSpec((None, hk, V_ROWS, n_keys), lambda bi, ti: (bi, 0, 0, 0)),
    ]
    args = [qt, qt, k, vt]
    if sink2 is not None:
        in_specs.append(_full(sink2.shape))
        args.append(sink2)
    aliases = {}
    if out is not None:
        in_specs.append(pl.BlockSpec(memory_space=pl.ANY))
        args.append(out)
        aliases = {len(args) - 1: 0}
    kern = functools.partial(_attn_kernel, n_heads=n_heads, group=n_heads // hk, has_sink=sink2 is not None)
    if out is not None:
        kern = _drop_last_input(kern, len(args))
    return pl.pallas_call(
        kern,
        grid=(b, n_tiles),
        in_specs=in_specs,
        out_specs=pl.BlockSpec((None, n_heads * A_V, tq), lambda bi, ti: (bi, 0, ti + q0)),
        out_shape=jax.ShapeDtypeStruct((b, n_heads * A_V, s), BF16),
        input_output_aliases=aliases,
        scratch_shapes=[pltpu.VMEM((2, n_keys, tq), F32), pltpu.VMEM((1, tq), F32)],
        compiler_params=_params(2),
        name="attention",
    )(*args)


def _drop_last_input(kern, n_in):
    def wrapped(*refs):
        return kern(*refs[:n_in - 1], *refs[n_in:])
    return wrapped


def _window_caps(n_ctx, tq):
    n_band = WINDOW + tq
    n_keys = n_ctx + n_band + WINDOW
    row = jnp.arange(n_keys, dtype=jnp.int32)[:, None]
    col = jnp.arange(tq, dtype=jnp.int32)[None, :]
    rel = row - n_ctx - WINDOW - col
    in_band = (rel >= -WINDOW) & (rel <= WINDOW)
    caps = []
    for first in (False, True):
        for last in (False, True):
            ok_lo = (row >= n_ctx + WINDOW) | (not first)
            ok_hi = (row < n_ctx + n_band) | (not last)
            valid = (row < n_ctx) | (in_band & ok_lo & ok_hi)
            caps.append(jnp.where(valid, jnp.inf, NEG).astype(F32))
    return jnp.stack(caps)


def _window_kernel(q_ref, k_ref, v_ref, sink_ref, cap_ref, o_ref, k_scr, v_scr, s_scr, *, n_ctx, n_tiles):
    tq = q_ref.shape[2]
    ti = pl.program_id(1)
    start = pl.multiple_of(n_ctx + ti * tq, LANES)
    lo = pl.multiple_of(start - WINDOW, LANES)
    nxt = pl.multiple_of(jnp.minimum(start + tq, n_ctx + (n_tiles - 1) * tq + tq - WINDOW), LANES)
    n_band = WINDOW + tq
    n_keys = n_ctx + n_band + WINDOW

    for g in range(HC_KV):
        k_scr[g, 0:n_ctx, :] = k_ref[g, 0:n_ctx, :]
        k_scr[g, n_ctx:n_ctx + n_band, :] = k_ref[g, pl.ds(lo, n_band), :]
        k_scr[g, n_ctx + n_band:n_keys, :] = k_ref[g, pl.ds(nxt, WINDOW), :]
        v_scr[g, :, 0:n_ctx] = v_ref[g, :, 0:n_ctx]
        v_scr[g, :, n_ctx:n_ctx + n_band] = v_ref[g, :, pl.ds(lo, n_band)]
        v_scr[g, :, n_ctx + n_band:n_keys] = v_ref[g, :, pl.ds(nxt, WINDOW)]

    group = HC // HC_KV
    chunk = KEY_CHUNK
    n_chunks = n_keys // chunk

    def score_chunk(hh, c, mrun):
        rows = slice(c * chunk, (c + 1) * chunk)
        s = jnp.minimum(_dot(k_scr[hh // group, rows, :], q_ref[hh]), cap_ref[rows, :])
        s_scr[hh % 2, rows, :] = s
        part = jnp.max(s.reshape(chunk // SUBLANES, SUBLANES, tq), axis=0)
        return part if mrun is None else jnp.maximum(mrun, part)

    def output_chunk(hh, c, m, acc):
        rows = slice(c * chunk, (c + 1) * chunk)
        p = jnp.exp2(s_scr[hh % 2, rows, :] - m).astype(BF16)
        part = _dot(v_scr[hh // group, :, rows], p)
        return part if acc is None else acc + part

    lag = n_chunks + WINDOW_LAG
    maxes = {}
    mrun = acc = m = snk = None
    for g in range(HC * n_chunks + lag):
        if g < HC * n_chunks:
            hs, cs = divmod(g, n_chunks)
            mrun = score_chunk(hs, cs, mrun)
            if cs == n_chunks - 1:
                maxes[hs] = jnp.max(mrun, axis=0, keepdims=True)
                mrun = None
        if g >= lag:
            ho, co = divmod(g - lag, n_chunks)
            if co == 0:
                snk = sink_ref[ho]
                m = jnp.maximum(maxes.pop(ho), snk)
                acc = None
            acc = output_chunk(ho, co, m, acc)
            if co == n_chunks - 1:
                denom = acc[C_HD:C_HD + 1] + jnp.exp2(snk - m)
                o_ref[ho * C_HD:(ho + 1) * C_HD, :] = (acc[0:C_HD] * (1.0 / denom)).astype(BF16)


def _window_attention(qt, k, vt, sink2, caps, *, n_ctx):
    b, n_heads, dk, s = qt.shape
    hk = k.shape[1]
    t = TOKEN_TILE
    n_tiles = (s - n_ctx) // t
    n_win_keys = n_ctx + WINDOW + t + WINDOW
    return pl.pallas_call(
        functools.partial(_window_kernel, n_ctx=n_ctx, n_tiles=n_tiles),
        grid=(b, n_tiles),
        in_specs=[
            pl.BlockSpec((None, n_heads, dk, t), lambda bi, ti: (bi, 0, 0, ti)),
            pl.BlockSpec((None, hk, s, dk), lambda bi, ti: (bi, 0, 0, 0)),
            pl.BlockSpec((None, hk, V_ROWS, s), lambda bi, ti: (bi, 0, 0, 0)),
            _full(sink2.shape),
            pl.BlockSpec((None, n_win_keys, t),
                         lambda bi, ti: (2 * (ti == 0).astype(jnp.int32) + (ti == n_tiles - 1).astype(jnp.int32), 0, 0)),
        ],
        out_specs=pl.BlockSpec((None, n_heads * C_HD, t), lambda bi, ti: (bi, 0, ti)),
        out_shape=jax.ShapeDtypeStruct((b, n_heads * C_HD, s), BF16),
        scratch_shapes=[
            pltpu.VMEM((hk, n_win_keys, dk), BF16), pltpu.VMEM((hk, V_ROWS, n_win_keys), BF16),
            pltpu.VMEM((2, n_win_keys, t), F32),
        ],
        compiler_params=_params(2),
        name="window_attention",
    )(qt, k, vt, sink2, caps)


def _merge_kernel(*refs, n_src):
    mod_ref, g1_ref, oa_ref, ob_ref, oc_ref, wg_ref, bg_ref, wbr_ref, wout_ref, xo_ref = refs[n_src:]
    x = _token_tile(refs[:n_src], pl.program_id(1) == 0)
    h = _modulated_norm(x, g1_ref[...], mod_ref[0:1, :], mod_ref[1:2, :])
    hb = h.astype(BF16)
    y = None
    for i, o_ref in enumerate((oa_ref, ob_ref, oc_ref)):
        cols = slice(i * D_MODEL, (i + 1) * D_MODEL)
        gate = _sigmoid(_dot(hb, wg_ref[:, cols]) + bg_ref[:, cols])
        term = gate * _dot_tn(o_ref[...], wbr_ref[i])
        y = term if y is None else y + term
    z = _dot(y.astype(BF16), wout_ref[...])
    xo_ref[...] = x + mod_ref[2:3, :] * z


def _merge(src, mods, l, wts, oa, ob, oc, *, tile0, n_tiles):
    b = src[-1].shape[0]
    s = sum(a.shape[1] for a in src)
    t = TOKEN_TILE
    feat = lambda bi, ti: (bi, 0, _q_slot(ti + tile0, s // t))
    names = ("w_gate", "b_gate", "w_branch", "w_out")
    return pl.pallas_call(
        functools.partial(_merge_kernel, n_src=len(src)),
        grid=(b, n_tiles),
        in_specs=_token_specs(src, tile0) + [_mod_spec(l, tile0), _layer(wts["g1"], l)] + [
            pl.BlockSpec((None, BRANCH_W, t), feat),
            pl.BlockSpec((None, BRANCH_W, t), feat),
            pl.BlockSpec((None, BRANCH_W, t), feat),
        ] + [_layer(wts[n], l) for n in names],
        out_specs=pl.BlockSpec((None, t, D_MODEL), lambda bi, ti: (bi, ti + tile0, 0)),
        out_shape=jax.ShapeDtypeStruct((b, s, D_MODEL), F32),
        input_output_aliases={0: 0} if len(src) == 1 else {},
        compiler_params=_params(2),
        name="merge",
    )(*src, mods, wts["g1"], oa, ob, oc, *(wts[n] for n in names))


def _ffn_kernel(xp_ref, x_ref, xn_ref, mod_ref, g2_ref, wup_ref, wconv_ref, bconv_ref, wdown_ref, gf_ref,
                o_ref, act_ref, *, tile0, n_seq_tiles, final_norm):
    t = x_ref.shape[0]
    halo = SUBLANES
    ta = pl.program_id(1) + tile0
    x = x_ref[...]
    xa = jnp.concatenate([xp_ref[...], x, xn_ref[...]], axis=0)
    h = _modulated_norm(xa, g2_ref[...], mod_ref[3:4, :], mod_ref[4:5, :])
    row = lax.broadcasted_iota(jnp.int32, (t + 2 * halo, 1), 0)
    keep_prev = (ta > 1).astype(F32)
    keep_next = ((ta > 0) & (ta < n_seq_tiles - 1)).astype(F32)
    keep = jnp.where(row < halo, keep_prev, jnp.where(row >= t + halo, keep_next, 1.0))
    hb = (h * keep).astype(BF16)

    def conv(u, cols):
        return (u[halo - 1:halo - 1 + t] * wconv_ref[0:1, cols] + u[halo:halo + t] * wconv_ref[1:2, cols]
                + u[halo + 1:halo + 1 + t] * wconv_ref[2:3, cols] + bconv_ref[:, cols])

    for c in range(D_FF // FF_CHUNK):
        ca = slice(c * FF_CHUNK, (c + 1) * FF_CHUNK)
        cg = slice(D_FF + c * FF_CHUNK, D_FF + (c + 1) * FF_CHUNK)
        a = conv(_dot(hb, wup_ref[:, ca]), ca)
        gv = conv(_dot(hb, wup_ref[:, cg]), cg)
        act_ref[:, ca] = (a * _sigmoid(a) * gv).astype(BF16)
    y = x + mod_ref[5:6, :] * _dot(act_ref[...], wdown_ref[...])
    if final_norm:
        y = y * lax.rsqrt(jnp.mean(y * y, axis=-1, keepdims=True) + EPS) * gf_ref[...]
    o_ref[...] = y


def _conv_ffn(xc, mods, l, wts, g_final, *, tile0, n_tiles, final_norm):
    b, s, _ = xc.shape
    t = TOKEN_TILE
    n_seq_tiles = s // t
    per = t // SUBLANES
    last_blk = s // SUBLANES - 1
    out_rows = n_tiles * t if final_norm else s
    out_tile0 = 0 if final_norm else tile0
    return pl.pallas_call(
        functools.partial(_ffn_kernel, tile0=tile0, n_seq_tiles=n_seq_tiles, final_norm=final_norm),
        grid=(b, n_tiles),
        in_specs=[
            pl.BlockSpec((None, SUBLANES, D_MODEL), lambda bi, ti: (bi, jnp.maximum((ti + tile0) * per - 1, 0), 0)),
            pl.BlockSpec((None, t, D_MODEL), lambda bi, ti: (bi, ti + tile0, 0)),
            pl.BlockSpec((None, SUBLANES, D_MODEL),
                         lambda bi, ti: (bi, jnp.minimum((ti + tile0 + 1) * per, last_blk), 0)),
            _mod_spec(l, tile0),
            _layer(wts["g2"], l), _layer(wts["w_up"], l), _layer(wts["w_conv"], l), _layer(wts["b_conv"], l),
            _layer(wts["w_down"], l),
            _full((1, D_MODEL)),
        ],
        out_specs=pl.BlockSpec((None, t, D_MODEL), lambda bi, ti: (bi, ti + out_tile0, 0)),
        out_shape=jax.ShapeDtypeStruct((b, out_rows, D_MODEL), F32),
        scratch_shapes=[pltpu.VMEM((t, D_FF), BF16)],
        compiler_params=_params(2),
        name="conv_ffn",
    )(xc, xc, xc, mods, wts["g2"], wts["w_up"], wts["w_conv"], wts["b_conv"], wts["w_down"], g_final)


def _partner(head_dim):
    nf = head_dim // 4
    d = np.arange(head_dim)
    a, half, f = d // (2 * nf), (d % (2 * nf)) // nf, d % nf
    return a * 2 * nf + (1 - half) * nf + f, np.where(half == 0, -1.0, 1.0).astype(np.float32)


def _rope_tables(n_ctx, n_lat):
    def full(head_dim):
        nf = head_dim // 4
        _, sign = _partner(head_dim)
        rows = n_lat // GRID_W
        row = jnp.repeat(jnp.arange(rows, dtype=F32), GRID_W)
        col = jnp.tile(jnp.arange(GRID_W, dtype=F32), rows)
        inv = ROPE_THETA ** (-jnp.arange(nf, dtype=F32) / nf)
        ang = jnp.stack([row[:, None] * inv, col[:, None] * inv], axis=1)
        c = jnp.broadcast_to(jnp.cos(ang)[:, :, None, :], (n_lat, 2, 2, nf)).reshape(n_lat, head_dim)
        s = jnp.broadcast_to(jnp.sin(ang)[:, :, None, :], (n_lat, 2, 2, nf)).reshape(n_lat, head_dim) * sign
        c = jnp.concatenate([jnp.ones((n_ctx, head_dim), F32), c], axis=0)
        s = jnp.concatenate([jnp.zeros((n_ctx, head_dim), F32), s], axis=0)
        return c, s

    c64, s64 = full(B_HD)
    c32, s32 = full(A_ROPE)
    pad_a = lambda v: jnp.pad(v, ((0, 0), (A_NOPE, LANES - A_NOPE - A_ROPE)))
    return {
        "ck": jnp.tile(c64, (1, LANES // B_HD)), "sk": jnp.tile(s64, (1, LANES // B_HD)),
        "ca": pad_a(c32), "sa": pad_a(s32),
        "ctq": c64.T, "stq": s64.T, "cta": c32.T, "sta": s32.T,
    }


def _rot_ranges(base, head_dim, n_heads):
    nf = head_dim // 4
    out = []
    for hh in range(n_heads):
        b0 = base + hh * head_dim
        for a in range(2):
            out += [(b0 + a * 2 * nf + nf, b0 + a * 2 * nf + 2 * nf), (b0 + a * 2 * nf, b0 + a * 2 * nf + nf)]
    return out


def _prepare_weights(w_in, b_gate, g_norm1, g_q_a, w_q_b, g_kv_a, w_kv_b, g_qn, g_kn, w_branch, w_out, g_norm2,
                     w_up, w_conv, b_conv, w_down):
    depth = w_in.shape[0]
    cuts = [int(v) for v in np.cumsum((0,) + IN_SIZES)]
    aq, akv, akr, bq, bk, bv, cq, ck, cv, gl = ((cuts[i], cuts[i + 1]) for i in range(len(IN_SIZES)))
    cols = lambda ranges: [w_in[:, :, lo:hi] for lo, hi in ranges]
    zeros = lambda n: [jnp.zeros((depth, D_MODEL, n), F32)]
    pad_a = lambda pieces: zeros(A_NOPE) + pieces + zeros(LANES - A_NOPE - A_ROPE)
    w_std = jnp.concatenate(
        cols([akv, bk]) + cols(_rot_ranges(bk[0], B_HD, HB_KV)) + cols([ck]) + cols(_rot_ranges(ck[0], C_HD, HC_KV))
        + pad_a(cols([akr])) + pad_a(cols(_rot_ranges(akr[0], A_ROPE, 1))), axis=2)
    w_t = jnp.swapaxes(jnp.concatenate(cols([aq, akv, bq, cq, bv, cv]), axis=2), 1, 2)
    wqb = w_q_b.reshape(depth, A_Q_RANK, HA, A_NOPE + A_ROPE)
    wqb = jnp.pad(wqb, ((0, 0), (0, 0), (0, 0), (0, HEAD_PAD - A_NOPE - A_ROPE)))
    wqb = jnp.swapaxes(wqb.reshape(depth, A_Q_RANK, HA * HEAD_PAD), 1, 2)
    wkv = w_kv_b.reshape(depth, A_KV_RANK, HA, A_NOPE + A_V)
    w_k = jnp.pad(wkv[..., :A_NOPE], ((0, 0), (0, 0), (0, 0), (0, HEAD_PAD - A_NOPE)))
    w_k = w_k.reshape(depth, A_KV_RANK, HA * HEAD_PAD)
    w_v = jnp.swapaxes(wkv[..., A_NOPE:].reshape(depth, A_KV_RANK, HA * A_V), 1, 2)
    g_kn2 = jnp.tile(g_kn, (1, LANES // B_HD))
    p64, _ = _partner(B_HD)
    p128 = np.concatenate([p64 + i * B_HD for i in range(LANES // B_HD)])
    return {
        "g1": g_norm1.reshape(depth, 1, D_MODEL), "g2": g_norm2.reshape(depth, 1, D_MODEL),
        "w_std": w_std.astype(BF16), "w_t": w_t.astype(BF16), "w_qb": wqb.astype(BF16),
        "w_k": w_k.astype(BF16), "w_v": w_v.astype(BF16),
        "g_qa_col": g_q_a.reshape(depth, A_Q_RANK, 1), "g_kv_row": g_kv_a.reshape(depth, 1, A_KV_RANK),
        "g_kv_col": g_kv_a.reshape(depth, A_KV_RANK, 1), "g_qn_col": g_qn.reshape(depth, B_HD, 1),
        "g_kn_row": g_kn2.reshape(depth, 1, LANES), "g_kn_rot_row": g_kn2[:, p128].reshape(depth, 1, LANES),
        "w_gate": w_in[:, :, gl[0]:gl[1]].astype(BF16), "b_gate": b_gate.reshape(depth, 1, N_BRANCH * D_MODEL),
        "w_branch": w_branch.astype(BF16), "w_out": w_out.astype(BF16),
        "w_up": w_up.astype(BF16), "w_conv": w_conv, "b_conv": b_conv.reshape(depth, 1, 2 * D_FF),
        "w_down": w_down.astype(BF16),
    }


def kernel(x, c, ctx, c_ctx, w_mod, b_mod, g_norm1, w_in, b_gate, g_q_a, w_q_b, g_kv_a, w_kv_b, g_qn, g_kn, sink,
           w_branch, w_out, g_norm2, w_up, w_conv, b_conv, w_down, g_final):
    b, n_lat, d = x.shape
    n_ctx = ctx.shape[1]
    depth = w_mod.shape[0]
    t = TOKEN_TILE
    assert d == D_MODEL and n_ctx == t and n_lat % t == 0 and n_lat % GRID_W == 0 and b <= MOD_ROWS // 2
    assert n_lat % QUERY_TILE == 0
    assert depth >= 2
    s = n_ctx + n_lat
    n_t = s // t
    scale_a = 1.0 / math.sqrt(A_NOPE + A_ROPE)
    scale_h = 1.0 / math.sqrt(B_HD)

    cvec = jnp.zeros((MOD_ROWS, D_MODEL), F32).at[:b].set(c).at[MOD_ROWS // 2].set(c_ctx)
    mods = _modulation(cvec, w_mod, b_mod)
    tabs = _rope_tables(n_ctx, n_lat)
    wts = _prepare_weights(w_in, b_gate, g_norm1, g_q_a, w_q_b, g_kv_a, w_kv_b, g_qn, g_kn, w_branch, w_out, g_norm2,
                           w_up, w_conv, b_conv, w_down)
    gf = g_final.reshape(1, D_MODEL)
    caps = _window_caps(n_ctx, t)
    sinks = jnp.broadcast_to((sink * LOG2E).reshape(depth, HC, 1, 1), (depth, HC, 1, t)).astype(F32)

    src = (ctx, x)
    for l in range(depth):
        last = l == depth - 1
        qa, ka, va, qb, kb, vb, qc, kc, vc = _project(src, mods, l, wts, tabs, scale_a, scale_h)
        lat = dict(n_keys=s, tq=QUERY_TILE, q0=0, n_tiles=n_lat // QUERY_TILE)
        oa = _attention(qa, ka, va, None, **lat)
        ob = _attention(qb, kb, vb, None, **lat)
        oc = _window_attention(qc, kc, vc, sinks[l], caps, n_ctx=n_ctx)
        if not last:
            cx = dict(n_keys=n_ctx, tq=t, q0=n_t - 1, n_tiles=1)
            oa = _attention(qa, ka, va, None, out=oa, **cx)
            ob = _attention(qb, kb, vb, None, out=ob, **cx)
            oc = _attention(qc, kc, vc, sinks[l], out=oc, **cx)
        tiles = dict(tile0=1, n_tiles=n_t - 1) if last else dict(tile0=0, n_tiles=n_t)
        xc = _merge(src, mods, l, wts, oa, ob, oc, **tiles)
        src = (_conv_ffn(xc, mods, l, wts, gf, final_norm=last, **tiles),)
    return src[0]
```

```python
import functools
import math

import jax
import jax.numpy as jnp
import numpy as np
from jax import lax
from jax.experimental import pallas as pl
from jax.experimental.pallas import tpu as pltpu

F32 = jnp.float32
BF16 = jnp.bfloat16

D_MODEL = 1024
GRID_W = 64
EPS = 1e-6
ROPE_THETA = 10000.0
WINDOW = 128
NEG = -1e30
HA, A_NOPE, A_ROPE, A_V, A_Q_RANK, A_KV_RANK = 8, 64, 32, 64, 256, 128
HB, HB_KV, B_HD = 8, 2, 64
HC, HC_KV, C_HD = 8, 2, 64
BRANCH_W = 512
N_BRANCH = 3
D_FF = 2816
CONV_W = 3
N_MOD = 6
IN_SIZES = (A_Q_RANK, A_KV_RANK, A_ROPE, HB * B_HD, HB_KV * B_HD, HB_KV * B_HD, HC * C_HD, HC_KV * C_HD,
            HC_KV * C_HD, N_BRANCH * D_MODEL)

LANES = 128
SUBLANES = 8
BF16_ROWS = 16
TOKEN_TILE = 256
QUERY_TILE = 512
MOD_ROWS = 16
HEAD_PAD = 128
V_ROWS = A_V + BF16_ROWS
FF_CHUNK = 256
KEY_CHUNK = 256
OUTPUT_LAG = 2
WINDOW_LAG = 2
VMEM_LIMIT = 56 * 1024 * 1024
LOG2E = 1.4426950408889634


def _dot(a, b):
    return jnp.dot(a, b, preferred_element_type=F32)


def _dot_nt(a, b):
    return lax.dot_general(a, b, (((1,), (1,)), ((), ())), preferred_element_type=F32)


def _dot_tn(a, b):
    return lax.dot_general(a, b, (((0,), (0,)), ((), ())), preferred_element_type=F32)


def _sigmoid(x):
    return 1.0 / (1.0 + jnp.exp(-x))


def _modulated_norm(x, g, shift, scale):
    ms = jnp.mean(x * x, axis=-1, keepdims=True)
    return (x * lax.rsqrt(ms + EPS) * g) * (1.0 + scale) + shift


def _params(n_grid):
    return pltpu.CompilerParams(dimension_semantics=("arbitrary",) * n_grid, vmem_limit_bytes=VMEM_LIMIT)


def _full(shape):
    nd = len(shape)
    return pl.BlockSpec(shape, lambda *_: (0,) * nd)


def _layer(arr, l):
    nd = arr.ndim - 1
    return pl.BlockSpec((None,) + arr.shape[1:], lambda *_: (l,) + (0,) * nd)


def _mod_spec(l, tile0):
    return pl.BlockSpec((None, None, N_MOD, D_MODEL),
                        lambda bi, ti: (l, jnp.where(ti + tile0 == 0, MOD_ROWS // 2, bi), 0, 0))


def _token_specs(src, tile0):
    t = TOKEN_TILE
    if len(src) == 1:
        return [pl.BlockSpec((None, t, D_MODEL), lambda bi, ti: (bi, ti + tile0, 0))]
    assert tile0 == 0
    return [pl.BlockSpec((None, t, D_MODEL), lambda bi, ti: (bi, 0, 0)),
            pl.BlockSpec((None, t, D_MODEL), lambda bi, ti: (bi, jnp.maximum(ti - 1, 0), 0))]


def _q_slot(tile, n_seq_tiles):
    return (tile + n_seq_tiles - 1) % n_seq_tiles


def _token_tile(refs, is_ctx):
    if len(refs) == 1:
        return refs[0][...]
    return jnp.where(is_ctx, refs[0][...], refs[1][...])


def _mod_kernel(c_ref, w_ref, b_ref, o_ref):
    c = c_ref[...]
    a = c * _sigmoid(c)
    o_ref[...] = jnp.dot(a, w_ref[...], precision=lax.Precision.HIGHEST, preferred_element_type=F32) + b_ref[...]


def _modulation(cvec, w_mod, b_mod):
    depth = w_mod.shape[0]
    n_col = N_MOD * D_MODEL
    out = pl.pallas_call(
        _mod_kernel,
        grid=(depth, N_MOD),
        in_specs=[
            pl.BlockSpec((MOD_ROWS, D_MODEL), lambda l, j: (0, 0)),
            pl.BlockSpec((None, D_MODEL, D_MODEL), lambda l, j: (l, 0, j)),
            pl.BlockSpec((None, 1, D_MODEL), lambda l, j: (l, 0, j)),
        ],
        out_specs=pl.BlockSpec((None, MOD_ROWS, D_MODEL), lambda l, j: (l, 0, j)),
        out_shape=jax.ShapeDtypeStruct((depth, MOD_ROWS, n_col), F32),
        compiler_params=_params(2),
        name="modulation",
    )(cvec, w_mod, b_mod.reshape(depth, 1, n_col))
    return out.reshape(depth, MOD_ROWS, N_MOD, D_MODEL)


STD_AKV, STD_BK, STD_BKR, STD_CK, STD_CKR, STD_KR, STD_KRR = (i * LANES for i in range(7))
STD_COLS = 7 * LANES
T_AQ = 0
T_AKV = T_AQ + A_Q_RANK
T_BQ = T_AKV + A_KV_RANK
T_CQ = T_BQ + HB * B_HD
T_BV = T_CQ + HC * C_HD
T_CV = T_BV + HB_KV * B_HD
T_ROWS = T_CV + HC_KV * C_HD


def _swap_halves(x, nf):
    parts = []
    for a in range(2):
        base = a * 2 * nf
        parts += [x[base + nf:base + 2 * nf], x[base:base + nf]]
    return jnp.concatenate(parts, axis=0)


def _ones_rows(t):
    row = lax.broadcasted_iota(jnp.int32, (BF16_ROWS, t), 0)
    return jnp.where(row == 0, 1.0, 0.0).astype(BF16)


def _proj_kernel(*refs, n_src, n_tiles, n_flat, scale_a, scale_h):
    (mod_ref, g1_ref, wstd_ref, wt_ref, wqb_ref, wk_ref, wv_ref,
     gqa_ref, gkvr_ref, gkvc_ref, gqn_ref, gknr_ref, gknrr_ref,
     ck_ref, sk_ref, ca_ref, sa_ref, ctq_ref, stq_ref, cta_ref, sta_ref,
     qa_ref, ka_ref, va_ref, qb_ref, kb_ref, vb_ref, qc_ref, kc_ref, vc_ref, ps_scr, pt_scr) = refs[n_src:]
    t = TOKEN_TILE
    j = pl.program_id(0)

    @pl.when(j == 0)
    def _():
        ps_scr[1] = jnp.zeros(ps_scr.shape[1:], F32)
        pt_scr[1] = jnp.zeros(pt_scr.shape[1:], F32)

    def stage1(slot):
        is_ctx = jnp.minimum(j, n_flat - 1) % n_tiles == 0
        h = _modulated_norm(_token_tile(refs[:n_src], is_ctx), g1_ref[...], mod_ref[0:1, :], mod_ref[1:2, :])
        hb = h.astype(BF16)
        ps_scr[slot] = _dot(hb, wstd_ref[...])
        pt_scr[slot] = _dot_nt(wt_ref[...], hb)

    def stage2(slot):
        ps, pt = ps_scr.at[slot], pt_scr.at[slot]
        ones = _ones_rows(t)

        aq = pt[T_AQ:T_AQ + A_Q_RANK]
        aqn = aq * lax.rsqrt(jnp.mean(aq * aq, axis=0, keepdims=True) + EPS) * gqa_ref[...]
        qt = _dot(wqb_ref[...], aqn.astype(BF16))
        cta, sta = cta_ref[...], sta_ref[...]
        qs = scale_a * LOG2E
        for hh in range(HA):
            base = hh * HEAD_PAD
            qa_ref[hh, 0:A_NOPE, :] = (qt[base:base + A_NOPE] * qs).astype(BF16)
            r = qt[base + A_NOPE:base + A_NOPE + A_ROPE]
            rr = r * cta + _swap_halves(r, A_ROPE // 4) * sta
            qa_ref[hh, A_NOPE:A_NOPE + A_ROPE, :] = (rr * qs).astype(BF16)
            qa_ref[hh, A_NOPE + A_ROPE:HEAD_PAD, :] = jnp.zeros((HEAD_PAD - A_NOPE - A_ROPE, t), BF16)

        akv = ps[:, STD_AKV:STD_AKV + LANES]
        akvn = akv * lax.rsqrt(jnp.mean(akv * akv, axis=-1, keepdims=True) + EPS) * gkvr_ref[...]
        kn = _dot(akvn.astype(BF16), wk_ref[...])
        kr = ps[:, STD_KR:STD_KR + LANES] * ca_ref[...] + ps[:, STD_KRR:STD_KRR + LANES] * sa_ref[...]
        for hh in range(HA):
            ka_ref[hh] = (kn[:, hh * HEAD_PAD:(hh + 1) * HEAD_PAD] + kr).astype(BF16)

        akvt = pt[T_AKV:T_AKV + A_KV_RANK]
        akvtn = akvt * lax.rsqrt(jnp.mean(akvt * akvt, axis=0, keepdims=True) + EPS) * gkvc_ref[...]
        vt = _dot(wv_ref[...], akvtn.astype(BF16))
        for hh in range(HA):
            va_ref[hh, 0:A_V, :] = vt[hh * A_V:(hh + 1) * A_V].astype(BF16)
            va_ref[hh, A_V:V_ROWS, :] = ones

        ctq, stq = ctq_ref[...], stq_ref[...]
        ck, sk = ck_ref[...], sk_ref[...]
        qsh = scale_h * LOG2E
        lane = lax.broadcasted_iota(jnp.int32, (t, LANES), 1)
        first = lane < B_HD

        for hh in range(HB):
            blk = pt[T_BQ + hh * B_HD:T_BQ + (hh + 1) * B_HD]
            y = blk * lax.rsqrt(jnp.mean(blk * blk, axis=0, keepdims=True) + EPS) * gqn_ref[...]
            qb_ref[hh] = ((y * ctq + _swap_halves(y, B_HD // 4) * stq) * qsh).astype(BF16)
            blk = pt[T_CQ + hh * C_HD:T_CQ + (hh + 1) * C_HD]
            qc_ref[hh] = ((blk * ctq + _swap_halves(blk, C_HD // 4) * stq) * qsh).astype(BF16)

        bk = ps[:, STD_BK:STD_BK + LANES]
        sq = bk * bk
        s0 = jnp.sum(jnp.where(first, sq, 0.0), axis=-1, keepdims=True)
        s1 = jnp.sum(jnp.where(first, 0.0, sq), axis=-1, keepdims=True)
        rk = lax.rsqrt(jnp.where(first, s0, s1) * (1.0 / B_HD) + EPS)
        kb = (bk * rk * gknr_ref[...]) * ck + (ps[:, STD_BKR:STD_BKR + LANES] * rk * gknrr_ref[...]) * sk
        kc = ps[:, STD_CK:STD_CK + LANES] * ck + ps[:, STD_CKR:STD_CKR + LANES] * sk
        for g in range(HB_KV):
            kb_ref[g] = kb[:, g * B_HD:(g + 1) * B_HD].astype(BF16)
            kc_ref[g] = kc[:, g * C_HD:(g + 1) * C_HD].astype(BF16)
            vb_ref[g, 0:B_HD, :] = pt[T_BV + g * B_HD:T_BV + (g + 1) * B_HD].astype(BF16)
            vb_ref[g, B_HD:V_ROWS, :] = ones
            vc_ref[g, 0:C_HD, :] = pt[T_CV + g * C_HD:T_CV + (g + 1) * C_HD].astype(BF16)
            vc_ref[g, C_HD:V_ROWS, :] = ones

    @pl.when(j % 2 == 0)
    def _():
        stage1(0)
        stage2(1)

    @pl.when(j % 2 == 1)
    def _():
        stage1(1)
        stage2(0)


def _project(src, mods, l, wts, tabs, scale_a, scale_h):
    b = src[-1].shape[0]
    s = sum(a.shape[1] for a in src)
    t = TOKEN_TILE
    n_t = s // t
    n_flat = b * n_t
    tile1 = lambda j: jnp.minimum(j, n_flat - 1)
    tile2 = lambda j: jnp.maximum(j - 1, 0)
    if len(src) == 1:
        tok_specs = [pl.BlockSpec((None, t, D_MODEL), lambda j: (tile1(j) // n_t, tile1(j) % n_t, 0))]
    else:
        tok_specs = [pl.BlockSpec((None, t, D_MODEL), lambda j: (tile1(j) // n_t, 0, 0)),
                     pl.BlockSpec((None, t, D_MODEL),
                                  lambda j: (tile1(j) // n_t, jnp.maximum(tile1(j) % n_t - 1, 0), 0))]
    mod_spec = pl.BlockSpec(
        (None, None, N_MOD, D_MODEL),
        lambda j: (l, jnp.where(tile1(j) % n_t == 0, MOD_ROWS // 2, tile1(j) // n_t), 0, 0))
    tok2 = lambda j: (tile2(j) % n_t, 0)
    feat2 = lambda j: (0, tile2(j) % n_t)
    names = ("g1", "w_std", "w_t", "w_qb", "w_k", "w_v", "g_qa_col", "g_kv_row", "g_kv_col", "g_qn_col", "g_kn_row",
             "g_kn_rot_row")
    in_specs = tok_specs + [mod_spec] + [_layer(wts[n], l) for n in names] + [
        pl.BlockSpec((t, LANES), tok2), pl.BlockSpec((t, LANES), tok2),
        pl.BlockSpec((t, LANES), tok2), pl.BlockSpec((t, LANES), tok2),
        pl.BlockSpec((B_HD, t), feat2), pl.BlockSpec((B_HD, t), feat2),
        pl.BlockSpec((A_ROPE, t), feat2), pl.BlockSpec((A_ROPE, t), feat2),
    ]
    qspec = lambda heads, rows: pl.BlockSpec((None, heads, rows, t),
                                             lambda j: (tile2(j) // n_t, 0, 0, _q_slot(tile2(j) % n_t, n_t)))
    vspec = lambda heads, rows: pl.BlockSpec((None, heads, rows, t),
                                             lambda j: (tile2(j) // n_t, 0, 0, tile2(j) % n_t))
    kspec = lambda heads, cols: pl.BlockSpec((None, heads, t, cols),
                                             lambda j: (tile2(j) // n_t, 0, tile2(j) % n_t, 0))
    out_specs = [
        qspec(HA, HEAD_PAD), kspec(HA, HEAD_PAD), vspec(HA, V_ROWS),
        qspec(HB, B_HD), kspec(HB_KV, B_HD), vspec(HB_KV, V_ROWS),
        qspec(HC, C_HD), kspec(HC_KV, C_HD), vspec(HC_KV, V_ROWS),
    ]
    sd = jax.ShapeDtypeStruct
    out_shape = [
        sd((b, HA, HEAD_PAD, s), BF16), sd((b, HA, s, HEAD_PAD), BF16), sd((b, HA, V_ROWS, s), BF16),
        sd((b, HB, B_HD, s), BF16), sd((b, HB_KV, s, B_HD), BF16), sd((b, HB_KV, V_ROWS, s), BF16),
        sd((b, HC, C_HD, s), BF16), sd((b, HC_KV, s, C_HD), BF16), sd((b, HC_KV, V_ROWS, s), BF16),
    ]
    return pl.pallas_call(
        functools.partial(_proj_kernel, n_src=len(src), n_tiles=n_t, n_flat=n_flat, scale_a=scale_a,
                          scale_h=scale_h),
        grid=(n_flat + 1,),
        in_specs=in_specs,
        out_specs=out_specs,
        out_shape=out_shape,
        scratch_shapes=[pltpu.VMEM((2, t, STD_COLS), F32), pltpu.VMEM((2, T_ROWS, t), F32)],
        compiler_params=_params(1),
        name="projection",
    )(*src, mods, *(wts[n] for n in names),
      tabs["ck"], tabs["sk"], tabs["ca"], tabs["sa"], tabs["ctq"], tabs["stq"], tabs["cta"], tabs["sta"])


def _attn_kernel(*refs, n_heads, group, has_sink):
    if has_sink:
        q_ref, qn_ref, k_ref, v_ref, sink_ref, o_ref, s_scr, m_scr = refs
    else:
        q_ref, qn_ref, k_ref, v_ref, o_ref, s_scr, m_scr = refs
    n_keys, tq = s_scr.shape[1], s_scr.shape[2]
    chunk = min(KEY_CHUNK, n_keys)
    n_chunks = n_keys // chunk
    lag = min(OUTPUT_LAG, n_chunks - 1)

    def score_chunk(head, c, mrun):
        q = qn_ref[0] if head == n_heads else q_ref[head]
        rows = slice(c * chunk, (c + 1) * chunk)
        s = _dot(k_ref[(head % n_heads) // group, rows, :], q)
        s_scr[head % 2, rows, :] = s
        part = jnp.max(s.reshape(chunk // SUBLANES, SUBLANES, tq), axis=0)
        return part if mrun is None else jnp.maximum(mrun, part)

    def output_chunk(head, c, m, acc):
        rows = slice(c * chunk, (c + 1) * chunk)
        p = jnp.exp2(s_scr[head % 2, rows, :] - m).astype(BF16)
        part = _dot(v_ref[head // group, :, rows], p)
        return part if acc is None else acc + part

    @pl.when(pl.program_id(1) == 0)
    def _():
        mrun = None
        for c in range(n_chunks):
            mrun = score_chunk(0, c, mrun)
        m_scr[...] = jnp.max(mrun, axis=0, keepdims=True)

    maxes = {0: m_scr[...]}
    mrun = acc = m = snk = None
    for g in range(n_heads * n_chunks + lag):
        if g < n_heads * n_chunks:
            hs, cs = divmod(g, n_chunks)
            mrun = score_chunk(hs + 1, cs, mrun)
            if cs == n_chunks - 1:
                maxes[hs + 1] = jnp.max(mrun, axis=0, keepdims=True)
                mrun = None
        if g >= lag:
            ho, co = divmod(g - lag, n_chunks)
            if co == 0:
                m = maxes.pop(ho)
                if has_sink:
                    snk = sink_ref[ho]
                    m = jnp.maximum(m, snk)
                acc = None
            acc = output_chunk(ho, co, m, acc)
            if co == n_chunks - 1:
                denom = acc[A_V:A_V + 1]
                if has_sink:
                    denom = denom + jnp.exp2(snk - m)
                o_ref[ho * A_V:(ho + 1) * A_V, :] = (acc[0:A_V] * (1.0 / denom)).astype(BF16)
    m_scr[...] = maxes[n_heads]


def _attention(qt, k, vt, sink2, *, n_keys, tq, q0, n_tiles, out=None):
    b, n_heads, dk, s = qt.shape
    hk = k.shape[1]
    in_specs = [
        pl.BlockSpec((None, n_heads, dk, tq), lambda bi, ti: (bi, 0, 0, ti + q0)),
        pl.BlockSpec((None, 1, dk, tq), lambda bi, ti: (bi, 0, 0, jnp.minimum(ti + 1, n_tiles - 1) + q0)),
        pl.BlockSpec((None, hk, n_keys, k.shape[3]), lambda bi, ti: (bi, 0, 0, 0)),
        pl.BlockSpec((None, hk, V_ROWS, n_keys), lambda bi, ti: (bi, 0, 0, 0)),
    ]
    args = [qt, qt, k, vt]
    if sink2 is not None:
        in_specs.append(_full(sink2.shape))
        args.append(sink2)
    aliases = {}
    if out is not None:
        in_specs.append(pl.BlockSpec(memory_space=pl.ANY))
        args.append(out)
        aliases = {len(args) - 1: 0}
    kern = functools.partial(_attn_kernel, n_heads=n_heads, group=n_heads // hk, has_sink=sink2 is not None)
    if out is not None:
        kern = _drop_last_input(kern, len(args))
    return pl.pallas_call(
        kern,
        grid=(b, n_tiles),
        in_specs=in_specs,
        out_specs=pl.BlockSpec((None, n_heads * A_V, tq), lambda bi, ti: (bi, 0, ti + q0)),
        out_shape=jax.ShapeDtypeStruct((b, n_heads * A_V, s), BF16),
        input_output_aliases=aliases,
        scratch_shapes=[pltpu.VMEM((2, n_keys, tq), F32), pltpu.VMEM((1, tq), F32)],
        compiler_params=_params(2),
        name="attention",
    )(*args)


def _drop_last_input(kern, n_in):
    def wrapped(*refs):
        return kern(*refs[:n_in - 1], *refs[n_in:])
    return wrapped


def _window_caps(n_ctx, tq):
    n_band = WINDOW + tq
    n_keys = n_ctx + n_band + WINDOW
    row = jnp.arange(n_keys, dtype=jnp.int32)[:, None]
    col = jnp.arange(tq, dtype=jnp.int32)[None, :]
    rel = row - n_ctx - WINDOW - col
    in_band = (rel >= -WINDOW) & (rel <= WINDOW)
    caps = []
    for first in (False, True):
        for last in (False, True):
            ok_lo = (row >= n_ctx + WINDOW) | (not first)
            ok_hi = (row < n_ctx + n_band) | (not last)
            valid = (row < n_ctx) | (in_band & ok_lo & ok_hi)
            caps.append(jnp.where(valid, jnp.inf, NEG).astype(F32))
    return jnp.stack(caps)


def _window_kernel(q_ref, k_ref, v_ref, sink_ref, cap_ref, o_ref, k_scr, v_scr, s_scr, *, n_ctx, n_tiles):
    tq = q_ref.shape[2]
    ti = pl.program_id(1)
    start = pl.multiple_of(n_ctx + ti * tq, LANES)
    lo = pl.multiple_of(start - WINDOW, LANES)
    nxt = pl.multiple_of(jnp.minimum(start + tq, n_ctx + (n_tiles - 1) * tq + tq - WINDOW), LANES)
    n_band = WINDOW + tq
    n_keys = n_ctx + n_band + WINDOW

    for g in range(HC_KV):
        k_scr[g, 0:n_ctx, :] = k_ref[g, 0:n_ctx, :]
        k_scr[g, n_ctx:n_ctx + n_band, :] = k_ref[g, pl.ds(lo, n_band), :]
        k_scr[g, n_ctx + n_band:n_keys, :] = k_ref[g, pl.ds(nxt, WINDOW), :]
        v_scr[g, :, 0:n_ctx] = v_ref[g, :, 0:n_ctx]
        v_scr[g, :, n_ctx:n_ctx + n_band] = v_ref[g, :, pl.ds(lo, n_band)]
        v_scr[g, :, n_ctx + n_band:n_keys] = v_ref[g, :, pl.ds(nxt, WINDOW)]

    group = HC // HC_KV
    chunk = KEY_CHUNK
    n_chunks = n_keys // chunk

    def score_chunk(hh, c, mrun):
        rows = slice(c * chunk, (c + 1) * chunk)
        s = jnp.minimum(_dot(k_scr[hh // group, rows, :], q_ref[hh]), cap_ref[rows, :])
        s_scr[hh % 2, rows, :] = s
        part = jnp.max(s.reshape(chunk // SUBLANES, SUBLANES, tq), axis=0)
        return part if mrun is None else jnp.maximum(mrun, part)

    def output_chunk(hh, c, m, acc):
        rows = slice(c * chunk, (c + 1) * chunk)
        p = jnp.exp2(s_scr[hh % 2, rows, :] - m).astype(BF16)
        part = _dot(v_scr[hh // group, :, rows], p)
        return part if acc is None else acc + part

    lag = n_chunks + WINDOW_LAG
    maxes = {}
    mrun = acc = m = snk = None
    for g in range(HC * n_chunks + lag):
        if g < HC * n_chunks:
            hs, cs = divmod(g, n_chunks)
            mrun = score_chunk(hs, cs, mrun)
            if cs == n_chunks - 1:
                maxes[hs] = jnp.max(mrun, axis=0, keepdims=True)
                mrun = None
        if g >= lag:
            ho, co = divmod(g - lag, n_chunks)
            if co == 0:
                snk = sink_ref[ho]
                m = jnp.maximum(maxes.pop(ho), snk)
                acc = None
            acc = output_chunk(ho, co, m, acc)
            if co == n_chunks - 1:
                denom = acc[C_HD:C_HD + 1] + jnp.exp2(snk - m)
                o_ref[ho * C_HD:(ho + 1) * C_HD, :] = (acc[0:C_HD] * (1.0 / denom)).astype(BF16)


def _window_attention(qt, k, vt, sink2, caps, *, n_ctx):
    b, n_heads, dk, s = qt.shape
    hk = k.shape[1]
    t = TOKEN_TILE
    n_tiles = (s - n_ctx) // t
    n_win_keys = n_ctx + WINDOW + t + WINDOW
    return pl.pallas_call(
        functools.partial(_window_kernel, n_ctx=n_ctx, n_tiles=n_tiles),
        grid=(b, n_tiles),
        in_specs=[
            pl.BlockSpec((None, n_heads, dk, t), lambda bi, ti: (bi, 0, 0, ti)),
            pl.BlockSpec((None, hk, s, dk), lambda bi, ti: (bi, 0, 0, 0)),
            pl.BlockSpec((None, hk, V_ROWS, s), lambda bi, ti: (bi, 0, 0, 0)),
            _full(sink2.shape),
            pl.BlockSpec((None, n_win_keys, t),
                         lambda bi, ti: (2 * (ti == 0).astype(jnp.int32) + (ti == n_tiles - 1).astype(jnp.int32), 0, 0)),
        ],
        out_specs=pl.BlockSpec((None, n_heads * C_HD, t), lambda bi, ti: (bi, 0, ti)),
        out_shape=jax.ShapeDtypeStruct((b, n_heads * C_HD, s), BF16),
        scratch_shapes=[
            pltpu.VMEM((hk, n_win_keys, dk), BF16), pltpu.VMEM((hk, V_ROWS, n_win_keys), BF16),
            pltpu.VMEM((2, n_win_keys, t), F32),
        ],
        compiler_params=_params(2),
        name="window_attention",
    )(qt, k, vt, sink2, caps)


def _merge_kernel(*refs, n_src):
    mod_ref, g1_ref, oa_ref, ob_ref, oc_ref, wg_ref, bg_ref, wbr_ref, wout_ref, xo_ref = refs[n_src:]
    x = _token_tile(refs[:n_src], pl.program_id(1) == 0)
    h = _modulated_norm(x, g1_ref[...], mod_ref[0:1, :], mod_ref[1:2, :])
    hb = h.astype(BF16)
    y = None
    for i, o_ref in enumerate((oa_ref, ob_ref, oc_ref)):
        cols = slice(i * D_MODEL, (i + 1) * D_MODEL)
        gate = _sigmoid(_dot(hb, wg_ref[:, cols]) + bg_ref[:, cols])
        term = gate * _dot_tn(o_ref[...], wbr_ref[i])
        y = term if y is None else y + term
    z = _dot(y.astype(BF16), wout_ref[...])
    xo_ref[...] = x + mod_ref[2:3, :] * z


def _merge(src, mods, l, wts, oa, ob, oc, *, tile0, n_tiles):
    b = src[-1].shape[0]
    s = sum(a.shape[1] for a in src)
    t = TOKEN_TILE
    feat = lambda bi, ti: (bi, 0, _q_slot(ti + tile0, s // t))
    names = ("w_gate", "b_gate", "w_branch", "w_out")
    return pl.pallas_call(
        functools.partial(_merge_kernel, n_src=len(src)),
        grid=(b, n_tiles),
        in_specs=_token_specs(src, tile0) + [_mod_spec(l, tile0), _layer(wts["g1"], l)] + [
            pl.BlockSpec((None, BRANCH_W, t), feat),
            pl.BlockSpec((None, BRANCH_W, t), feat),
            pl.BlockSpec((None, BRANCH_W, t), feat),
        ] + [_layer(wts[n], l) for n in names],
        out_specs=pl.BlockSpec((None, t, D_MODEL), lambda bi, ti: (bi, ti + tile0, 0)),
        out_shape=jax.ShapeDtypeStruct((b, s, D_MODEL), F32),
        input_output_aliases={0: 0} if len(src) == 1 else {},
        compiler_params=_params(2),
        name="merge",
    )(*src, mods, wts["g1"], oa, ob, oc, *(wts[n] for n in names))


def _ffn_kernel(xp_ref, x_ref, xn_ref, mod_ref, g2_ref, wup_ref, wconv_ref, bconv_ref, wdown_ref, gf_ref,
                o_ref, act_ref, *, tile0, n_seq_tiles, final_norm):
    t = x_ref.shape[0]
    halo = SUBLANES
    ta = pl.program_id(1) + tile0
    x = x_ref[...]
    xa = jnp.concatenate([xp_ref[...], x, xn_ref[...]], axis=0)
    h = _modulated_norm(xa, g2_ref[...], mod_ref[3:4, :], mod_ref[4:5, :])
    row = lax.broadcasted_iota(jnp.int32, (t + 2 * halo, 1), 0)
    keep_prev = (ta > 1).astype(F32)
    keep_next = ((ta > 0) & (ta < n_seq_tiles - 1)).astype(F32)
    keep = jnp.where(row < halo, keep_prev, jnp.where(row >= t + halo, keep_next, 1.0))
    hb = (h * keep).astype(BF16)

    def conv(u, cols):
        return (u[halo - 1:halo - 1 + t] * wconv_ref[0:1, cols] + u[halo:halo + t] * wconv_ref[1:2, cols]
                + u[halo + 1:halo + 1 + t] * wconv_ref[2:3, cols] + bconv_ref[:, cols])

    for c in range(D_FF // FF_CHUNK):
        ca = slice(c * FF_CHUNK, (c + 1) * FF_CHUNK)
        cg = slice(D_FF + c * FF_CHUNK, D_FF + (c + 1) * FF_CHUNK)
        a = conv(_dot(hb, wup_ref[:, ca]), ca)
        gv = conv(_dot(hb, wup_ref[:, cg]), cg)
        act_ref[:, ca] = (a * _sigmoid(a) * gv).astype(BF16)
    y = x + mod_ref[5:6, :] * _dot(act_ref[...], wdown_ref[...])
    if final_norm:
        y = y * lax.rsqrt(jnp.mean(y * y, axis=-1, keepdims=True) + EPS) * gf_ref[...]
    o_ref[...] = y


def _conv_ffn(xc, mods, l, wts, g_final, *, tile0, n_tiles, final_norm):
    b, s, _ = xc.shape
    t = TOKEN_TILE
    n_seq_tiles = s // t
    per = t // SUBLANES
    last_blk = s // SUBLANES - 1
    out_rows = n_tiles * t if final_norm else s
    out_tile0 = 0 if final_norm else tile0
    return pl.pallas_call(
        functools.partial(_ffn_kernel, tile0=tile0, n_seq_tiles=n_seq_tiles, final_norm=final_norm),
        grid=(b, n_tiles),
        in_specs=[
            pl.BlockSpec((None, SUBLANES, D_MODEL), lambda bi, ti: (bi, jnp.maximum((ti + tile0) * per - 1, 0), 0)),
            pl.BlockSpec((None, t, D_MODEL), lambda bi, ti: (bi, ti + tile0, 0)),
            pl.BlockSpec((None, SUBLANES, D_MODEL),
                         lambda bi, ti: (bi, jnp.minimum((ti + tile0 + 1) * per, last_blk), 0)),
            _mod_spec(l, tile0),
            _layer(wts["g2"], l), _layer(wts["w_up"], l), _layer(wts["w_conv"], l), _layer(wts["b_conv"], l),
            _layer(wts["w_down"], l),
            _full((1, D_MODEL)),
        ],
        out_specs=pl.BlockSpec((None, t, D_MODEL), lambda bi, ti: (bi, ti + out_tile0, 0)),
        out_shape=jax.ShapeDtypeStruct((b, out_rows, D_MODEL), F32),
        scratch_shapes=[pltpu.VMEM((t, D_FF), BF16)],
        compiler_params=_params(2),
        name="conv_ffn",
    )(xc, xc, xc, mods, wts["g2"], wts["w_up"], wts["w_conv"], wts["b_conv"], wts["w_down"], g_final)


def _partner(head_dim):
    nf = head_dim // 4
    d = np.arange(head_dim)
    a, half, f = d // (2 * nf), (d % (2 * nf)) // nf, d % nf
    return a * 2 * nf + (1 - half) * nf + f, np.where(half == 0, -1.0, 1.0).astype(np.float32)


def _rope_tables(n_ctx, n_lat):
    def full(head_dim):
        nf = head_dim // 4
        _, sign = _partner(head_dim)
        rows = n_lat // GRID_W
        row = jnp.repeat(jnp.arange(rows, dtype=F32), GRID_W)
        col = jnp.tile(jnp.arange(GRID_W, dtype=F32), rows)
        inv = ROPE_THETA ** (-jnp.arange(nf, dtype=F32) / nf)
        ang = jnp.stack([row[:, None] * inv, col[:, None] * inv], axis=1)
        c = jnp.broadcast_to(jnp.cos(ang)[:, :, None, :], (n_lat, 2, 2, nf)).reshape(n_lat, head_dim)
        s = jnp.broadcast_to(jnp.sin(ang)[:, :, None, :], (n_lat, 2, 2, nf)).reshape(n_lat, head_dim) * sign
        c = jnp.concatenate([jnp.ones((n_ctx, head_dim), F32), c], axis=0)
        s = jnp.concatenate([jnp.zeros((n_ctx, head_dim), F32), s], axis=0)
        return c, s

    c64, s64 = full(B_HD)
    c32, s32 = full(A_ROPE)
    pad_a = lambda v: jnp.pad(v, ((0, 0), (A_NOPE, LANES - A_NOPE - A_ROPE)))
    return {
        "ck": jnp.tile(c64, (1, LANES // B_HD)), "sk": jnp.tile(s64, (1, LANES // B_HD)),
        "ca": pad_a(c32), "sa": pad_a(s32),
        "ctq": c64.T, "stq": s64.T, "cta": c32.T, "sta": s32.T,
    }


def _rot_ranges(base, head_dim, n_heads):
    nf = head_dim // 4
    out = []
    for hh in range(n_heads):
        b0 = base + hh * head_dim
        for a in range(2):
            out += [(b0 + a * 2 * nf + nf, b0 + a * 2 * nf + 2 * nf), (b0 + a * 2 * nf, b0 + a * 2 * nf + nf)]
    return out


def _prepare_weights(w_in, b_gate, g_norm1, g_q_a, w_q_b, g_kv_a, w_kv_b, g_qn, g_kn, w_branch, w_out, g_norm2,
                     w_up, w_conv, b_conv, w_down):
    depth = w_in.shape[0]
    cuts = [int(v) for v in np.cumsum((0,) + IN_SIZES)]
    aq, akv, akr, bq, bk, bv, cq, ck, cv, gl = ((cuts[i], cuts[i + 1]) for i in range(len(IN_SIZES)))
    cols = lambda ranges: [w_in[:, :, lo:hi] for lo, hi in ranges]
    zeros = lambda n: [jnp.zeros((depth, D_MODEL, n), F32)]
    pad_a = lambda pieces: zeros(A_NOPE) + pieces + zeros(LANES - A_NOPE - A_ROPE)
    w_std = jnp.concatenate(
        cols([akv, bk]) + cols(_rot_ranges(bk[0], B_HD, HB_KV)) + cols([ck]) + cols(_rot_ranges(ck[0], C_HD, HC_KV))
        + pad_a(cols([akr])) + pad_a(cols(_rot_ranges(akr[0], A_ROPE, 1))), axis=2)
    w_t = jnp.swapaxes(jnp.concatenate(cols([aq, akv, bq, cq, bv, cv]), axis=2), 1, 2)
    wqb = w_q_b.reshape(depth, A_Q_RANK, HA, A_NOPE + A_ROPE)
    wqb = jnp.pad(wqb, ((0, 0), (0, 0), (0, 0), (0, HEAD_PAD - A_NOPE - A_ROPE)))
    wqb = jnp.swapaxes(wqb.reshape(depth, A_Q_RANK, HA * HEAD_PAD), 1, 2)
    wkv = w_kv_b.reshape(depth, A_KV_RANK, HA, A_NOPE + A_V)
    w_k = jnp.pad(wkv[..., :A_NOPE], ((0, 0), (0, 0), (0, 0), (0, HEAD_PAD - A_NOPE)))
    w_k = w_k.reshape(depth, A_KV_RANK, HA * HEAD_PAD)
    w_v = jnp.swapaxes(wkv[..., A_NOPE:].reshape(depth, A_KV_RANK, HA * A_V), 1, 2)
    g_kn2 = jnp.tile(g_kn, (1, LANES // B_HD))
    p64, _ = _partner(B_HD)
    p128 = np.concatenate([p64 + i * B_HD for i in range(LANES // B_HD)])
    return {
        "g1": g_norm1.reshape(depth, 1, D_MODEL), "g2": g_norm2.reshape(depth, 1, D_MODEL),
        "w_std": w_std.astype(BF16), "w_t": w_t.astype(BF16), "w_qb": wqb.astype(BF16),
        "w_k": w_k.astype(BF16), "w_v": w_v.astype(BF16),
        "g_qa_col": g_q_a.reshape(depth, A_Q_RANK, 1), "g_kv_row": g_kv_a.reshape(depth, 1, A_KV_RANK),
        "g_kv_col": g_kv_a.reshape(depth, A_KV_RANK, 1), "g_qn_col": g_qn.reshape(depth, B_HD, 1),
        "g_kn_row": g_kn2.reshape(depth, 1, LANES), "g_kn_rot_row": g_kn2[:, p128].reshape(depth, 1, LANES),
        "w_gate": w_in[:, :, gl[0]:gl[1]].astype(BF16), "b_gate": b_gate.reshape(depth, 1, N_BRANCH * D_MODEL),
        "w_branch": w_branch.astype(BF16), "w_out": w_out.astype(BF16),
        "w_up": w_up.astype(BF16), "w_conv": w_conv, "b_conv": b_conv.reshape(depth, 1, 2 * D_FF),
        "w_down": w_down.astype(BF16),
    }


def kernel(x, c, ctx, c_ctx, w_mod, b_mod, g_norm1, w_in, b_gate, g_q_a, w_q_b, g_kv_a, w_kv_b, g_qn, g_kn, sink,
           w_branch, w_out, g_norm2, w_up, w_conv, b_conv, w_down, g_final):
    b, n_lat, d = x.shape
    n_ctx = ctx.shape[1]
    depth = w_mod.shape[0]
    t = TOKEN_TILE
    assert d == D_MODEL and n_ctx == t and n_lat % t == 0 and n_lat % GRID_W == 0 and b <= MOD_ROWS // 2
    assert n_lat % QUERY_TILE == 0
    assert depth >= 2
    s = n_ctx + n_lat
    n_t = s // t
    scale_a = 1.0 / math.sqrt(A_NOPE + A_ROPE)
    scale_h = 1.0 / math.sqrt(B_HD)

    cvec = jnp.zeros((MOD_ROWS, D_MODEL), F32).at[:b].set(c).at[MOD_ROWS // 2].set(c_ctx)
    mods = _modulation(cvec, w_mod, b_mod)
    tabs = _rope_tables(n_ctx, n_lat)
    wts = _prepare_weights(w_in, b_gate, g_norm1, g_q_a, w_q_b, g_kv_a, w_kv_b, g_qn, g_kn, w_branch, w_out, g_norm2,
                           w_up, w_conv, b_conv, w_down)
    gf = g_final.reshape(1, D_MODEL)
    caps = _window_caps(n_ctx, t)
    sinks = jnp.broadcast_to((sink * LOG2E).reshape(depth, HC, 1, 1), (depth, HC, 1, t)).astype(F32)

    src = (ctx, x)
    for l in range(depth):
        last = l == depth - 1
        qa, ka, va, qb, kb, vb, qc, kc, vc = _project(src, mods, l, wts, tabs, scale_a, scale_h)
        lat = dict(n_keys=s, tq=QUERY_TILE, q0=0, n_tiles=n_lat // QUERY_TILE)
        oa = _attention(qa, ka, va, None, **lat)
        ob = _attention(qb, kb, vb, None, **lat)
        oc = _window_attention(qc, kc, vc, sinks[l], caps, n_ctx=n_ctx)
        if not last:
            cx = dict(n_keys=n_ctx, tq=t, q0=n_t - 1, n_tiles=1)
            oa = _attention(qa, ka, va, None, out=oa, **cx)
            ob = _attention(qb, kb, vb, None, out=ob, **cx)
            oc = _attention(qc, kc, vc, sinks[l], out=oc, **cx)
        tiles = dict(tile0=1, n_tiles=n_t - 1) if last else dict(tile0=0, n_tiles=n_t)
        xc = _merge(src, mods, l, wts, oa, ob, oc, **tiles)
        src = (_conv_ffn(xc, mods, l, wts, gf, final_norm=last, **tiles),)
    return src[0]
```

```python
import functools
import math

import jax
import jax.numpy as jnp
import numpy as np
from jax import lax
from jax.experimental import pallas as pl
from jax.experimental.pallas import tpu as pltpu

F32 = jnp.float32
BF16 = jnp.bfloat16

D_MODEL = 1024
GRID_W = 64
EPS = 1e-6
ROPE_THETA = 10000.0
WINDOW = 128
NEG = -1e30
HA, A_NOPE, A_ROPE, A_V, A_Q_RANK, A_KV_RANK = 8, 64, 32, 64, 256, 128
HB, HB_KV, B_HD = 8, 2, 64
HC, HC_KV, C_HD = 8, 2, 64
BRANCH_W = 512
N_BRANCH = 3
D_FF = 2816
CONV_W = 3
N_MOD = 6
IN_SIZES = (A_Q_RANK, A_KV_RANK, A_ROPE, HB * B_HD, HB_KV * B_HD, HB_KV * B_HD, HC * C_HD, HC_KV * C_HD,
            HC_KV * C_HD, N_BRANCH * D_MODEL)

LANES = 128
SUBLANES = 8
BF16_ROWS = 16
TOKEN_TILE = 256
QUERY_TILE = 512
MOD_ROWS = 16
HEAD_PAD = 128
V_ROWS = A_V + BF16_ROWS
FF_CHUNK = 256
KEY_CHUNK = 256
OUTPUT_LAG = 1
WINDOW_LAG = 2
VMEM_LIMIT = 56 * 1024 * 1024
LOG2E = 1.4426950408889634


def _dot(a, b):
    return jnp.dot(a, b, preferred_element_type=F32)


def _dot_nt(a, b):
    return lax.dot_general(a, b, (((1,), (1,)), ((), ())), preferred_element_type=F32)


def _dot_tn(a, b):
    return lax.dot_general(a, b, (((0,), (0,)), ((), ())), preferred_element_type=F32)


def _sigmoid(x):
    return 1.0 / (1.0 + jnp.exp(-x))


def _modulated_norm(x, g, shift, scale):
    ms = jnp.mean(x * x, axis=-1, keepdims=True)
    return (x * lax.rsqrt(ms + EPS) * g) * (1.0 + scale) + shift


def _params(n_grid):
    return pltpu.CompilerParams(dimension_semantics=("arbitrary",) * n_grid, vmem_limit_bytes=VMEM_LIMIT)


def _full(shape):
    nd = len(shape)
    return pl.BlockSpec(shape, lambda *_: (0,) * nd)


def _layer(arr, l):
    nd = arr.ndim - 1
    return pl.BlockSpec((None,) + arr.shape[1:], lambda *_: (l,) + (0,) * nd)


def _mod_spec(l, tile0):
    return pl.BlockSpec((None, None, N_MOD, D_MODEL),
                        lambda bi, ti: (l, jnp.where(ti + tile0 == 0, MOD_ROWS // 2, bi), 0, 0))


def _token_specs(src, tile0):
    t = TOKEN_TILE
    if len(src) == 1:
        return [pl.BlockSpec((None, t, D_MODEL), lambda bi, ti: (bi, ti + tile0, 0))]
    assert tile0 == 0
    return [pl.BlockSpec((None, t, D_MODEL), lambda bi, ti: (bi, 0, 0)),
            pl.BlockSpec((None, t, D_MODEL), lambda bi, ti: (bi, jnp.maximum(ti - 1, 0), 0))]


def _q_slot(tile, n_seq_tiles):
    return (tile + n_seq_tiles - 1) % n_seq_tiles


def _token_tile(refs, is_ctx):
    if len(refs) == 1:
        return refs[0][...]
    return jnp.where(is_ctx, refs[0][...], refs[1][...])


def _mod_kernel(c_ref, w_ref, b_ref, o_ref):
    c = c_ref[...]
    a = c * _sigmoid(c)
    o_ref[...] = jnp.dot(a, w_ref[...], precision=lax.Precision.HIGHEST, preferred_element_type=F32) + b_ref[...]


def _modulation(cvec, w_mod, b_mod):
    depth = w_mod.shape[0]
    n_col = N_MOD * D_MODEL
    out = pl.pallas_call(
        _mod_kernel,
        grid=(depth, N_MOD),
        in_specs=[
            pl.BlockSpec((MOD_ROWS, D_MODEL), lambda l, j: (0, 0)),
            pl.BlockSpec((None, D_MODEL, D_MODEL), lambda l, j: (l, 0, j)),
            pl.BlockSpec((None, 1, D_MODEL), lambda l, j: (l, 0, j)),
        ],
        out_specs=pl.BlockSpec((None, MOD_ROWS, D_MODEL), lambda l, j: (l, 0, j)),
        out_shape=jax.ShapeDtypeStruct((depth, MOD_ROWS, n_col), F32),
        compiler_params=_params(2),
        name="modulation",
    )(cvec, w_mod, b_mod.reshape(depth, 1, n_col))
    return out.reshape(depth, MOD_ROWS, N_MOD, D_MODEL)


STD_AKV, STD_BK, STD_BKR, STD_CK, STD_CKR, STD_KR, STD_KRR = (i * LANES for i in range(7))
STD_COLS = 7 * LANES
T_AQ = 0
T_AKV = T_AQ + A_Q_RANK
T_BQ = T_AKV + A_KV_RANK
T_CQ = T_BQ + HB * B_HD
T_BV = T_CQ + HC * C_HD
T_CV = T_BV + HB_KV * B_HD
T_ROWS = T_CV + HC_KV * C_HD


def _swap_halves(x, nf):
    parts = []
    for a in range(2):
        base = a * 2 * nf
        parts += [x[base + nf:base + 2 * nf], x[base:base + nf]]
    return jnp.concatenate(parts, axis=0)


def _ones_rows(t):
    row = lax.broadcasted_iota(jnp.int32, (BF16_ROWS, t), 0)
    return jnp.where(row == 0, 1.0, 0.0).astype(BF16)


def _proj_kernel(*refs, n_src, n_tiles, n_flat, scale_a, scale_h):
    (mod_ref, g1_ref, wstd_ref, wt_ref, wqb_ref, wk_ref, wv_ref,
     gqa_ref, gkvr_ref, gkvc_ref, gqn_ref, gknr_ref, gknrr_ref,
     ck_ref, sk_ref, ca_ref, sa_ref, ctq_ref, stq_ref, cta_ref, sta_ref,
     qa_ref, ka_ref, va_ref, qb_ref, kb_ref, vb_ref, qc_ref, kc_ref, vc_ref, ps_scr, pt_scr) = refs[n_src:]
    t = TOKEN_TILE
    j = pl.program_id(0)

    @pl.when(j == 0)
    def _():
        ps_scr[1] = jnp.zeros(ps_scr.shape[1:], F32)
        pt_scr[1] = jnp.zeros(pt_scr.shape[1:], F32)

    def stage1(slot):
        is_ctx = jnp.minimum(j, n_flat - 1) % n_tiles == 0
        h = _modulated_norm(_token_tile(refs[:n_src], is_ctx), g1_ref[...], mod_ref[0:1, :], mod_ref[1:2, :])
        hb = h.astype(BF16)
        ps_scr[slot] = _dot(hb, wstd_ref[...])
        pt_scr[slot] = _dot_nt(wt_ref[...], hb)

    def stage2(slot):
        ps, pt = ps_scr.at[slot], pt_scr.at[slot]
        ones = _ones_rows(t)

        aq = pt[T_AQ:T_AQ + A_Q_RANK]
        aqn = aq * lax.rsqrt(jnp.mean(aq * aq, axis=0, keepdims=True) + EPS) * gqa_ref[...]
        qt = _dot(wqb_ref[...], aqn.astype(BF16))
        cta, sta = cta_ref[...], sta_ref[...]
        qs = scale_a * LOG2E
        for hh in range(HA):
            base = hh * HEAD_PAD
            qa_ref[hh, 0:A_NOPE, :] = (qt[base:base + A_NOPE] * qs).astype(BF16)
            r = qt[base + A_NOPE:base + A_NOPE + A_ROPE]
            rr = r * cta + _swap_halves(r, A_ROPE // 4) * sta
            qa_ref[hh, A_NOPE:A_NOPE + A_ROPE, :] = (rr * qs).astype(BF16)
            qa_ref[hh, A_NOPE + A_ROPE:HEAD_PAD, :] = jnp.zeros((HEAD_PAD - A_NOPE - A_ROPE, t), BF16)

        akv = ps[:, STD_AKV:STD_AKV + LANES]
        akvn = akv * lax.rsqrt(jnp.mean(akv * akv, axis=-1, keepdims=True) + EPS) * gkvr_ref[...]
        kn = _dot(akvn.astype(BF16), wk_ref[...])
        kr = ps[:, STD_KR:STD_KR + LANES] * ca_ref[...] + ps[:, STD_KRR:STD_KRR + LANES] * sa_ref[...]
        for hh in range(HA):
            ka_ref[hh] = (kn[:, hh * HEAD_PAD:(hh + 1) * HEAD_PAD] + kr).astype(BF16)

        akvt = pt[T_AKV:T_AKV + A_KV_RANK]
        akvtn = akvt * lax.rsqrt(jnp.mean(akvt * akvt, axis=0, keepdims=True) + EPS) * gkvc_ref[...]
        vt = _dot(wv_ref[...], akvtn.astype(BF16))
        for hh in range(HA):
            va_ref[hh, 0:A_V, :] = vt[hh * A_V:(hh + 1) * A_V].astype(BF16)
            va_ref[hh, A_V:V_ROWS, :] = ones

        ctq, stq = ctq_ref[...], stq_ref[...]
        ck, sk = ck_ref[...], sk_ref[...]
        qsh = scale_h * LOG2E
        lane = lax.broadcasted_iota(jnp.int32, (t, LANES), 1)
        first = lane < B_HD

        for hh in range(HB):
            blk = pt[T_BQ + hh * B_HD:T_BQ + (hh + 1) * B_HD]
            y = blk * lax.rsqrt(jnp.mean(blk * blk, axis=0, keepdims=True) + EPS) * gqn_ref[...]
            qb_ref[hh] = ((y * ctq + _swap_halves(y, B_HD // 4) * stq) * qsh).astype(BF16)
            blk = pt[T_CQ + hh * C_HD:T_CQ + (hh + 1) * C_HD]
            qc_ref[hh] = ((blk * ctq + _swap_halves(blk, C_HD // 4) * stq) * qsh).astype(BF16)

        bk = ps[:, STD_BK:STD_BK + LANES]
        sq = bk * bk
        s0 = jnp.sum(jnp.where(first, sq, 0.0), axis=-1, keepdims=True)
        s1 = jnp.sum(jnp.where(first, 0.0, sq), axis=-1, keepdims=True)
        rk = lax.rsqrt(jnp.where(first, s0, s1) * (1.0 / B_HD) + EPS)
        kb = (bk * rk * gknr_ref[...]) * ck + (ps[:, STD_BKR:STD_BKR + LANES] * rk * gknrr_ref[...]) * sk
        kc = ps[:, STD_CK:STD_CK + LANES] * ck + ps[:, STD_CKR:STD_CKR + LANES] * sk
        for g in range(HB_KV):
            kb_ref[g] = kb[:, g * B_HD:(g + 1) * B_HD].astype(BF16)
            kc_ref[g] = kc[:, g * C_HD:(g + 1) * C_HD].astype(BF16)
            vb_ref[g, 0:B_HD, :] = pt[T_BV + g * B_HD:T_BV + (g + 1) * B_HD].astype(BF16)
            vb_ref[g, B_HD:V_ROWS, :] = ones
            vc_ref[g, 0:C_HD, :] = pt[T_CV + g * C_HD:T_CV + (g + 1) * C_HD].astype(BF16)
            vc_ref[g, C_HD:V_ROWS, :] = ones

    @pl.when(j % 2 == 0)
    def _():
        stage1(0)
        stage2(1)

    @pl.when(j % 2 == 1)
    def _():
        stage1(1)
        stage2(0)


def _project(src, mods, l, wts, tabs, scale_a, scale_h):
    b = src[-1].shape[0]
    s = sum(a.shape[1] for a in src)
    t = TOKEN_TILE
    n_t = s // t
    n_flat = b * n_t
    tile1 = lambda j: jnp.minimum(j, n_flat - 1)
    tile2 = lambda j: jnp.maximum(j - 1, 0)
    if len(src) == 1:
        tok_specs = [pl.BlockSpec((None, t, D_MODEL), lambda j: (tile1(j) // n_t, tile1(j) % n_t, 0))]
    else:
        tok_specs = [pl.BlockSpec((None, t, D_MODEL), lambda j: (tile1(j) // n_t, 0, 0)),
                     pl.BlockSpec((None, t, D_MODEL),
                                  lambda j: (tile1(j) // n_t, jnp.maximum(tile1(j) % n_t - 1, 0), 0))]
    mod_spec = pl.BlockSpec(
        (None, None, N_MOD, D_MODEL),
        lambda j: (l, jnp.where(tile1(j) % n_t == 0, MOD_ROWS // 2, tile1(j) // n_t), 0, 0))
    tok2 = lambda j: (tile2(j) % n_t, 0)
    feat2 = lambda j: (0, tile2(j) % n_t)
    names = ("g1", "w_std", "w_t", "w_qb", "w_k", "w_v", "g_qa_col", "g_kv_row", "g_kv_col", "g_qn_col", "g_kn_row",
             "g_kn_rot_row")
    in_specs = tok_specs + [mod_spec] + [_layer(wts[n], l) for n in names] + [
        pl.BlockSpec((t, LANES), tok2), pl.BlockSpec((t, LANES), tok2),
        pl.BlockSpec((t, LANES), tok2), pl.BlockSpec((t, LANES), tok2),
        pl.BlockSpec((B_HD, t), feat2), pl.BlockSpec((B_HD, t), feat2),
        pl.BlockSpec((A_ROPE, t), feat2), pl.BlockSpec((A_ROPE, t), feat2),
    ]
    qspec = lambda heads, rows: pl.BlockSpec((None, heads, rows, t),
                                             lambda j: (tile2(j) // n_t, 0, 0, _q_slot(tile2(j) % n_t, n_t)))
    vspec = lambda heads, rows: pl.BlockSpec((None, heads, rows, t),
                                             lambda j: (tile2(j) // n_t, 0, 0, tile2(j) % n_t))
    kspec = lambda heads, cols: pl.BlockSpec((None, heads, t, cols),
                                             lambda j: (tile2(j) // n_t, 0, tile2(j) % n_t, 0))
    out_specs = [
        qspec(HA, HEAD_PAD), kspec(HA, HEAD_PAD), vspec(HA, V_ROWS),
        qspec(HB, B_HD), kspec(HB_KV, B_HD), vspec(HB_KV, V_ROWS),
        qspec(HC, C_HD), kspec(HC_KV, C_HD), vspec(HC_KV, V_ROWS),
    ]
    sd = jax.ShapeDtypeStruct
    out_shape = [
        sd((b, HA, HEAD_PAD, s), BF16), sd((b, HA, s, HEAD_PAD), BF16), sd((b, HA, V_ROWS, s), BF16),
        sd((b, HB, B_HD, s), BF16), sd((b, HB_KV, s, B_HD), BF16), sd((b, HB_KV, V_ROWS, s), BF16),
        sd((b, HC, C_HD, s), BF16), sd((b, HC_KV, s, C_HD), BF16), sd((b, HC_KV, V_ROWS, s), BF16),
    ]
    return pl.pallas_call(
        functools.partial(_proj_kernel, n_src=len(src), n_tiles=n_t, n_flat=n_flat, scale_a=scale_a,
                          scale_h=scale_h),
        grid=(n_flat + 1,),
        in_specs=in_specs,
        out_specs=out_specs,
        out_shape=out_shape,
        scratch_shapes=[pltpu.VMEM((2, t, STD_COLS), F32), pltpu.VMEM((2, T_ROWS, t), F32)],
        compiler_params=_params(1),
        name="projection",
    )(*src, mods, *(wts[n] for n in names),
      tabs["ck"], tabs["sk"], tabs["ca"], tabs["sa"], tabs["ctq"], tabs["stq"], tabs["cta"], tabs["sta"])


def _attn_kernel(*refs, n_heads, group, has_sink):
    if has_sink:
        q_ref, qn_ref, k_ref, v_ref, sink_ref, o_ref, s_scr, m_scr = refs
    else:
        q_ref, qn_ref, k_ref, v_ref, o_ref, s_scr, m_scr = refs
    n_keys, tq = s_scr.shape[1], s_scr.shape[2]
    chunk = min(KEY_CHUNK, n_keys)
    n_chunks = n_keys // chunk
    lag = min(OUTPUT_LAG, n_chunks - 1)

    def score_chunk(head, c, mrun):
        q = qn_ref[0] if head == n_heads else q_ref[head]
        rows = slice(c * chunk, (c + 1) * chunk)
        s = _dot(k_ref[(head % n_heads) // group, rows, :], q)
        s_scr[head % 2, rows, :] = s
        part = jnp.max(s.reshape(chunk // SUBLANES, SUBLANES, tq), axis=0)
        return part if mrun is None else jnp.maximum(mrun, part)

    def output_chunk(head, c, m, acc):
        rows = slice(c * chunk, (c + 1) * chunk)
        p = jnp.exp2(s_scr[head % 2, rows, :] - m).astype(BF16)
        part = _dot(v_ref[head // group, :, rows], p)
        return part if acc is None else acc + part

    @pl.when(pl.program_id(1) == 0)
    def _():
        mrun = None
        for c in range(n_chunks):
            mrun = score_chunk(0, c, mrun)
        m_scr[...] = jnp.max(mrun, axis=0, keepdims=True)

    maxes = {0: m_scr[...]}
    mrun = acc = m = snk = None
    for g in range(n_heads * n_chunks + lag):
        if g < n_heads * n_chunks:
            hs, cs = divmod(g, n_chunks)
            mrun = score_chunk(hs + 1, cs, mrun)
            if cs == n_chunks - 1:
                maxes[hs + 1] = jnp.max(mrun, axis=0, keepdims=True)
                mrun = None
        if g >= lag:
            ho, co = divmod(g - lag, n_chunks)
            if co == 0:
                m = maxes.pop(ho)
                if has_sink:
                    snk = sink_ref[ho]
                    m = jnp.maximum(m, snk)
                acc = None
            acc = output_chunk(ho, co, m, acc)
            if co == n_chunks - 1:
                denom = acc[A_V:A_V + 1]
                if has_sink:
                    denom = denom + jnp.exp2(snk - m)
                o_ref[ho * A_V:(ho + 1) * A_V, :] = (acc[0:A_V] * (1.0 / denom)).astype(BF16)
    m_scr[...] = maxes[n_heads]


def _attention(qt, k, vt, sink2, *, n_keys, tq, q0, n_tiles, out=None):
    b, n_heads, dk, s = qt.shape
    hk = k.shape[1]
    in_specs = [
        pl.BlockSpec((None, n_heads, dk, tq), lambda bi, ti: (bi, 0, 0, ti + q0)),
        pl.BlockSpec((None, 1, dk, tq), lambda bi, ti: (bi, 0, 0, jnp.minimum(ti + 1, n_tiles - 1) + q0)),
        pl.BlockSpec((None, hk, n_keys, k.shape[3]), lambda bi, ti: (bi, 0, 0, 0)),
        pl.BlockSpec((None, hk, V_ROWS, n_keys), lambda bi, ti: (bi, 0, 0, 0)),
    ]
    args = [qt, qt, k, vt]
    if sink2 is not None:
        in_specs.append(_full(sink2.shape))
        args.append(sink2)
    aliases = {}
    if out is not None:
        in_specs.append(pl.BlockSpec(memory_space=pl.ANY))
        args.append(out)
        aliases = {len(args) - 1: 0}
    kern = functools.partial(_attn_kernel, n_heads=n_heads, group=n_heads // hk, has_sink=sink2 is not None)
    if out is not None:
        kern = _drop_last_input(kern, len(args))
    return pl.pallas_call(
        kern,
        grid=(b, n_tiles),
        in_specs=in_specs,
        out_specs=pl.BlockSpec((None, n_heads * A_V, tq), lambda bi, ti: (bi, 0, ti + q0)),
        out_shape=jax.ShapeDtypeStruct((b, n_heads * A_V, s), BF16),
        input_output_aliases=aliases,
        scratch_shapes=[pltpu.VMEM((2, n_keys, tq), F32), pltpu.VMEM((1, tq), F32)],
        compiler_params=_params(2),
        name="attention",
    )(*args)


def _drop_last_input(kern, n_in):
    def wrapped(*refs):
        return kern(*refs[:n_in - 1], *refs[n_in:])
    return wrapped


def _window_caps(n_ctx, tq):
    n_band = WINDOW + tq
    n_keys = n_ctx + n_band + WINDOW
    row = jnp.arange(n_keys, dtype=jnp.int32)[:, None]
    col = jnp.arange(tq, dtype=jnp.int32)[None, :]
    rel = row - n_ctx - WINDOW - col
    in_band = (rel >= -WINDOW) & (rel <= WINDOW)
    caps = []
    for first in (False, True):
        for last in (False, True):
            ok_lo = (row >= n_ctx + WINDOW) | (not first)
            ok_hi = (row < n_ctx + n_band) | (not last)
            valid = (row < n_ctx) | (in_band & ok_lo & ok_hi)
            caps.append(jnp.where(valid, jnp.inf, NEG).astype(F32))
    return jnp.stack(caps)


def _window_kernel(q_ref, k_ref, v_ref, sink_ref, cap_ref, o_ref, k_scr, v_scr, s_scr, *, n_ctx, n_tiles):
    tq = q_ref.shape[2]
    ti = pl.program_id(1)
    start = pl.multiple_of(n_ctx + ti * tq, LANES)
    lo = pl.multiple_of(start - WINDOW, LANES)
    nxt = pl.multiple_of(jnp.minimum(start + tq, n_ctx + (n_tiles - 1) * tq + tq - WINDOW), LANES)
    n_band = WINDOW + tq
    n_keys = n_ctx + n_band + WINDOW

    for g in range(HC_KV):
        k_scr[g, 0:n_ctx, :] = k_ref[g, 0:n_ctx, :]
        k_scr[g, n_ctx:n_ctx + n_band, :] = k_ref[g, pl.ds(lo, n_band), :]
        k_scr[g, n_ctx + n_band:n_keys, :] = k_ref[g, pl.ds(nxt, WINDOW), :]
        v_scr[g, :, 0:n_ctx] = v_ref[g, :, 0:n_ctx]
        v_scr[g, :, n_ctx:n_ctx + n_band] = v_ref[g, :, pl.ds(lo, n_band)]
        v_scr[g, :, n_ctx + n_band:n_keys] = v_ref[g, :, pl.ds(nxt, WINDOW)]

    group = HC // HC_KV
    chunk = KEY_CHUNK
    n_chunks = n_keys // chunk

    def score_chunk(hh, c, mrun):
        rows = slice(c * chunk, (c + 1) * chunk)
        s = jnp.minimum(_dot(k_scr[hh // group, rows, :], q_ref[hh]), cap_ref[rows, :])
        s_scr[hh % 2, rows, :] = s
        part = jnp.max(s.reshape(chunk // SUBLANES, SUBLANES, tq), axis=0)
        return part if mrun is None else jnp.maximum(mrun, part)

    def output_chunk(hh, c, m, acc):
        rows = slice(c * chunk, (c + 1) * chunk)
        p = jnp.exp2(s_scr[hh % 2, rows, :] - m).astype(BF16)
        part = _dot(v_scr[hh // group, :, rows], p)
        return part if acc is None else acc + part

    lag = n_chunks + WINDOW_LAG
    maxes = {}
    mrun = acc = m = snk = None
    for g in range(HC * n_chunks + lag):
        if g < HC * n_chunks:
            hs, cs = divmod(g, n_chunks)
            mrun = score_chunk(hs, cs, mrun)
            if cs == n_chunks - 1:
                maxes[hs] = jnp.max(mrun, axis=0, keepdims=True)
                mrun = None
        if g >= lag:
            ho, co = divmod(g - lag, n_chunks)
            if co == 0:
                snk = sink_ref[ho]
                m = jnp.maximum(maxes.pop(ho), snk)
                acc = None
            acc = output_chunk(ho, co, m, acc)
            if co == n_chunks - 1:
                denom = acc[C_HD:C_HD + 1] + jnp.exp2(snk - m)
                o_ref[ho * C_HD:(ho + 1) * C_HD, :] = (acc[0:C_HD] * (1.0 / denom)).astype(BF16)


def _window_attention(qt, k, vt, sink2, caps, *, n_ctx):
    b, n_heads, dk, s = qt.shape
    hk = k.shape[1]
    t = TOKEN_TILE
    n_tiles = (s - n_ctx) // t
    n_win_keys = n_ctx + WINDOW + t + WINDOW
    return pl.pallas_call(
        functools.partial(_window_kernel, n_ctx=n_ctx, n_tiles=n_tiles),
        grid=(b, n_tiles),
        in_specs=[
            pl.BlockSpec((None, n_heads, dk, t), lambda bi, ti: (bi, 0, 0, ti)),
            pl.BlockSpec((None, hk, s, dk), lambda bi, ti: (bi, 0, 0, 0)),
            pl.BlockSpec((None, hk, V_ROWS, s), lambda bi, ti: (bi, 0, 0, 0)),
            _full(sink2.shape),
            pl.BlockSpec((None, n_win_keys, t),
                         lambda bi, ti: (2 * (ti == 0).astype(jnp.int32) + (ti == n_tiles - 1).astype(jnp.int32), 0, 0)),
        ],
        out_specs=pl.BlockSpec((None, n_heads * C_HD, t), lambda bi, ti: (bi, 0, ti)),
        out_shape=jax.ShapeDtypeStruct((b, n_heads * C_HD, s), BF16),
        scratch_shapes=[
            pltpu.VMEM((hk, n_win_keys, dk), BF16), pltpu.VMEM((hk, V_ROWS, n_win_keys), BF16),
            pltpu.VMEM((2, n_win_keys, t), F32),
        ],
        compiler_params=_params(2),
        name="window_attention",
    )(qt, k, vt, sink2, caps)


def _merge_kernel(*refs, n_src):
    mod_ref, g1_ref, oa_ref, ob_ref, oc_ref, wg_ref, bg_ref, wbr_ref, wout_ref, xo_ref = refs[n_src:]
    x = _token_tile(refs[:n_src], pl.program_id(1) == 0)
    h = _modulated_norm(x, g1_ref[...], mod_ref[0:1, :], mod_ref[1:2, :])
    hb = h.astype(BF16)
    y = None
    for i, o_ref in enumerate((oa_ref, ob_ref, oc_ref)):
        cols = slice(i * D_MODEL, (i + 1) * D_MODEL)
        gate = _sigmoid(_dot(hb, wg_ref[:, cols]) + bg_ref[:, cols])
        term = gate * _dot_tn(o_ref[...], wbr_ref[i])
        y = term if y is None else y + term
    z = _dot(y.astype(BF16), wout_ref[...])
    xo_ref[...] = x + mod_ref[2:3, :] * z


def _merge(src, mods, l, wts, oa, ob, oc, *, tile0, n_tiles):
    b = src[-1].shape[0]
    s = sum(a.shape[1] for a in src)
    t = TOKEN_TILE
    feat = lambda bi, ti: (bi, 0, _q_slot(ti + tile0, s // t))
    names = ("w_gate", "b_gate", "w_branch", "w_out")
    return pl.pallas_call(
        functools.partial(_merge_kernel, n_src=len(src)),
        grid=(b, n_tiles),
        in_specs=_token_specs(src, tile0) + [_mod_spec(l, tile0), _layer(wts["g1"], l)] + [
            pl.BlockSpec((None, BRANCH_W, t), feat),
            pl.BlockSpec((None, BRANCH_W, t), feat),
            pl.BlockSpec((None, BRANCH_W, t), feat),
        ] + [_layer(wts[n], l) for n in names],
        out_specs=pl.BlockSpec((None, t, D_MODEL), lambda bi, ti: (bi, ti + tile0, 0)),
        out_shape=jax.ShapeDtypeStruct((b, s, D_MODEL), F32),
        input_output_aliases={0: 0} if len(src) == 1 else {},
        compiler_params=_params(2),
        name="merge",
    )(*src, mods, wts["g1"], oa, ob, oc, *(wts[n] for n in names))


def _ffn_kernel(xp_ref, x_ref, xn_ref, mod_ref, g2_ref, wup_ref, wconv_ref, bconv_ref, wdown_ref, gf_ref,
                o_ref, act_ref, *, tile0, n_seq_tiles, final_norm):
    t = x_ref.shape[0]
    halo = SUBLANES
    ta = pl.program_id(1) + tile0
    x = x_ref[...]
    xa = jnp.concatenate([xp_ref[...], x, xn_ref[...]], axis=0)
    h = _modulated_norm(xa, g2_ref[...], mod_ref[3:4, :], mod_ref[4:5, :])
    row = lax.broadcasted_iota(jnp.int32, (t + 2 * halo, 1), 0)
    keep_prev = (ta > 1).astype(F32)
    keep_next = ((ta > 0) & (ta < n_seq_tiles - 1)).astype(F32)
    keep = jnp.where(row < halo, keep_prev, jnp.where(row >= t + halo, keep_next, 1.0))
    hb = (h * keep).astype(BF16)

    def conv(u, cols):
        return (u[halo - 1:halo - 1 + t] * wconv_ref[0:1, cols] + u[halo:halo + t] * wconv_ref[1:2, cols]
                + u[halo + 1:halo + 1 + t] * wconv_ref[2:3, cols] + bconv_ref[:, cols])

    for c in range(D_FF // FF_CHUNK):
        ca = slice(c * FF_CHUNK, (c + 1) * FF_CHUNK)
        cg = slice(D_FF + c * FF_CHUNK, D_FF + (c + 1) * FF_CHUNK)
        a = conv(_dot(hb, wup_ref[:, ca]), ca)
        gv = conv(_dot(hb, wup_ref[:, cg]), cg)
        act_ref[:, ca] = (a * _sigmoid(a) * gv).astype(BF16)
    y = x + mod_ref[5:6, :] * _dot(act_ref[...], wdown_ref[...])
    if final_norm:
        y = y * lax.rsqrt(jnp.mean(y * y, axis=-1, keepdims=True) + EPS) * gf_ref[...]
    o_ref[...] = y


def _conv_ffn(xc, mods, l, wts, g_final, *, tile0, n_tiles, final_norm):
    b, s, _ = xc.shape
    t = TOKEN_TILE
    n_seq_tiles = s // t
    per = t // SUBLANES
    last_blk = s // SUBLANES - 1
    out_rows = n_tiles * t if final_norm else s
    out_tile0 = 0 if final_norm else tile0
    return pl.pallas_call(
        functools.partial(_ffn_kernel, tile0=tile0, n_seq_tiles=n_seq_tiles, final_norm=final_norm),
        grid=(b, n_tiles),
        in_specs=[
            pl.BlockSpec((None, SUBLANES, D_MODEL), lambda bi, ti: (bi, jnp.maximum((ti + tile0) * per - 1, 0), 0)),
            pl.BlockSpec((None, t, D_MODEL), lambda bi, ti: (bi, ti + tile0, 0)),
            pl.BlockSpec((None, SUBLANES, D_MODEL),
                         lambda bi, ti: (bi, jnp.minimum((ti + tile0 + 1) * per, last_blk), 0)),
            _mod_spec(l, tile0),
            _layer(wts["g2"], l), _layer(wts["w_up"], l), _layer(wts["w_conv"], l), _layer(wts["b_conv"], l),
            _layer(wts["w_down"], l),
            _full((1, D_MODEL)),
        ],
        out_specs=pl.BlockSpec((None, t, D_MODEL), lambda bi, ti: (bi, ti + out_tile0, 0)),
        out_shape=jax.ShapeDtypeStruct((b, out_rows, D_MODEL), F32),
        scratch_shapes=[pltpu.VMEM((t, D_FF), BF16)],
        compiler_params=_params(2),
        name="conv_ffn",
    )(xc, xc, xc, mods, wts["g2"], wts["w_up"], wts["w_conv"], wts["b_conv"], wts["w_down"], g_final)


def _partner(head_dim):
    nf = head_dim // 4
    d = np.arange(head_dim)
    a, half, f = d // (2 * nf), (d % (2 * nf)) // nf, d % nf
    return a * 2 * nf + (1 - half) * nf + f, np.where(half == 0, -1.0, 1.0).astype(np.float32)


def _rope_tables(n_ctx, n_lat):
    def full(head_dim):
        nf = head_dim // 4
        _, sign = _partner(head_dim)
        rows = n_lat // GRID_W
        row = jnp.repeat(jnp.arange(rows, dtype=F32), GRID_W)
        col = jnp.tile(jnp.arange(GRID_W, dtype=F32), rows)
        inv = ROPE_THETA ** (-jnp.arange(nf, dtype=F32) / nf)
        ang = jnp.stack([row[:, None] * inv, col[:, None] * inv], axis=1)
        c = jnp.broadcast_to(jnp.cos(ang)[:, :, None, :], (n_lat, 2, 2, nf)).reshape(n_lat, head_dim)
        s = jnp.broadcast_to(jnp.sin(ang)[:, :, None, :], (n_lat, 2, 2, nf)).reshape(n_lat, head_dim) * sign
        c = jnp.concatenate([jnp.ones((n_ctx, head_dim), F32), c], axis=0)
        s = jnp.concatenate([jnp.zeros((n_ctx, head_dim), F32), s], axis=0)
        return c, s

    c64, s64 = full(B_HD)
    c32, s32 = full(A_ROPE)
    pad_a = lambda v: jnp.pad(v, ((0, 0), (A_NOPE, LANES - A_NOPE - A_ROPE)))
    return {
        "ck": jnp.tile(c64, (1, LANES // B_HD)), "sk": jnp.tile(s64, (1, LANES // B_HD)),
        "ca": pad_a(c32), "sa": pad_a(s32),
        "ctq": c64.T, "stq": s64.T, "cta": c32.T, "sta": s32.T,
    }


def _rot_ranges(base, head_dim, n_heads):
    nf = head_dim // 4
    out = []
    for hh in range(n_heads):
        b0 = base + hh * head_dim
        for a in range(2):
            out += [(b0 + a * 2 * nf + nf, b0 + a * 2 * nf + 2 * nf), (b0 + a * 2 * nf, b0 + a * 2 * nf + nf)]
    return out


def _prepare_weights(w_in, b_gate, g_norm1, g_q_a, w_q_b, g_kv_a, w_kv_b, g_qn, g_kn, w_branch, w_out, g_norm2,
                     w_up, w_conv, b_conv, w_down):
    depth = w_in.shape[0]
    cuts = [int(v) for v in np.cumsum((0,) + IN_SIZES)]
    aq, akv, akr, bq, bk, bv, cq, ck, cv, gl = ((cuts[i], cuts[i + 1]) for i in range(len(IN_SIZES)))
    cols = lambda ranges: [w_in[:, :, lo:hi] for lo, hi in ranges]
    zeros = lambda n: [jnp.zeros((depth, D_MODEL, n), F32)]
    pad_a = lambda pieces: zeros(A_NOPE) + pieces + zeros(LANES - A_NOPE - A_ROPE)
    w_std = jnp.concatenate(
        cols([akv, bk]) + cols(_rot_ranges(bk[0], B_HD, HB_KV)) + cols([ck]) + cols(_rot_ranges(ck[0], C_HD, HC_KV))
        + pad_a(cols([akr])) + pad_a(cols(_rot_ranges(akr[0], A_ROPE, 1))), axis=2)
    w_t = jnp.swapaxes(jnp.concatenate(cols([aq, akv, bq, cq, bv, cv]), axis=2), 1, 2)
    wqb = w_q_b.reshape(depth, A_Q_RANK, HA, A_NOPE + A_ROPE)
    wqb = jnp.pad(wqb, ((0, 0), (0, 0), (0, 0), (0, HEAD_PAD - A_NOPE - A_ROPE)))
    wqb = jnp.swapaxes(wqb.reshape(depth, A_Q_RANK, HA * HEAD_PAD), 1, 2)
    wkv = w_kv_b.reshape(depth, A_KV_RANK, HA, A_NOPE + A_V)
    w_k = jnp.pad(wkv[..., :A_NOPE], ((0, 0), (0, 0), (0, 0), (0, HEAD_PAD - A_NOPE)))
    w_k = w_k.reshape(depth, A_KV_RANK, HA * HEAD_PAD)
    w_v = jnp.swapaxes(wkv[..., A_NOPE:].reshape(depth, A_KV_RANK, HA * A_V), 1, 2)
    g_kn2 = jnp.tile(g_kn, (1, LANES // B_HD))
    p64, _ = _partner(B_HD)
    p128 = np.concatenate([p64 + i * B_HD for i in range(LANES // B_HD)])
    return {
        "g1": g_norm1.reshape(depth, 1, D_MODEL), "g2": g_norm2.reshape(depth, 1, D_MODEL),
        "w_std": w_std.astype(BF16), "w_t": w_t.astype(BF16), "w_qb": wqb.astype(BF16),
        "w_k": w_k.astype(BF16), "w_v": w_v.astype(BF16),
        "g_qa_col": g_q_a.reshape(depth, A_Q_RANK, 1), "g_kv_row": g_kv_a.reshape(depth, 1, A_KV_RANK),
        "g_kv_col": g_kv_a.reshape(depth, A_KV_RANK, 1), "g_qn_col": g_qn.reshape(depth, B_HD, 1),
        "g_kn_row": g_kn2.reshape(depth, 1, LANES), "g_kn_rot_row": g_kn2[:, p128].reshape(depth, 1, LANES),
        "w_gate": w_in[:, :, gl[0]:gl[1]].astype(BF16), "b_gate": b_gate.reshape(depth, 1, N_BRANCH * D_MODEL),
        "w_branch": w_branch.astype(BF16), "w_out": w_out.astype(BF16),
        "w_up": w_up.astype(BF16), "w_conv": w_conv, "b_conv": b_conv.reshape(depth, 1, 2 * D_FF),
        "w_down": w_down.astype(BF16),
    }


def kernel(x, c, ctx, c_ctx, w_mod, b_mod, g_norm1, w_in, b_gate, g_q_a, w_q_b, g_kv_a, w_kv_b, g_qn, g_kn, sink,
           w_branch, w_out, g_norm2, w_up, w_conv, b_conv, w_down, g_final):
    b, n_lat, d = x.shape
    n_ctx = ctx.shape[1]
    depth = w_mod.shape[0]
    t = TOKEN_TILE
    assert d == D_MODEL and n_ctx == t and n_lat % t == 0 and n_lat % GRID_W == 0 and b <= MOD_ROWS // 2
    assert n_lat % QUERY_TILE == 0
    assert depth >= 2
    s = n_ctx + n_lat
    n_t = s // t
    scale_a = 1.0 / math.sqrt(A_NOPE + A_ROPE)
    scale_h = 1.0 / math.sqrt(B_HD)

    cvec = jnp.zeros((MOD_ROWS, D_MODEL), F32).at[:b].set(c).at[MOD_ROWS // 2].set(c_ctx)
    mods = _modulation(cvec, w_mod, b_mod)
    tabs = _rope_tables(n_ctx, n_lat)
    wts = _prepare_weights(w_in, b_gate, g_norm1, g_q_a, w_q_b, g_kv_a, w_kv_b, g_qn, g_kn, w_branch, w_out, g_norm2,
                           w_up, w_conv, b_conv, w_down)
    gf = g_final.reshape(1, D_MODEL)
    caps = _window_caps(n_ctx, t)
    sinks = jnp.broadcast_to((sink * LOG2E).reshape(depth, HC, 1, 1), (depth, HC, 1, t)).astype(F32)

    src = (ctx, x)
    for l in range(depth):
        last = l == depth - 1
        qa, ka, va, qb, kb, vb, qc, kc, vc = _project(src, mods, l, wts, tabs, scale_a, scale_h)
        lat = dict(n_keys=s, tq=QUERY_TILE, q0=0, n_tiles=n_lat // QUERY_TILE)
        oa = _attention(qa, ka, va, None, **lat)
        ob = _attention(qb, kb, vb, None, **lat)
        oc = _window_attention(qc, kc, vc, sinks[l], caps, n_ctx=n_ctx)
        if not last:
            cx = dict(n_keys=n_ctx, tq=t, q0=n_t - 1, n_tiles=1)
            oa = _attention(qa, ka, va, None, out=oa, **cx)
            ob = _attention(qb, kb, vb, None, out=ob, **cx)
            oc = _attention(qc, kc, vc, sinks[l], out=oc, **cx)
        tiles = dict(tile0=1, n_tiles=n_t - 1) if last else dict(tile0=0, n_tiles=n_t)
        xc = _merge(src, mods, l, wts, oa, ob, oc, **tiles)
        src = (_conv_ffn(xc, mods, l, wts, gf, final_norm=last, **tiles),)
    return src[0]
```

```python
import functools
import math

import jax
import jax.numpy as jnp
import numpy as np
from jax import lax
from jax.experimental import pallas as pl
from jax.experimental.pallas import tpu as pltpu

F32 = jnp.float32
BF16 = jnp.bfloat16

D_MODEL = 1024
GRID_W = 64
EPS = 1e-6
ROPE_THETA = 10000.0
WINDOW = 128
NEG = -1e30
HA, A_NOPE, A_ROPE, A_V, A_Q_RANK, A_KV_RANK = 8, 64, 32, 64, 256, 128
HB, HB_KV, B_HD = 8, 2, 64
HC, HC_KV, C_HD = 8, 2, 64
BRANCH_W = 512
N_BRANCH = 3
D_FF = 2816
CONV_W = 3
N_MOD = 6
IN_SIZES = (A_Q_RANK, A_KV_RANK, A_ROPE, HB * B_HD, HB_KV * B_HD, HB_KV * B_HD, HC * C_HD, HC_KV * C_HD,
            HC_KV * C_HD, N_BRANCH * D_MODEL)

LANES = 128
SUBLANES = 8
BF16_ROWS = 16
TOKEN_TILE = 256
QUERY_TILE = 512
MOD_ROWS = 16
HEAD_PAD = 128
V_ROWS = A_V + BF16_ROWS
FF_CHUNK = 256
KEY_CHUNK = 256
OUTPUT_LAG = 2
WINDOW_LAG = 2
VMEM_LIMIT = 56 * 1024 * 1024
LOG2E = 1.4426950408889634


def _dot(a, b):
    return jnp.dot(a, b, preferred_element_type=F32)


def _dot_nt(a, b):
    return lax.dot_general(a, b, (((1,), (1,)), ((), ())), preferred_element_type=F32)


def _dot_tn(a, b):
    return lax.dot_general(a, b, (((0,), (0,)), ((), ())), preferred_element_type=F32)


def _sigmoid(x):
    return 1.0 / (1.0 + jnp.exp(-x))


def _modulated_norm(x, g, shift, scale):
    ms = jnp.mean(x * x, axis=-1, keepdims=True)
    return (x * lax.rsqrt(ms + EPS) * g) * (1.0 + scale) + shift


def _params(n_grid):
    return pltpu.CompilerParams(dimension_semantics=("arbitrary",) * n_grid, vmem_limit_bytes=VMEM_LIMIT)


def _full(shape):
    nd = len(shape)
    return pl.BlockSpec(shape, lambda *_: (0,) * nd)


def _layer(arr, l):
    nd = arr.ndim - 1
    return pl.BlockSpec((None,) + arr.shape[1:], lambda *_: (l,) + (0,) * nd, pipeline_mode=pl.Buffered(1))


def _mod_spec(l, tile0):
    return pl.BlockSpec((None, None, N_MOD, D_MODEL),
                        lambda bi, ti: (l, jnp.where(ti + tile0 == 0, MOD_ROWS // 2, bi), 0, 0))


def _token_specs(src, tile0):
    t = TOKEN_TILE
    if len(src) == 1:
        return [pl.BlockSpec((None, t, D_MODEL), lambda bi, ti: (bi, ti + tile0, 0))]
    assert tile0 == 0
    return [pl.BlockSpec((None, t, D_MODEL), lambda bi, ti: (bi, 0, 0)),
            pl.BlockSpec((None, t, D_MODEL), lambda bi, ti: (bi, jnp.maximum(ti - 1, 0), 0))]


def _q_slot(tile, n_seq_tiles):
    return (tile + n_seq_tiles - 1) % n_seq_tiles


def _token_tile(refs, is_ctx):
    if len(refs) == 1:
        return refs[0][...]
    return jnp.where(is_ctx, refs[0][...], refs[1][...])


def _mod_kernel(c_ref, w_ref, b_ref, o_ref):
    c = c_ref[...]
    a = c * _sigmoid(c)
    o_ref[...] = jnp.dot(a, w_ref[...], precision=lax.Precision.HIGHEST, preferred_element_type=F32) + b_ref[...]


def _modulation(cvec, w_mod, b_mod):
    depth = w_mod.shape[0]
    n_col = N_MOD * D_MODEL
    out = pl.pallas_call(
        _mod_kernel,
        grid=(depth, N_MOD),
        in_specs=[
            pl.BlockSpec((MOD_ROWS, D_MODEL), lambda l, j: (0, 0)),
            pl.BlockSpec((None, D_MODEL, D_MODEL), lambda l, j: (l, 0, j)),
            pl.BlockSpec((None, 1, D_MODEL), lambda l, j: (l, 0, j)),
        ],
        out_specs=pl.BlockSpec((None, MOD_ROWS, D_MODEL), lambda l, j: (l, 0, j)),
        out_shape=jax.ShapeDtypeStruct((depth, MOD_ROWS, n_col), F32),
        compiler_params=_params(2),
        name="modulation",
    )(cvec, w_mod, b_mod.reshape(depth, 1, n_col))
    return out.reshape(depth, MOD_ROWS, N_MOD, D_MODEL)


STD_AKV, STD_BK, STD_BKR, STD_CK, STD_CKR, STD_KR, STD_KRR = (i * LANES for i in range(7))
STD_COLS = 7 * LANES
T_AQ = 0
T_AKV = T_AQ + A_Q_RANK
T_BQ = T_AKV + A_KV_RANK
T_CQ = T_BQ + HB * B_HD
T_BV = T_CQ + HC * C_HD
T_CV = T_BV + HB_KV * B_HD
T_ROWS = T_CV + HC_KV * C_HD


def _swap_halves(x, nf):
    parts = []
    for a in range(2):
        base = a * 2 * nf
        parts += [x[base + nf:base + 2 * nf], x[base:base + nf]]
    return jnp.concatenate(parts, axis=0)


def _ones_rows(t):
    row = lax.broadcasted_iota(jnp.int32, (BF16_ROWS, t), 0)
    return jnp.where(row == 0, 1.0, 0.0).astype(BF16)


def _proj_kernel(*refs, n_src, n_tiles, n_flat, scale_a, scale_h):
    (mod_ref, g1_ref, wstd_ref, wt_ref, wqb_ref, wk_ref, wv_ref,
     gqa_ref, gkvr_ref, gkvc_ref, gqn_ref, gknr_ref, gknrr_ref,
     ck_ref, sk_ref, ca_ref, sa_ref, ctq_ref, stq_ref, cta_ref, sta_ref,
     qa_ref, ka_ref, va_ref, qb_ref, kb_ref, vb_ref, qc_ref, kc_ref, vc_ref, ps_scr, pt_scr) = refs[n_src:]
    t = TOKEN_TILE
    j = pl.program_id(0)

    @pl.when(j == 0)
    def _():
        ps_scr[1] = jnp.zeros(ps_scr.shape[1:], F32)
        pt_scr[1] = jnp.zeros(pt_scr.shape[1:], F32)

    def stage1(slot):
        is_ctx = jnp.minimum(j, n_flat - 1) % n_tiles == 0
        h = _modulated_norm(_token_tile(refs[:n_src], is_ctx), g1_ref[...], mod_ref[0:1, :], mod_ref[1:2, :])
        hb = h.astype(BF16)
        ps_scr[slot] = _dot(hb, wstd_ref[...])
        pt_scr[slot] = _dot_nt(wt_ref[...], hb)

    def stage2(slot):
        ps, pt = ps_scr.at[slot], pt_scr.at[slot]
        ones = _ones_rows(t)

        aq = pt[T_AQ:T_AQ + A_Q_RANK]
        aqn = aq * lax.rsqrt(jnp.mean(aq * aq, axis=0, keepdims=True) + EPS) * gqa_ref[...]
        qt = _dot(wqb_ref[...], aqn.astype(BF16))
        cta, sta = cta_ref[...], sta_ref[...]
        qs = scale_a * LOG2E
        for hh in range(HA):
            base = hh * HEAD_PAD
            qa_ref[hh, 0:A_NOPE, :] = (qt[base:base + A_NOPE] * qs).astype(BF16)
            r = qt[base + A_NOPE:base + A_NOPE + A_ROPE]
            rr = r * cta + _swap_halves(r, A_ROPE // 4) * sta
            qa_ref[hh, A_NOPE:A_NOPE + A_ROPE, :] = (rr * qs).astype(BF16)
            qa_ref[hh, A_NOPE + A_ROPE:HEAD_PAD, :] = jnp.zeros((HEAD_PAD - A_NOPE - A_ROPE, t), BF16)

        akv = ps[:, STD_AKV:STD_AKV + LANES]
        akvn = akv * lax.rsqrt(jnp.mean(akv * akv, axis=-1, keepdims=True) + EPS) * gkvr_ref[...]
        kn = _dot(akvn.astype(BF16), wk_ref[...])
        kr = ps[:, STD_KR:STD_KR + LANES] * ca_ref[...] + ps[:, STD_KRR:STD_KRR + LANES] * sa_ref[...]
        for hh in range(HA):
            ka_ref[hh] = (kn[:, hh * HEAD_PAD:(hh + 1) * HEAD_PAD] + kr).astype(BF16)

        akvt = pt[T_AKV:T_AKV + A_KV_RANK]
        akvtn = akvt * lax.rsqrt(jnp.mean(akvt * akvt, axis=0, keepdims=True) + EPS) * gkvc_ref[...]
        vt = _dot(wv_ref[...], akvtn.astype(BF16))
        for hh in range(HA):
            va_ref[hh, 0:A_V, :] = vt[hh * A_V:(hh + 1) * A_V].astype(BF16)
            va_ref[hh, A_V:V_ROWS, :] = ones

        ctq, stq = ctq_ref[...], stq_ref[...]
        ck, sk = ck_ref[...], sk_ref[...]
        qsh = scale_h * LOG2E
        lane = lax.broadcasted_iota(jnp.int32, (t, LANES), 1)
        first = lane < B_HD

        for hh in range(HB):
            blk = pt[T_BQ + hh * B_HD:T_BQ + (hh + 1) * B_HD]
            y = blk * lax.rsqrt(jnp.mean(blk * blk, axis=0, keepdims=True) + EPS) * gqn_ref[...]
            qb_ref[hh] = ((y * ctq + _swap_halves(y, B_HD // 4) * stq) * qsh).astype(BF16)
            blk = pt[T_CQ + hh * C_HD:T_CQ + (hh + 1) * C_HD]
            qc_ref[hh] = ((blk * ctq + _swap_halves(blk, C_HD // 4) * stq) * qsh).astype(BF16)

        bk = ps[:, STD_BK:STD_BK + LANES]
        sq = bk * bk
        s0 = jnp.sum(jnp.where(first, sq, 0.0), axis=-1, keepdims=True)
        s1 = jnp.sum(jnp.where(first, 0.0, sq), axis=-1, keepdims=True)
        rk = lax.rsqrt(jnp.where(first, s0, s1) * (1.0 / B_HD) + EPS)
        kb = (bk * rk * gknr_ref[...]) * ck + (ps[:, STD_BKR:STD_BKR + LANES] * rk * gknrr_ref[...]) * sk
        kc = ps[:, STD_CK:STD_CK + LANES] * ck + ps[:, STD_CKR:STD_CKR + LANES] * sk
        for g in range(HB_KV):
            kb_ref[g] = kb[:, g * B_HD:(g + 1) * B_HD].astype(BF16)
            kc_ref[g] = kc[:, g * C_HD:(g + 1) * C_HD].astype(BF16)
            vb_ref[g, 0:B_HD, :] = pt[T_BV + g * B_HD:T_BV + (g + 1) * B_HD].astype(BF16)
            vb_ref[g, B_HD:V_ROWS, :] = ones
            vc_ref[g, 0:C_HD, :] = pt[T_CV + g * C_HD:T_CV + (g + 1) * C_HD].astype(BF16)
            vc_ref[g, C_HD:V_ROWS, :] = ones

    @pl.when(j % 2 == 0)
    def _():
        stage1(0)
        stage2(1)

    @pl.when(j % 2 == 1)
    def _():
        stage1(1)
        stage2(0)


def _project(src, mods, l, wts, tabs, scale_a, scale_h):
    b = src[-1].shape[0]
    s = sum(a.shape[1] for a in src)
    t = TOKEN_TILE
    n_t = s // t
    n_flat = b * n_t
    tile1 = lambda j: jnp.minimum(j, n_flat - 1)
    tile2 = lambda j: jnp.maximum(j - 1, 0)
    if len(src) == 1:
        tok_specs = [pl.BlockSpec((None, t, D_MODEL), lambda j: (tile1(j) // n_t, tile1(j) % n_t, 0))]
    else:
        tok_specs = [pl.BlockSpec((None, t, D_MODEL), lambda j: (tile1(j) // n_t, 0, 0)),
                     pl.BlockSpec((None, t, D_MODEL),
                                  lambda j: (tile1(j) // n_t, jnp.maximum(tile1(j) % n_t - 1, 0), 0))]
    mod_spec = pl.BlockSpec(
        (None, None, N_MOD, D_MODEL),
        lambda j: (l, jnp.where(tile1(j) % n_t == 0, MOD_ROWS // 2, tile1(j) // n_t), 0, 0))
    tok2 = lambda j: (tile2(j) % n_t, 0)
    feat2 = lambda j: (0, tile2(j) % n_t)
    names = ("g1", "w_std", "w_t", "w_qb", "w_k", "w_v", "g_qa_col", "g_kv_row", "g_kv_col", "g_qn_col", "g_kn_row",
             "g_kn_rot_row")
    in_specs = tok_specs + [mod_spec] + [_layer(wts[n], l) for n in names] + [
        pl.BlockSpec((t, LANES), tok2), pl.BlockSpec((t, LANES), tok2),
        pl.BlockSpec((t, LANES), tok2), pl.BlockSpec((t, LANES), tok2),
        pl.BlockSpec((B_HD, t), feat2), pl.BlockSpec((B_HD, t), feat2),
        pl.BlockSpec((A_ROPE, t), feat2), pl.BlockSpec((A_ROPE, t), feat2),
    ]
    qspec = lambda heads, rows: pl.BlockSpec((None, heads, rows, t),
                                             lambda j: (tile2(j) // n_t, 0, 0, _q_slot(tile2(j) % n_t, n_t)))
    vspec = lambda heads, rows: pl.BlockSpec((None, heads, rows, t),
                                             lambda j: (tile2(j) // n_t, 0, 0, tile2(j) % n_t))
    kspec = lambda heads, cols: pl.BlockSpec((None, heads, t, cols),
                                             lambda j: (tile2(j) // n_t, 0, tile2(j) % n_t, 0))
    out_specs = [
        qspec(HA, HEAD_PAD), kspec(HA, HEAD_PAD), vspec(HA, V_ROWS),
        qspec(HB, B_HD), kspec(HB_KV, B_HD), vspec(HB_KV, V_ROWS),
        qspec(HC, C_HD), kspec(HC_KV, C_HD), vspec(HC_KV, V_ROWS),
    ]
    sd = jax.ShapeDtypeStruct
    out_shape = [
        sd((b, HA, HEAD_PAD, s), BF16), sd((b, HA, s, HEAD_PAD), BF16), sd((b, HA, V_ROWS, s), BF16),
        sd((b, HB, B_HD, s), BF16), sd((b, HB_KV, s, B_HD), BF16), sd((b, HB_KV, V_ROWS, s), BF16),
        sd((b, HC, C_HD, s), BF16), sd((b, HC_KV, s, C_HD), BF16), sd((b, HC_KV, V_ROWS, s), BF16),
    ]
    return pl.pallas_call(
        functools.partial(_proj_kernel, n_src=len(src), n_tiles=n_t, n_flat=n_flat, scale_a=scale_a,
                          scale_h=scale_h),
        grid=(n_flat + 1,),
        in_specs=in_specs,
        out_specs=out_specs,
        out_shape=out_shape,
        scratch_shapes=[pltpu.VMEM((2, t, STD_COLS), F32), pltpu.VMEM((2, T_ROWS, t), F32)],
        compiler_params=_params(1),
        name="projection",
    )(*src, mods, *(wts[n] for n in names),
      tabs["ck"], tabs["sk"], tabs["ca"], tabs["sa"], tabs["ctq"], tabs["stq"], tabs["cta"], tabs["sta"])


def _attn_kernel(*refs, n_heads, group, has_sink):
    if has_sink:
        q_ref, qn_ref, k_ref, v_ref, sink_ref, o_ref, s_scr, m_scr = refs
    else:
        q_ref, qn_ref, k_ref, v_ref, o_ref, s_scr, m_scr = refs
    n_keys, tq = s_scr.shape[1], s_scr.shape[2]
    chunk = min(KEY_CHUNK, n_keys)
    n_chunks = n_keys // chunk
    lag = min(OUTPUT_LAG, n_chunks - 1)

    def score_chunk(head, c, mrun):
        q = qn_ref[0] if head == n_heads else q_ref[head]
        rows = slice(c * chunk, (c + 1) * chunk)
        s = _dot(k_ref[(head % n_heads) // group, rows, :], q)
        s_scr[head % 2, rows, :] = s
        part = jnp.max(s.reshape(chunk // SUBLANES, SUBLANES, tq), axis=0)
        return part if mrun is None else jnp.maximum(mrun, part)

    def output_chunk(head, c, m, acc):
        rows = slice(c * chunk, (c + 1) * chunk)
        p = jnp.exp2(s_scr[head % 2, rows, :] - m).astype(BF16)
        part = _dot(v_ref[head // group, :, rows], p)
        return part if acc is None else acc + part

    @pl.when(pl.program_id(1) == 0)
    def _():
        mrun = None
        for c in range(n_chunks):
            mrun = score_chunk(0, c, mrun)
        m_scr[...] = jnp.max(mrun, axis=0, keepdims=True)

    maxes = {0: m_scr[...]}
    mrun = acc = m = snk = None
    for g in range(n_heads * n_chunks + lag):
        if g < n_heads * n_chunks:
            hs, cs = divmod(g, n_chunks)
            mrun = score_chunk(hs + 1, cs, mrun)
            if cs == n_chunks - 1:
                maxes[hs + 1] = jnp.max(mrun, axis=0, keepdims=True)
                mrun = None
        if g >= lag:
            ho, co = divmod(g - lag, n_chunks)
            if co == 0:
                m = maxes.pop(ho)
                if has_sink:
                    snk = sink_ref[ho]
                    m = jnp.maximum(m, snk)
                acc = None
            acc = output_chunk(ho, co, m, acc)
            if co == n_chunks - 1:
                denom = acc[A_V:A_V + 1]
                if has_sink:
                    denom = denom + jnp.exp2(snk - m)
                o_ref[ho * A_V:(ho + 1) * A_V, :] = (acc[0:A_V] * (1.0 / denom)).astype(BF16)
    m_scr[...] = maxes[n_heads]


def _attention(qt, k, vt, sink2, *, n_keys, tq, q0, n_tiles, out=None):
    b, n_heads, dk, s = qt.shape
    hk = k.shape[1]
    in_specs = [
        pl.BlockSpec((None, n_heads, dk, tq), lambda bi, ti: (bi, 0, 0, ti + q0)),
        pl.BlockSpec((None, 1, dk, tq), lambda bi, ti: (bi, 0, 0, jnp.minimum(ti + 1, n_tiles - 1) + q0)),
        pl.BlockSpec((None, hk, n_keys, k.shape[3]), lambda bi, ti: (bi, 0, 0, 0)),
        pl.BlockSpec((None, hk, V_ROWS, n_keys), lambda bi, ti: (bi, 0, 0, 0)),
    ]
    args = [qt, qt, k, vt]
    if sink2 is not None:
        in_specs.append(_full(sink2.shape))
        args.append(sink2)
    aliases = {}
    if out is not None:
        in_specs.append(pl.BlockSpec(memory_space=pl.ANY))
        args.append(out)
        aliases = {len(args) - 1: 0}
    kern = functools.partial(_attn_kernel, n_heads=n_heads, group=n_heads // hk, has_sink=sink2 is not None)
    if out is not None:
        kern = _drop_last_input(kern, len(args))
    return pl.pallas_call(
        kern,
        grid=(b, n_tiles),
        in_specs=in_specs,
        out_specs=pl.BlockSpec((None, n_heads * A_V, tq), lambda bi, ti: (bi, 0, ti + q0)),
        out_shape=jax.ShapeDtypeStruct((b, n_heads * A_V, s), BF16),
        input_output_aliases=aliases,
        scratch_shapes=[pltpu.VMEM((2, n_keys, tq), F32), pltpu.VMEM((1, tq), F32)],
        compiler_params=_params(2),
        name="attention",
    )(*args)


def _drop_last_input(kern, n_in):
    def wrapped(*refs):
        return kern(*refs[:n_in - 1], *refs[n_in:])
    return wrapped


def _window_caps(n_ctx, tq):
    n_band = WINDOW + tq
    n_keys = n_ctx + n_band + WINDOW
    row = jnp.arange(n_keys, dtype=jnp.int32)[:, None]
    col = jnp.arange(tq, dtype=jnp.int32)[None, :]
    rel = row - n_ctx - WINDOW - col
    in_band = (rel >= -WINDOW) & (rel <= WINDOW)
    caps = []
    for first in (False, True):
        for last in (False, True):
            ok_lo = (row >= n_ctx + WINDOW) | (not first)
            ok_hi = (row < n_ctx + n_band) | (not last)
            valid = (row < n_ctx) | (in_band & ok_lo & ok_hi)
            caps.append(jnp.where(valid, jnp.inf, NEG).astype(F32))
    return jnp.stack(caps)


def _window_kernel(q_ref, k_ref, v_ref, sink_ref, cap_ref, o_ref, k_scr, v_scr, s_scr, *, n_ctx, n_tiles):
    tq = q_ref.shape[2]
    ti = pl.program_id(1)
    start = pl.multiple_of(n_ctx + ti * tq, LANES)
    lo = pl.multiple_of(start - WINDOW, LANES)
    nxt = pl.multiple_of(jnp.minimum(start + tq, n_ctx + (n_tiles - 1) * tq + tq - WINDOW), LANES)
    n_band = WINDOW + tq
    n_keys = n_ctx + n_band + WINDOW

    for g in range(HC_KV):
        k_scr[g, 0:n_ctx, :] = k_ref[g, 0:n_ctx, :]
        k_scr[g, n_ctx:n_ctx + n_band, :] = k_ref[g, pl.ds(lo, n_band), :]
        k_scr[g, n_ctx + n_band:n_keys, :] = k_ref[g, pl.ds(nxt, WINDOW), :]
        v_scr[g, :, 0:n_ctx] = v_ref[g, :, 0:n_ctx]
        v_scr[g, :, n_ctx:n_ctx + n_band] = v_ref[g, :, pl.ds(lo, n_band)]
        v_scr[g, :, n_ctx + n_band:n_keys] = v_ref[g, :, pl.ds(nxt, WINDOW)]

    group = HC // HC_KV
    chunk = KEY_CHUNK
    n_chunks = n_keys // chunk

    def score_chunk(hh, c, mrun):
        rows = slice(c * chunk, (c + 1) * chunk)
        s = jnp.minimum(_dot(k_scr[hh // group, rows, :], q_ref[hh]), cap_ref[rows, :])
        s_scr[hh % 2, rows, :] = s
        part = jnp.max(s.reshape(chunk // SUBLANES, SUBLANES, tq), axis=0)
        return part if mrun is None else jnp.maximum(mrun, part)

    def output_chunk(hh, c, m, acc):
        rows = slice(c * chunk, (c + 1) * chunk)
        p = jnp.exp2(s_scr[hh % 2, rows, :] - m).astype(BF16)
        part = _dot(v_scr[hh // group, :, rows], p)
        return part if acc is None else acc + part

    lag = n_chunks + WINDOW_LAG
    maxes = {}
    mrun = acc = m = snk = None
    for g in range(HC * n_chunks + lag):
        if g < HC * n_chunks:
            hs, cs = divmod(g, n_chunks)
            mrun = score_chunk(hs, cs, mrun)
            if cs == n_chunks - 1:
                maxes[hs] = jnp.max(mrun, axis=0, keepdims=True)
                mrun = None
        if g >= lag:
            ho, co = divmod(g - lag, n_chunks)
            if co == 0:
                snk = sink_ref[ho]
                m = jnp.maximum(maxes.pop(ho), snk)
                acc = None
            acc = output_chunk(ho, co, m, acc)
            if co == n_chunks - 1:
                denom = acc[C_HD:C_HD + 1] + jnp.exp2(snk - m)
                o_ref[ho * C_HD:(ho + 1) * C_HD, :] = (acc[0:C_HD] * (1.0 / denom)).astype(BF16)


def _window_attention(qt, k, vt, sink2, caps, *, n_ctx):
    b, n_heads, dk, s = qt.shape
    hk = k.shape[1]
    t = TOKEN_TILE
    n_tiles = (s - n_ctx) // t
    n_win_keys = n_ctx + WINDOW + t + WINDOW
    return pl.pallas_call(
        functools.partial(_window_kernel, n_ctx=n_ctx, n_tiles=n_tiles),
        grid=(b, n_tiles),
        in_specs=[
            pl.BlockSpec((None, n_heads, dk, t), lambda bi, ti: (bi, 0, 0, ti)),
            pl.BlockSpec((None, hk, s, dk), lambda bi, ti: (bi, 0, 0, 0)),
            pl.BlockSpec((None, hk, V_ROWS, s), lambda bi, ti: (bi, 0, 0, 0)),
            _full(sink2.shape),
            pl.BlockSpec((None, n_win_keys, t),
                         lambda bi, ti: (2 * (ti == 0).astype(jnp.int32) + (ti == n_tiles - 1).astype(jnp.int32), 0, 0)),
        ],
        out_specs=pl.BlockSpec((None, n_heads * C_HD, t), lambda bi, ti: (bi, 0, ti)),
        out_shape=jax.ShapeDtypeStruct((b, n_heads * C_HD, s), BF16),
        scratch_shapes=[
            pltpu.VMEM((hk, n_win_keys, dk), BF16), pltpu.VMEM((hk, V_ROWS, n_win_keys), BF16),
            pltpu.VMEM((2, n_win_keys, t), F32),
        ],
        compiler_params=_params(2),
        name="window_attention",
    )(qt, k, vt, sink2, caps)


def _merge_kernel(*refs, n_src):
    mod_ref, g1_ref, oa_ref, ob_ref, oc_ref, wg_ref, bg_ref, wbr_ref, wout_ref, xo_ref = refs[n_src:]
    x = _token_tile(refs[:n_src], pl.program_id(1) == 0)
    h = _modulated_norm(x, g1_ref[...], mod_ref[0:1, :], mod_ref[1:2, :])
    hb = h.astype(BF16)
    y = None
    for i, o_ref in enumerate((oa_ref, ob_ref, oc_ref)):
        cols = slice(i * D_MODEL, (i + 1) * D_MODEL)
        gate = _sigmoid(_dot(hb, wg_ref[:, cols]) + bg_ref[:, cols])
        term = gate * _dot_tn(o_ref[...], wbr_ref[i])
        y = term if y is None else y + term
    z = _dot(y.astype(BF16), wout_ref[...])
    xo_ref[...] = x + mod_ref[2:3, :] * z


def _merge(src, mods, l, wts, oa, ob, oc, *, tile0, n_tiles):
    b = src[-1].shape[0]
    s = sum(a.shape[1] for a in src)
    t = TOKEN_TILE
    feat = lambda bi, ti: (bi, 0, _q_slot(ti + tile0, s // t))
    names = ("w_gate", "b_gate", "w_branch", "w_out")
    return pl.pallas_call(
        functools.partial(_merge_kernel, n_src=len(src)),
        grid=(b, n_tiles),
        in_specs=_token_specs(src, tile0) + [_mod_spec(l, tile0), _layer(wts["g1"], l)] + [
            pl.BlockSpec((None, BRANCH_W, t), feat),
            pl.BlockSpec((None, BRANCH_W, t), feat),
            pl.BlockSpec((None, BRANCH_W, t), feat),
        ] + [_layer(wts[n], l) for n in names],
        out_specs=pl.BlockSpec((None, t, D_MODEL), lambda bi, ti: (bi, ti + tile0, 0)),
        out_shape=jax.ShapeDtypeStruct((b, s, D_MODEL), F32),
        input_output_aliases={0: 0} if len(src) == 1 else {},
        compiler_params=_params(2),
        name="merge",
    )(*src, mods, wts["g1"], oa, ob, oc, *(wts[n] for n in names))


def _ffn_kernel(xp_ref, x_ref, xn_ref, mod_ref, g2_ref, wup_ref, wconv_ref, bconv_ref, wdown_ref, gf_ref,
                o_ref, act_ref, *, tile0, n_seq_tiles, final_norm):
    t = x_ref.shape[0]
    halo = SUBLANES
    ta = pl.program_id(1) + tile0
    x = x_ref[...]
    xa = jnp.concatenate([xp_ref[...], x, xn_ref[...]], axis=0)
    h = _modulated_norm(xa, g2_ref[...], mod_ref[3:4, :], mod_ref[4:5, :])
    row = lax.broadcasted_iota(jnp.int32, (t + 2 * halo, 1), 0)
    keep_prev = (ta > 1).astype(F32)
    keep_next = ((ta > 0) & (ta < n_seq_tiles - 1)).astype(F32)
    keep = jnp.where(row < halo, keep_prev, jnp.where(row >= t + halo, keep_next, 1.0))
    hb = (h * keep).astype(BF16)

    def conv(u, cols):
        return (u[halo - 1:halo - 1 + t] * wconv_ref[0:1, cols] + u[halo:halo + t] * wconv_ref[1:2, cols]
                + u[halo + 1:halo + 1 + t] * wconv_ref[2:3, cols] + bconv_ref[:, cols])

    for c in range(D_FF // FF_CHUNK):
        ca = slice(c * FF_CHUNK, (c + 1) * FF_CHUNK)
        cg = slice(D_FF + c * FF_CHUNK, D_FF + (c + 1) * FF_CHUNK)
        a = conv(_dot(hb, wup_ref[:, ca]), ca)
        gv = conv(_dot(hb, wup_ref[:, cg]), cg)
        act_ref[:, ca] = (a * _sigmoid(a) * gv).astype(BF16)
    y = x + mod_ref[5:6, :] * _dot(act_ref[...], wdown_ref[...])
    if final_norm:
        y = y * lax.rsqrt(jnp.mean(y * y, axis=-1, keepdims=True) + EPS) * gf_ref[...]
    o_ref[...] = y


def _conv_ffn(xc, mods, l, wts, g_final, *, tile0, n_tiles, final_norm):
    b, s, _ = xc.shape
    t = TOKEN_TILE
    n_seq_tiles = s // t
    per = t // SUBLANES
    last_blk = s // SUBLANES - 1
    out_rows = n_tiles * t if final_norm else s
    out_tile0 = 0 if final_norm else tile0
    return pl.pallas_call(
        functools.partial(_ffn_kernel, tile0=tile0, n_seq_tiles=n_seq_tiles, final_norm=final_norm),
        grid=(b, n_tiles),
        in_specs=[
            pl.BlockSpec((None, SUBLANES, D_MODEL), lambda bi, ti: (bi, jnp.maximum((ti + tile0) * per - 1, 0), 0)),
            pl.BlockSpec((None, t, D_MODEL), lambda bi, ti: (bi, ti + tile0, 0)),
            pl.BlockSpec((None, SUBLANES, D_MODEL),
                         lambda bi, ti: (bi, jnp.minimum((ti + tile0 + 1) * per, last_blk), 0)),
            _mod_spec(l, tile0),
            _layer(wts["g2"], l), _layer(wts["w_up"], l), _layer(wts["w_conv"], l), _layer(wts["b_conv"], l),
            _layer(wts["w_down"], l),
            _full((1, D_MODEL)),
        ],
        out_specs=pl.BlockSpec((None, t, D_MODEL), lambda bi, ti: (bi, ti + out_tile0, 0)),
        out_shape=jax.ShapeDtypeStruct((b, out_rows, D_MODEL), F32),
        scratch_shapes=[pltpu.VMEM((t, D_FF), BF16)],
        compiler_params=_params(2),
        name="conv_ffn",
    )(xc, xc, xc, mods, wts["g2"], wts["w_up"], wts["w_conv"], wts["b_conv"], wts["w_down"], g_final)


def _partner(head_dim):
    nf = head_dim // 4
    d = np.arange(head_dim)
    a, half, f = d // (2 * nf), (d % (2 * nf)) // nf, d % nf
    return a * 2 * nf + (1 - half) * nf + f, np.where(half == 0, -1.0, 1.0).astype(np.float32)


def _rope_tables(n_ctx, n_lat):
    def full(head_dim):
        nf = head_dim // 4
        _, sign = _partner(head_dim)
        rows = n_lat // GRID_W
        row = jnp.repeat(jnp.arange(rows, dtype=F32), GRID_W)
        col = jnp.tile(jnp.arange(GRID_W, dtype=F32), rows)
        inv = ROPE_THETA ** (-jnp.arange(nf, dtype=F32) / nf)
        ang = jnp.stack([row[:, None] * inv, col[:, None] * inv], axis=1)
        c = jnp.broadcast_to(jnp.cos(ang)[:, :, None, :], (n_lat, 2, 2, nf)).reshape(n_lat, head_dim)
        s = jnp.broadcast_to(jnp.sin(ang)[:, :, None, :], (n_lat, 2, 2, nf)).reshape(n_lat, head_dim) * sign
        c = jnp.concatenate([jnp.ones((n_ctx, head_dim), F32), c], axis=0)
        s = jnp.concatenate([jnp.zeros((n_ctx, head_dim), F32), s], axis=0)
        return c, s

    c64, s64 = full(B_HD)
    c32, s32 = full(A_ROPE)
    pad_a = lambda v: jnp.pad(v, ((0, 0), (A_NOPE, LANES - A_NOPE - A_ROPE)))
    return {
        "ck": jnp.tile(c64, (1, LANES // B_HD)), "sk": jnp.tile(s64, (1, LANES // B_HD)),
        "ca": pad_a(c32), "sa": pad_a(s32),
        "ctq": c64.T, "stq": s64.T, "cta": c32.T, "sta": s32.T,
    }


def _rot_ranges(base, head_dim, n_heads):
    nf = head_dim // 4
    out = []
    for hh in range(n_heads):
        b0 = base + hh * head_dim
        for a in range(2):
            out += [(b0 + a * 2 * nf + nf, b0 + a * 2 * nf + 2 * nf), (b0 + a * 2 * nf, b0 + a * 2 * nf + nf)]
    return out


def _prepare_weights(w_in, b_gate, g_norm1, g_q_a, w_q_b, g_kv_a, w_kv_b, g_qn, g_kn, w_branch, w_out, g_norm2,
                     w_up, w_conv, b_conv, w_down):
    depth = w_in.shape[0]
    cuts = [int(v) for v in np.cumsum((0,) + IN_SIZES)]
    aq, akv, akr, bq, bk, bv, cq, ck, cv, gl = ((cuts[i], cuts[i + 1]) for i in range(len(IN_SIZES)))
    cols = lambda ranges: [w_in[:, :, lo:hi] for lo, hi in ranges]
    zeros = lambda n: [jnp.zeros((depth, D_MODEL, n), F32)]
    pad_a = lambda pieces: zeros(A_NOPE) + pieces + zeros(LANES - A_NOPE - A_ROPE)
    w_std = jnp.concatenate(
        cols([akv, bk]) + cols(_rot_ranges(bk[0], B_HD, HB_KV)) + cols([ck]) + cols(_rot_ranges(ck[0], C_HD, HC_KV))
        + pad_a(cols([akr])) + pad_a(cols(_rot_ranges(akr[0], A_ROPE, 1))), axis=2)
    w_t = jnp.swapaxes(jnp.concatenate(cols([aq, akv, bq, cq, bv, cv]), axis=2), 1, 2)
    wqb = w_q_b.reshape(depth, A_Q_RANK, HA, A_NOPE + A_ROPE)
    wqb = jnp.pad(wqb, ((0, 0), (0, 0), (0, 0), (0, HEAD_PAD - A_NOPE - A_ROPE)))
    wqb = jnp.swapaxes(wqb.reshape(depth, A_Q_RANK, HA * HEAD_PAD), 1, 2)
    wkv = w_kv_b.reshape(depth, A_KV_RANK, HA, A_NOPE + A_V)
    w_k = jnp.pad(wkv[..., :A_NOPE], ((0, 0), (0, 0), (0, 0), (0, HEAD_PAD - A_NOPE)))
    w_k = w_k.reshape(depth, A_KV_RANK, HA * HEAD_PAD)
    w_v = jnp.swapaxes(wkv[..., A_NOPE:].reshape(depth, A_KV_RANK, HA * A_V), 1, 2)
    g_kn2 = jnp.tile(g_kn, (1, LANES // B_HD))
    p64, _ = _partner(B_HD)
    p128 = np.concatenate([p64 + i * B_HD for i in range(LANES // B_HD)])
    return {
        "g1": g_norm1.reshape(depth, 1, D_MODEL), "g2": g_norm2.reshape(depth, 1, D_MODEL),
        "w_std": w_std.astype(BF16), "w_t": w_t.astype(BF16), "w_qb": wqb.astype(BF16),
        "w_k": w_k.astype(BF16), "w_v": w_v.astype(BF16),
        "g_qa_col": g_q_a.reshape(depth, A_Q_RANK, 1), "g_kv_row": g_kv_a.reshape(depth, 1, A_KV_RANK),
        "g_kv_col": g_kv_a.reshape(depth, A_KV_RANK, 1), "g_qn_col": g_qn.reshape(depth, B_HD, 1),
        "g_kn_row": g_kn2.reshape(depth, 1, LANES), "g_kn_rot_row": g_kn2[:, p128].reshape(depth, 1, LANES),
        "w_gate": w_in[:, :, gl[0]:gl[1]].astype(BF16), "b_gate": b_gate.reshape(depth, 1, N_BRANCH * D_MODEL),
        "w_branch": w_branch.astype(BF16), "w_out": w_out.astype(BF16),
        "w_up": w_up.astype(BF16), "w_conv": w_conv, "b_conv": b_conv.reshape(depth, 1, 2 * D_FF),
        "w_down": w_down.astype(BF16),
    }


def kernel(x, c, ctx, c_ctx, w_mod, b_mod, g_norm1, w_in, b_gate, g_q_a, w_q_b, g_kv_a, w_kv_b, g_qn, g_kn, sink,
           w_branch, w_out, g_norm2, w_up, w_conv, b_conv, w_down, g_final):
    b, n_lat, d = x.shape
    n_ctx = ctx.shape[1]
    depth = w_mod.shape[0]
    t = TOKEN_TILE
    assert d == D_MODEL and n_ctx == t and n_lat % t == 0 and n_lat % GRID_W == 0 and b <= MOD_ROWS // 2
    assert n_lat % QUERY_TILE == 0
    assert depth >= 2
    s = n_ctx + n_lat
    n_t = s // t
    scale_a = 1.0 / math.sqrt(A_NOPE + A_ROPE)
    scale_h = 1.0 / math.sqrt(B_HD)

    cvec = jnp.zeros((MOD_ROWS, D_MODEL), F32).at[:b].set(c).at[MOD_ROWS // 2].set(c_ctx)
    mods = _modulation(cvec, w_mod, b_mod)
    tabs = _rope_tables(n_ctx, n_lat)
    wts = _prepare_weights(w_in, b_gate, g_norm1, g_q_a, w_q_b, g_kv_a, w_kv_b, g_qn, g_kn, w_branch, w_out, g_norm2,
                           w_up, w_conv, b_conv, w_down)
    gf = g_final.reshape(1, D_MODEL)
    caps = _window_caps(n_ctx, t)
    sinks = jnp.broadcast_to((sink * LOG2E).reshape(depth, HC, 1, 1), (depth, HC, 1, t)).astype(F32)

    src = (ctx, x)
    for l in range(depth):
        last = l == depth - 1
        qa, ka, va, qb, kb, vb, qc, kc, vc = _project(src, mods, l, wts, tabs, scale_a, scale_h)
        lat = dict(n_keys=s, tq=QUERY_TILE, q0=0, n_tiles=n_lat // QUERY_TILE)
        oa = _attention(qa, ka, va, None, **lat)
        ob = _attention(qb, kb, vb, None, **lat)
        oc = _window_attention(qc, kc, vc, sinks[l], caps, n_ctx=n_ctx)
        if not last:
            cx = dict(n_keys=n_ctx, tq=t, q0=n_t - 1, n_tiles=1)
            oa = _attention(qa, ka, va, None, out=oa, **cx)
            ob = _attention(qb, kb, vb, None, out=ob, **cx)
            oc = _attention(qc, kc, vc, sinks[l], out=oc, **cx)
        tiles = dict(tile0=1, n_tiles=n_t - 1) if last else dict(tile0=0, n_tiles=n_t)
        xc = _merge(src, mods, l, wts, oa, ob, oc, **tiles)
        src = (_conv_ffn(xc, mods, l, wts, gf, final_norm=last, **tiles),)
    return src[0]
```
